```python
import jax, jax.numpy as jnp
from jax import lax
import numpy as np

D_MODEL = 1024
BATCH = 2
SEQ = 16384
DEPTH = 2

N_A_LAYERS = DEPTH // 2
N_B_LAYERS = DEPTH - N_A_LAYERS
N_DENSE = (DEPTH + 1) // 2
N_MOE = DEPTH // 2
POOL_WINDOWS = (2, 4, 8, 16)
N_POOL_GROUPS = 4
POOL_GROUP_DIM = D_MODEL // N_POOL_GROUPS
HEAD_DIM = 128
N_HEADS = D_MODEL // HEAD_DIM
MOBA_BLOCK = 256
MOBA_TOPK = 3
Q_CHUNK = 128
ROPE_THETA = 500000.0
ROPE_DIM = HEAD_DIM // 4
D_FF = 2816
N_EXPERTS = 8
TOP_K_EXPERTS = 2
EXPERT_FF = 3584
PLE_DIM = 256
RMS_EPS = 1e-6
NEG_INF = -1e30

kernel_name = "yoco_pool_moba_moe_trunk"


def rms_norm(x, g):
    xf = x.astype(jnp.float32)
    var = jnp.mean(xf * xf, axis=-1, keepdims=True)
    return (xf * lax.rsqrt(var + RMS_EPS) * g.astype(jnp.float32)).astype(x.dtype)


def partial_rope(x, pos):
    half = ROPE_DIM // 2
    inv_freq = jnp.float32(ROPE_THETA) ** (-(jnp.arange(0, ROPE_DIM, 2, dtype=jnp.float32) / ROPE_DIM))
    ang = pos.astype(jnp.float32)[:, None] * inv_freq[None, :]
    cos, sin = jnp.cos(ang), jnp.sin(ang)
    xf = x.astype(jnp.float32)
    x1, x2, rest = xf[..., :half], xf[..., half:ROPE_DIM], xf[..., ROPE_DIM:]
    out = jnp.concatenate([x1 * cos - x2 * sin, x2 * cos + x1 * sin, rest], axis=-1)
    return out.astype(x.dtype)


def pool_mixer(xn, pool_w, pool_scale):
    B, S, D = xn.shape
    xf = xn.astype(jnp.float32)
    cs = jnp.cumsum(xf, axis=1)
    t = jnp.arange(S)
    outs = []
    for g, w in enumerate(POOL_WINDOWS):
        cg = cs[..., g * POOL_GROUP_DIM:(g + 1) * POOL_GROUP_DIM]
        shifted = jnp.concatenate(
            [jnp.zeros((B, w, POOL_GROUP_DIM), jnp.float32), cg[:, :S - w]], axis=1)
        cnt = jnp.minimum(t + 1, w).astype(jnp.float32)[None, :, None]
        outs.append((cg - shifted) / cnt)
    pooled = jnp.concatenate(outs, axis=-1) - xf
    pooled = pooled.reshape(B, S, N_POOL_GROUPS, POOL_GROUP_DIM).astype(xn.dtype)
    mixed = jnp.einsum('bsgc,gcd->bsgd', pooled, pool_w).reshape(B, S, D)
    return mixed * pool_scale


def shared_kv(h, kv_norm, w_k, w_v):
    B, S, D = h.shape
    kn = rms_norm(h, kv_norm)
    pos = jnp.arange(S)
    k = (kn @ w_k).reshape(B, S, N_HEADS, HEAD_DIM).transpose(0, 2, 1, 3)
    v = (kn @ w_v).reshape(B, S, N_HEADS, HEAD_DIM).transpose(0, 2, 1, 3)
    k = partial_rope(k, pos)
    nb = -(-S // MOBA_BLOCK)
    pad = nb * MOBA_BLOCK - S
    k = jnp.pad(k, ((0, 0), (0, 0), (0, pad), (0, 0)))
    v = jnp.pad(v, ((0, 0), (0, 0), (0, pad), (0, 0)))
    k_blocks = k.reshape(B, N_HEADS, nb, MOBA_BLOCK, HEAD_DIM)
    v_blocks = v.reshape(B, N_HEADS, nb, MOBA_BLOCK, HEAD_DIM)
    k_mean = jnp.mean(k_blocks.astype(jnp.float32), axis=3)
    return k_blocks, v_blocks, k_mean


def moba_attention(q, k_blocks, v_blocks, k_mean):
    B, H, S, hd = q.shape
    nb = k_blocks.shape[2]
    nc = S // Q_CHUNK
    n_sel = min(MOBA_TOPK, nb)
    scale = hd ** -0.5
    q_chunks = q.reshape(B, H, nc, Q_CHUNK, hd).transpose(2, 0, 1, 3, 4)
    bi = jnp.arange(B)[:, None, None]
    hi = jnp.arange(H)[None, :, None]
    blk_ids = jnp.arange(nb)

    def one_chunk(args):
        c, qc = args
        q0 = c * Q_CHUNK
        j = q0 // MOBA_BLOCK
        qf = qc.astype(jnp.float32)
        gate = jnp.einsum('bhqd,bhnd->bhqn', qf, k_mean)
        gate = jnp.where((blk_ids < j)[None, None, None, :], gate, NEG_INF)
        _, sel = lax.top_k(gate, n_sel)
        valid = sel < j
        idx = sel.reshape(B, H, Q_CHUNK * n_sel)
        k_sel = k_blocks[bi, hi, idx].reshape(B, H, Q_CHUNK, n_sel, MOBA_BLOCK, hd)
        v_sel = v_blocks[bi, hi, idx].reshape(B, H, Q_CHUNK, n_sel, MOBA_BLOCK, hd)
        s_sel = jnp.einsum('bhqd,bhqnld->bhqnl', qf, k_sel.astype(jnp.float32)) * scale
        s_sel = jnp.where(valid[..., None], s_sel, NEG_INF)
        s_sel = s_sel.reshape(B, H, Q_CHUNK, n_sel * MOBA_BLOCK)
        k_cur = lax.dynamic_index_in_dim(k_blocks, j, axis=2, keepdims=False)
        v_cur = lax.dynamic_index_in_dim(v_blocks, j, axis=2, keepdims=False)
        s_cur = jnp.einsum('bhqd,bhld->bhql', qf, k_cur.astype(jnp.float32)) * scale
        qpos = q0 + jnp.arange(Q_CHUNK)
        kpos = j * MOBA_BLOCK + jnp.arange(MOBA_BLOCK)
        s_cur = jnp.where((kpos[None, :] <= qpos[:, None])[None, None], s_cur, NEG_INF)
        probs = jax.nn.softmax(jnp.concatenate([s_sel, s_cur], axis=-1), axis=-1)
        p_sel = probs[..., :n_sel * MOBA_BLOCK].reshape(B, H, Q_CHUNK, n_sel, MOBA_BLOCK)
        p_cur = probs[..., n_sel * MOBA_BLOCK:]
        o = (jnp.einsum('bhqnl,bhqnld->bhqd', p_sel, v_sel.astype(jnp.float32))
             + jnp.einsum('bhql,bhld->bhqd', p_cur, v_cur.astype(jnp.float32)))
        return o.astype(q.dtype)

    out = lax.map(one_chunk, (jnp.arange(nc), q_chunks))
    return out.transpose(1, 0, 3, 2, 4).reshape(B, S, H * hd)


def moba_layer(xn, w_q, w_o, k_blocks, v_blocks, k_mean):
    B, S, D = xn.shape
    q = (xn @ w_q).reshape(B, S, N_HEADS, HEAD_DIM).transpose(0, 2, 1, 3)
    q = partial_rope(q, jnp.arange(S))
    return moba_attention(q, k_blocks, v_blocks, k_mean) @ w_o


def swiglu(xn, w1, w3, w2):
    return (jax.nn.silu(xn @ w1) * (xn @ w3)) @ w2


def moe_swiglu(xn, router, w1, w3, w2):
    B, S, D = xn.shape
    t = xn.reshape(B * S, D)
    logits = (t @ router).astype(jnp.float32)
    top_val, top_idx = lax.top_k(logits, TOP_K_EXPERTS)
    top_w = jax.nn.softmax(top_val, axis=-1)
    gates = jnp.sum(jax.nn.one_hot(top_idx, N_EXPERTS, dtype=jnp.float32) * top_w[..., None], axis=1)
    y = jnp.zeros((B * S, D), jnp.float32)
    for e in range(N_EXPERTS):
        y = y + gates[:, e:e + 1] * swiglu(t, w1[e], w3[e], w2[e]).astype(jnp.float32)
    return y.reshape(B, S, D).astype(xn.dtype)


def per_layer_embedding(hn, p_i, gate_w, proj_w):
    return jax.nn.sigmoid(hn @ gate_w) * (p_i @ proj_w)


def setup_inputs(seed: int = 0) -> dict:
    key = jax.random.key(seed)
    ks = jax.random.split(key, 24)
    D = D_MODEL
    f32 = jnp.float32

    def nrm(k, shape, fan_in):
        return jax.random.normal(k, shape, f32) * fan_in ** -0.5

    def gain(k, shape):
        return 1.0 + 0.05 * jax.random.normal(k, shape, f32)

    return {
        "x": jax.random.normal(ks[0], (BATCH, SEQ, D), f32),
        "p": jax.random.normal(ks[1], (DEPTH, BATCH, SEQ, PLE_DIM), f32),
        "pool_norm": gain(ks[2], (N_A_LAYERS, D)),
        "pool_w": nrm(ks[3], (N_A_LAYERS, N_POOL_GROUPS, POOL_GROUP_DIM, POOL_GROUP_DIM), POOL_GROUP_DIM),
        "pool_scale": 0.5 + 0.05 * jax.random.normal(ks[4], (N_A_LAYERS, D), f32),
        "kv_norm": gain(ks[5], (D,)),
        "w_k": nrm(ks[6], (D, D), D),
        "w_v": nrm(ks[7], (D, D), D),
        "attn_norm": gain(ks[8], (N_B_LAYERS, D)),
        "w_q": nrm(ks[9], (N_B_LAYERS, D, D), D),
        "w_o": nrm(ks[10], (N_B_LAYERS, D, D), D),
        "ffn_norm": gain(ks[11], (DEPTH, D)),
        "ffn_w1": nrm(ks[12], (N_DENSE, D, D_FF), D),
        "ffn_w3": nrm(ks[13], (N_DENSE, D, D_FF), D),
        "ffn_w2": nrm(ks[14], (N_DENSE, D_FF, D), D_FF),
        "router": nrm(ks[15], (N_MOE, D, N_EXPERTS), D),
        "exp_w1": nrm(ks[16], (N_MOE, N_EXPERTS, D, EXPERT_FF), D),
        "exp_w3": nrm(ks[17], (N_MOE, N_EXPERTS, D, EXPERT_FF), D),
        "exp_w2": nrm(ks[18], (N_MOE, N_EXPERTS, EXPERT_FF, D), EXPERT_FF),
        "ple_norm": gain(ks[19], (DEPTH, D)),
        "ple_gate": nrm(ks[20], (DEPTH, D, D), D),
        "ple_proj": nrm(ks[21], (DEPTH, PLE_DIM, D), PLE_DIM),
        "final_norm": gain(ks[22], (D,)),
    }


def reference(x, p, pool_norm, pool_w, pool_scale, kv_norm, w_k, w_v, attn_norm, w_q, w_o,
              ffn_norm, ffn_w1, ffn_w3, ffn_w2, router, exp_w1, exp_w3, exp_w2,
              ple_norm, ple_gate, ple_proj, final_norm):
    h = x
    shared = None
    for i in range(DEPTH):
        if i < N_A_LAYERS:
            h = h + pool_mixer(rms_norm(h, pool_norm[i]), pool_w[i], pool_scale[i])
        else:
            b = i - N_A_LAYERS
            if shared is None:
                shared = shared_kv(h, kv_norm, w_k, w_v)
            k_blocks, v_blocks, k_mean = shared
            h = h + moba_layer(rms_norm(h, attn_norm[b]), w_q[b], w_o[b], k_blocks, v_blocks, k_mean)
        hn = rms_norm(h, ffn_norm[i])
        if i % 2 == 0:
            d = i // 2
            h = h + swiglu(hn, ffn_w1[d], ffn_w3[d], ffn_w2[d])
        else:
            m = i // 2
            h = h + moe_swiglu(hn, router[m], exp_w1[m], exp_w3[m], exp_w2[m])
        h = h + per_layer_embedding(rms_norm(h, ple_norm[i]), p[i], ple_gate[i], ple_proj[i])
    return rms_norm(h, final_norm)
```

```python
import functools

import jax
import jax.numpy as jnp
from jax import lax
from jax.experimental import pallas as pl
from jax.experimental.pallas import tpu as pltpu

POOL_WINDOWS = (2, 4, 8, 16)
HEAD_DIM = 128
MOBA_BLOCK = 256
MOBA_TOPK = 3
ROPE_THETA = 500000.0
ROPE_DIM = HEAD_DIM // 4
TOP_K_EXPERTS = 2
RMS_EPS = 1e-6
NEG_INF = -1e30
REMOVED = -3e38

LANES = 128
ROW_TILE = 512
POOL_SUB = 128
VMEM_LIMIT = 56 * 1024 * 1024

bf16 = jnp.bfloat16
f32 = jnp.float32


def _dot(a, b):
    return jnp.dot(a, b, preferred_element_type=f32)


def _rms(x, g):
    var = jnp.mean(x * x, axis=-1, keepdims=True)
    return x * lax.rsqrt(var + RMS_EPS) * g


def _params(*sem):
    return pltpu.CompilerParams(dimension_semantics=sem, vmem_limit_bytes=VMEM_LIMIT)


def _const_spec(shape):
    nd = len(shape)
    return pl.BlockSpec(shape, lambda *_: (0,) * nd)


def _pool_kernel(x_ref, halo_ref, g_ref, pw_ref, ps_ref, o_ref, pooled_ref, *, ts, seq):
    i = pl.program_id(0)
    g = g_ref[...]
    x = x_ref[...]
    xn = _rms(x, g)
    keep = jnp.where((i * ts) % seq == 0, 0.0, 1.0)
    hnb = (_rms(halo_ref[...], g) * keep).astype(bf16)
    xnb = xn.astype(bf16)
    gd = x.shape[1] // len(POOL_WINDOWS)
    r = lax.broadcasted_iota(jnp.int32, (POOL_SUB, 2 * POOL_SUB), 0)
    c = lax.broadcasted_iota(jnp.int32, (POOL_SUB, 2 * POOL_SUB), 1)
    dist = r + POOL_SUB - c
    bands = [((dist >= 0) & (dist < w)).astype(bf16) for w in POOL_WINDOWS]
    rows = lax.broadcasted_iota(jnp.int32, (POOL_SUB, 1), 0)
    for sb in range(ts // POOL_SUB):
        lo, hi = sb * POOL_SUB, (sb + 1) * POOL_SUB
        prev = hnb if sb == 0 else xnb[lo - POOL_SUB:lo]
        ext = jnp.concatenate([prev, xnb[lo:hi]], axis=0)
        tpos = (i * ts + lo) % seq + rows
        for gi, w in enumerate(POOL_WINDOWS):
            cs = slice(gi * gd, (gi + 1) * gd)
            wsum = _dot(bands[gi], ext[:, cs])
            cnt = jnp.minimum(tpos + 1, w).astype(f32)
            pooled_ref[lo:hi, cs] = (wsum / cnt - xn[lo:hi, cs]).astype(bf16)
    for gi in range(len(POOL_WINDOWS)):
        cs = slice(gi * gd, (gi + 1) * gd)
        mixed = _dot(pooled_ref[:, cs], pw_ref[gi])
        o_ref[:, cs] = x[:, cs] + mixed * ps_ref[:, cs]


def _pool_layer(h, norm, pool_w, pool_scale, seq):
    t, d = h.shape
    ts = ROW_TILE
    ng = len(POOL_WINDOWS)
    gd = d // ng
    per = ts // POOL_SUB
    return pl.pallas_call(
        functools.partial(_pool_kernel, ts=ts, seq=seq),
        grid=(t // ts,),
        in_specs=[
            pl.BlockSpec((ts, d), lambda i: (i, 0)),
            pl.BlockSpec((POOL_SUB, d), lambda i: (jnp.maximum(i * per - 1, 0), 0)),
            _const_spec((1, d)),
            _const_spec((ng, gd, gd)),
            _const_spec((1, d)),
        ],
        out_specs=pl.BlockSpec((ts, d), lambda i: (i, 0)),
        out_shape=jax.ShapeDtypeStruct((t, d), f32),
        scratch_shapes=[pltpu.VMEM((ts, d), bf16)],
        compiler_params=_params("parallel"),
        name="pool",
    )(h, h, norm.reshape(1, d), pool_w.astype(bf16), pool_scale.reshape(1, d))


def _swiglu_kernel(h_ref, g_ref, w1_ref, w3_ref, w2_ref, o_ref, hn_ref, acc_ref, *, nf):
    f = pl.program_id(1)

    @pl.when(f == 0)
    def _():
        x = h_ref[...]
        hn_ref[...] = _rms(x, g_ref[...]).astype(bf16)
        acc_ref[...] = x

    hn = hn_ref[...]
    a = _dot(hn, w1_ref[...])
    b = _dot(hn, w3_ref[...])
    acc_ref[...] += _dot((jax.nn.silu(a) * b).astype(bf16), w2_ref[...])

    @pl.when(f == nf - 1)
    def _():
        o_ref[...] = acc_ref[...]


def _ff_chunk(ff, target):
    units = ff // LANES
    best = 1
    for k in range(1, units + 1):
        if units % k == 0 and k * LANES <= target:
            best = k
    return best * LANES


def _swiglu_layer(h, norm, w1, w3, w2):
    t, d = h.shape
    ff = w1.shape[1]
    tm = ROW_TILE
    fc = _ff_chunk(ff, 1408)
    nf = ff // fc
    w1r = w1.astype(bf16).reshape(d, nf, fc).transpose(1, 0, 2)
    w3r = w3.astype(bf16).reshape(d, nf, fc).transpose(1, 0, 2)
    w2r = w2.astype(bf16).reshape(nf, fc, d)
    return pl.pallas_call(
        functools.partial(_swiglu_kernel, nf=nf),
        grid=(t // tm, nf),
        in_specs=[
            pl.BlockSpec((tm, d), lambda i, f: (i, 0)),
            _const_spec((1, d)),
            pl.BlockSpec((None, d, fc), lambda i, f: (f, 0, 0)),
            pl.BlockSpec((None, d, fc), lambda i, f: (f, 0, 0)),
            pl.BlockSpec((None, fc, d), lambda i, f: (f, 0, 0)),
        ],
        out_specs=pl.BlockSpec((tm, d), lambda i, f: (i, 0)),
        out_shape=jax.ShapeDtypeStruct((t, d), f32),
        scratch_shapes=[pltpu.VMEM((tm, d), bf16), pltpu.VMEM((tm, d), f32)],
        compiler_params=_params("parallel", "arbitrary"),
        name="swiglu",
    )(h, norm.reshape(1, d), w1r, w3r, w2r)


def _rope_tables(seq):
    half = ROPE_DIM // 2
    inv_freq = jnp.float32(ROPE_THETA) ** (-(jnp.arange(0, ROPE_DIM, 2, dtype=f32) / ROPE_DIM))
    ang = jnp.arange(seq, dtype=f32)[:, None] * inv_freq[None, :]
    cos, sin = jnp.cos(ang), jnp.sin(ang)
    ones = jnp.ones((seq, HEAD_DIM - ROPE_DIM), f32)
    zeros = jnp.zeros((seq, HEAD_DIM - half), f32)
    cos_t = jnp.concatenate([cos, cos, ones], axis=1)
    up_t = jnp.concatenate([-sin, zeros], axis=1)
    dn_t = jnp.concatenate([jnp.zeros((seq, half), f32), sin, jnp.zeros((seq, HEAD_DIM - ROPE_DIM), f32)], axis=1)
    return cos_t, up_t, dn_t


def _rope(xh, cos_t, up_t, dn_t):
    half = ROPE_DIM // 2
    return (xh * cos_t + pltpu.roll(xh, HEAD_DIM - half, 1) * up_t + pltpu.roll(xh, half, 1) * dn_t)


def _ple_qkv_kernel(h_ref, p_ref, pn_ref, wg_ref, wp_ref, kvn_ref, wk_ref, wv_ref, an_ref, wq_ref,
                    cos_ref, up_ref, dn_ref, h_out, q_out, k_out, v_out, km_out):
    h = h_ref[...]
    hn = _rms(h, pn_ref[...]).astype(bf16)
    h3 = h + jax.nn.sigmoid(_dot(hn, wg_ref[...])) * _dot(p_ref[...].astype(bf16), wp_ref[...])
    h_out[...] = h3
    base = h3 * lax.rsqrt(jnp.mean(h3 * h3, axis=-1, keepdims=True) + RMS_EPS)
    kn = (base * kvn_ref[...]).astype(bf16)
    qn = (base * an_ref[...]).astype(bf16)
    v_out[...] = _dot(kn, wv_ref[...]).astype(bf16)
    k = _dot(kn, wk_ref[...])
    q = _dot(qn, wq_ref[...])
    cos_t, up_t, dn_t = cos_ref[...], up_ref[...], dn_ref[...]
    tm, d = h.shape
    sub = km_out.shape[0] // (tm // MOBA_BLOCK)
    for hh in range(d // HEAD_DIM):
        cs = slice(hh * HEAD_DIM, (hh + 1) * HEAD_DIM)
        q_out[:, cs] = _rope(q[:, cs], cos_t, up_t, dn_t)
        kr = _rope(k[:, cs], cos_t, up_t, dn_t)
        k_out[:, cs] = kr.astype(bf16)
        for bi in range(tm // MOBA_BLOCK):
            m = jnp.mean(kr[bi * MOBA_BLOCK:(bi + 1) * MOBA_BLOCK], axis=0, keepdims=True)
            km_out[bi * sub:(bi + 1) * sub, cs] = jnp.broadcast_to(m, (sub, HEAD_DIM))


def _ple_qkv_layer(h, p_i, ple_norm, ple_gate, ple_proj, kv_norm, w_k, w_v, attn_norm, w_q, seq):
    t, d = h.shape
    pd = p_i.shape[1]
    tm = ROW_TILE
    sub = 8
    cos_t, up_t, dn_t = _rope_tables(seq)
    tiles_per_seq = seq // tm
    row = lambda i: (i, 0)
    tab = pl.BlockSpec((tm, HEAD_DIM), lambda i: (i % tiles_per_seq, 0))
    nkm = t // MOBA_BLOCK * sub
    outs = pl.pallas_call(
        _ple_qkv_kernel,
        grid=(t // tm,),
        in_specs=[
            pl.BlockSpec((tm, d), row), pl.BlockSpec((tm, pd), row),
            _const_spec((1, d)), _const_spec((d, d)), _const_spec((pd, d)),
            _const_spec((1, d)), _const_spec((d, d)), _const_spec((d, d)),
            _const_spec((1, d)), _const_spec((d, d)),
            tab, tab, tab,
        ],
        out_specs=[
            pl.BlockSpec((tm, d), row), pl.BlockSpec((tm, d), row),
            pl.BlockSpec((tm, d), row), pl.BlockSpec((tm, d), row),
            pl.BlockSpec((tm // MOBA_BLOCK * sub, d), row),
        ],
        out_shape=[
            jax.ShapeDtypeStruct((t, d), f32), jax.ShapeDtypeStruct((t, d), f32),
            jax.ShapeDtypeStruct((t, d), bf16), jax.ShapeDtypeStruct((t, d), bf16),
            jax.ShapeDtypeStruct((nkm, d), f32),
        ],
        compiler_params=_params("parallel"),
        name="ple_qkv",
    )(h, p_i, ple_norm.reshape(1, d), ple_gate.astype(bf16), ple_proj.astype(bf16),
      kv_norm.reshape(1, d), w_k.astype(bf16), w_v.astype(bf16),
      attn_norm.reshape(1, d), w_q.astype(bf16), cos_t, up_t, dn_t)
    h3, q, k, v, km = outs
    return h3, q, k, v, km.reshape(t // MOBA_BLOCK, sub, d)[:, 0, :]


def _attn_kernel(q_ref, k_ref, v_ref, km_ref, o_ref):
    j = pl.program_id(2)
    qf = q_ref[...]
    nb = km_ref.shape[0]
    bs = qf.shape[0]
    gate = lax.dot_general(qf, km_ref[...], (((1,), (1,)), ((), ())),
                           precision=lax.Precision.HIGHEST, preferred_element_type=f32)
    blk = lax.broadcasted_iota(jnp.int32, (bs, nb), 1).astype(f32)
    jf = j.astype(f32)
    cand = jnp.where(blk < jf, gate, NEG_INF)
    picks = []
    for _ in range(min(MOBA_TOPK, nb)):
        mx = jnp.max(cand, axis=1, keepdims=True)
        pick = jnp.min(jnp.where(cand == mx, blk, float(nb)), axis=1, keepdims=True)
        picks.append(pick)
        cand = jnp.where(blk == pick, REMOVED, cand)

    qb = (qf * (HEAD_DIM ** -0.5)).astype(bf16)

    def scores(n):
        start = pl.multiple_of(n * bs, bs)
        kn = k_ref[pl.ds(start, bs), :]
        vn = v_ref[pl.ds(start, bs), :]
        s = lax.dot_general(qb, kn, (((1,), (1,)), ((), ())), preferred_element_type=f32)
        return s, vn

    s, vn = scores(j)
    rq = lax.broadcasted_iota(jnp.int32, (bs, bs), 0)
    ck = lax.broadcasted_iota(jnp.int32, (bs, bs), 1)
    s = jnp.where(ck <= rq, s, NEG_INF)
    m0 = jnp.max(s, axis=1, keepdims=True)
    p = jnp.exp(s - m0)
    l0 = jnp.sum(p, axis=1, keepdims=True)
    acc0 = _dot(p.astype(bf16), vn)

    def body(n, carry):
        m, l, acc = carry
        s, vn = scores(n)
        nf_ = n.astype(f32)
        sel = picks[0] == nf_
        for pk in picks[1:]:
            sel = sel | (pk == nf_)
        s = jnp.where(sel, s, NEG_INF)
        m_new = jnp.maximum(m, jnp.max(s, axis=1, keepdims=True))
        alpha = jnp.exp(m - m_new)
        p = jnp.exp(s - m_new)
        l = alpha * l + jnp.sum(p, axis=1, keepdims=True)
        acc = alpha * acc + _dot(p.astype(bf16), vn)
        return m_new, l, acc

    _, l, acc = lax.fori_loop(0, j, body, (m0, l0, acc0))
    o_ref[...] = (acc / l).astype(o_ref.dtype)


def _attention(q, k, v, km, batch, seq):
    t, d = q.shape
    nh = d // HEAD_DIM
    nb = seq // MOBA_BLOCK
    return pl.pallas_call(
        _attn_kernel,
        grid=(batch, nh, nb),
        in_specs=[
            pl.BlockSpec((MOBA_BLOCK, HEAD_DIM), lambda b, h, j: (b * nb + j, h)),
            pl.BlockSpec((seq, HEAD_DIM), lambda b, h, j: (b, h)),
            pl.BlockSpec((seq, HEAD_DIM), lambda b, h, j: (b, h)),
            pl.BlockSpec((nb, HEAD_DIM), lambda b, h, j: (b, h)),
        ],
        out_specs=pl.BlockSpec((MOBA_BLOCK, HEAD_DIM), lambda b, h, j: (b * nb + j, h)),
        out_shape=jax.ShapeDtypeStruct((t, d), bf16),
        compiler_params=_params("parallel", "parallel", "arbitrary"),
        name="attn",
    )(q, k, v, km)


def _oproj_router_kernel(h_ref, a_ref, wo_ref, g_ref, r_ref, h_out, hn_out, route_out, *, n_exp):
    h4 = h_ref[...] + _dot(a_ref[...], wo_ref[...])
    h_out[...] = h4
    hn = _rms(h4, g_ref[...])
    hn_out[...] = hn
    logits = jnp.dot(hn, r_ref[...], precision=lax.Precision.HIGHEST, preferred_element_type=f32)
    lane = lax.broadcasted_iota(jnp.int32, logits.shape, 1).astype(f32)
    cand = jnp.where(lane < n_exp, logits, NEG_INF)
    m1 = jnp.max(cand, axis=1, keepdims=True)
    i1 = jnp.min(jnp.where(cand == m1, lane, float(LANES)), axis=1, keepdims=True)
    cand = jnp.where(lane == i1, REMOVED, cand)
    m2 = jnp.max(cand, axis=1, keepdims=True)
    i2 = jnp.min(jnp.where(cand == m2, lane, float(LANES)), axis=1, keepdims=True)
    e2 = jnp.exp(m2 - m1)
    den = 1.0 + e2
    route_out[...] = jnp.where(lane == 0, i1, jnp.where(lane == 1, i2, jnp.where(
        lane == 2, 1.0 / den, jnp.where(lane == 3, e2 / den, 0.0))))


def _oproj_router_layer(h, attn, w_o, norm, router):
    t, d = h.shape
    tm = ROW_TILE
    n_exp = router.shape[1]
    r_pad = jnp.zeros((d, LANES), f32).at[:, :n_exp].set(router)
    row = lambda i: (i, 0)
    return pl.pallas_call(
        functools.partial(_oproj_router_kernel, n_exp=n_exp),
        grid=(t // tm,),
        in_specs=[pl.BlockSpec((tm, d), row), pl.BlockSpec((tm, d), row), _const_spec((d, d)),
                  _const_spec((1, d)), _const_spec((d, LANES))],
        out_specs=[pl.BlockSpec((tm, d), row), pl.BlockSpec((tm, d), row), pl.BlockSpec((tm, LANES), row)],
        out_shape=[jax.ShapeDtypeStruct((t, d), f32), jax.ShapeDtypeStruct((t, d), f32),
                   jax.ShapeDtypeStruct((t, LANES), f32)],
        compiler_params=_params("parallel"),
        name="oproj_router",
    )(h, attn, w_o.astype(bf16), norm.reshape(1, d), r_pad)


def _expert_kernel(te_ref, tv_ref, nv_ref, src_ref, dst_ref, hn_hbm, w1_ref, w3_ref, w2_ref, out_hbm,
                   xg_ref, xb_ref, acc_ref, gsem, ssem, *, tm, nf):
    i = pl.program_id(0)
    f = pl.program_id(1)
    valid = tv_ref[i] > 0

    def row_in(r):
        return pltpu.make_async_copy(hn_hbm.at[pl.ds(src_ref[0, 0, r], 1), :], xg_ref.at[pl.ds(r, 1), :], gsem)

    def row_out(r):
        return pltpu.make_async_copy(acc_ref.at[pl.ds(r, 1), :], out_hbm.at[pl.ds(dst_ref[0, 0, r], 1), :], ssem)

    @pl.when(valid & (f == 0))
    def _():
        lax.fori_loop(0, tm, lambda r, c: (row_in(r).start(), c)[1], 0)
        lax.fori_loop(0, tm, lambda r, c: (row_in(r).wait(), c)[1], 0)
        xb_ref[...] = xg_ref[...].astype(bf16)
        acc_ref[...] = jnp.zeros_like(acc_ref)

    @pl.when(valid)
    def _():
        x = xb_ref[...]
        a = _dot(x, w1_ref[...])
        b = _dot(x, w3_ref[...])
        acc_ref[...] += _dot((jax.nn.silu(a) * b).astype(bf16), w2_ref[...])

    @pl.when(valid & (f == nf - 1))
    def _():
        n = nv_ref[i]
        lax.fori_loop(0, n, lambda r, c: (row_out(r).start(), c)[1], 0)
        lax.fori_loop(0, n, lambda r, c: (row_out(r).wait(), c)[1], 0)


def _route_tables(e12, n_exp, tm):
    nslots = e12.size
    ef = e12.reshape(-1)
    order = jnp.argsort(ef, stable=True).astype(jnp.int32)
    counts = jnp.sum(ef[:, None] == jnp.arange(n_exp, dtype=jnp.int32)[None, :], axis=0, dtype=jnp.int32)
    tiles_e = (counts + tm - 1) // tm
    tile_end = jnp.cumsum(tiles_e)
    tile_start = tile_end - tiles_e
    group_start = jnp.cumsum(counts) - counts
    nt = nslots // tm + n_exp
    total = tile_end[-1]
    ti = jnp.arange(nt, dtype=jnp.int32)
    tv = (ti < total).astype(jnp.int32)
    tc = jnp.minimum(ti, total - 1)
    te = jnp.minimum(jnp.searchsorted(tile_end, tc, side="right").astype(jnp.int32), n_exp - 1)
    rank0 = (tc - tile_start[te]) * tm
    nv = jnp.clip(counts[te] - rank0, 0, tm) * tv
    rank = rank0[:, None] + jnp.arange(tm, dtype=jnp.int32)[None, :]
    real = (rank < counts[te][:, None]) & (tv[:, None] > 0)
    sidx = jnp.clip(group_start[te][:, None] + rank, 0, nslots - 1)
    slot = jnp.where(real, order[sidx], 0)
    src = (slot // TOP_K_EXPERTS).reshape(nt, 1, tm)
    dst = slot.reshape(nt, 1, tm)
    return te, tv, nv, src, dst, nt


def _experts_layer(hn, e12, w1, w3, w2):
    t, d = hn.shape
    n_exp, _, ff = w1.shape
    tm = ROW_TILE
    fc = _ff_chunk(ff, 896)
    nf = ff // fc
    te, tv, nv, src, dst, nt = _route_tables(e12, n_exp, tm)
    w1r = w1.astype(bf16).reshape(n_exp, d, nf, fc).transpose(0, 2, 1, 3)
    w3r = w3.astype(bf16).reshape(n_exp, d, nf, fc).transpose(0, 2, 1, 3)
    w2r = w2.astype(bf16).reshape(n_exp, nf, fc, d)

    def wmap(i, f, te_r, tv_r, nv_r):
        return (te_r[i], jnp.where(tv_r[i] > 0, f, nf - 1), 0, 0)

    idx_spec = pl.BlockSpec((1, 1, tm), lambda i, f, *_: (i, 0, 0), memory_space=pltpu.SMEM)
    grid_spec = pltpu.PrefetchScalarGridSpec(
        num_scalar_prefetch=3,
        grid=(nt, nf),
        in_specs=[
            idx_spec, idx_spec,
            pl.BlockSpec(memory_space=pl.ANY),
            pl.BlockSpec((None, None, d, fc), wmap),
            pl.BlockSpec((None, None, d, fc), wmap),
            pl.BlockSpec((None, None, fc, d), wmap),
        ],
        out_specs=pl.BlockSpec(memory_space=pl.ANY),
        scratch_shapes=[pltpu.VMEM((tm, d), f32), pltpu.VMEM((tm, d), bf16), pltpu.VMEM((tm, d), f32),
                        pltpu.SemaphoreType.DMA(()), pltpu.SemaphoreType.DMA(())],
    )
    return pl.pallas_call(
        functools.partial(_expert_kernel, tm=tm, nf=nf),
        grid_spec=grid_spec,
        out_shape=jax.ShapeDtypeStruct((t * TOP_K_EXPERTS, d), f32),
        compiler_params=_params("arbitrary", "arbitrary"),
        name="experts",
    )(te, tv, nv, src, dst, hn, w1r, w3r, w2r)


def _final_kernel(h_ref, y_ref, route_ref, p_ref, pn_ref, wg_ref, wp_ref, fn_ref, o_ref):
    d = h_ref.shape[1]
    route = route_ref[...]
    h5 = h_ref[...] + route[:, 2:3] * y_ref[:, :d] + route[:, 3:4] * y_ref[:, d:]
    hn = _rms(h5, pn_ref[...]).astype(bf16)
    h6 = h5 + jax.nn.sigmoid(_dot(hn, wg_ref[...])) * _dot(p_ref[...].astype(bf16), wp_ref[...])
    o_ref[...] = _rms(h6, fn_ref[...])


def _final_layer(h, y2, route, p_i, ple_norm, ple_gate, ple_proj, final_norm):
    t, d = h.shape
    pd = p_i.shape[1]
    tm = ROW_TILE
    row = lambda i: (i, 0)
    return pl.pallas_call(
        _final_kernel,
        grid=(t // tm,),
        in_specs=[pl.BlockSpec((tm, d), row), pl.BlockSpec((tm, TOP_K_EXPERTS * d), row),
                  pl.BlockSpec((tm, LANES), row), pl.BlockSpec((tm, pd), row),
                  _const_spec((1, d)), _const_spec((d, d)), _const_spec((pd, d)), _const_spec((1, d))],
        out_specs=pl.BlockSpec((tm, d), row),
        out_shape=jax.ShapeDtypeStruct((t, d), f32),
        compiler_params=_params("parallel"),
        name="final",
    )(h, y2, route, p_i, ple_norm.reshape(1, d), ple_gate.astype(bf16), ple_proj.astype(bf16),
      final_norm.reshape(1, d))


def kernel(x, p, pool_norm, pool_w, pool_scale, kv_norm, w_k, w_v, attn_norm, w_q, w_o, ffn_norm, ffn_w1, ffn_w3, ffn_w2, router, exp_w1, exp_w3, exp_w2, ple_norm, ple_gate, ple_proj, final_norm):
    batch, seq, d = x.shape
    t = batch * seq
    assert seq % ROW_TILE == 0 and ROW_TILE % MOBA_BLOCK == 0 and d % HEAD_DIM == 0
    assert p.shape[0] == 2 and router.shape[2] <= LANES
    h = x.reshape(t, d)
    pf = p.reshape(p.shape[0], t, p.shape[-1])

    h = _pool_layer(h, pool_norm[0], pool_w[0], pool_scale[0], seq)
    h = _swiglu_layer(h, ffn_norm[0], ffn_w1[0], ffn_w3[0], ffn_w2[0])
    h, q, k, v, km = _ple_qkv_layer(h, pf[0], ple_norm[0], ple_gate[0], ple_proj[0],
                                    kv_norm, w_k, w_v, attn_norm[0], w_q[0], seq)
    attn = _attention(q, k, v, km, batch, seq)
    h, hn, route = _oproj_router_layer(h, attn, w_o[0], ffn_norm[1], router[0])
    e12 = route[:, :TOP_K_EXPERTS].astype(jnp.int32)
    y = _experts_layer(hn, e12, exp_w1[0], exp_w3[0], exp_w2[0])
    out = _final_layer(h, y.reshape(t, TOP_K_EXPERTS * d), route, pf[1], ple_norm[1], ple_gate[1],
                       ple_proj[1], final_norm)
    return out.reshape(batch, seq, d)
```

```python
import functools

import jax
import jax.numpy as jnp
from jax import lax
from jax.experimental import pallas as pl
from jax.experimental.pallas import tpu as pltpu

POOL_WINDOWS = (2, 4, 8, 16)
HEAD_DIM = 128
MOBA_BLOCK = 256
MOBA_TOPK = 3
ROPE_THETA = 500000.0
ROPE_DIM = HEAD_DIM // 4
TOP_K_EXPERTS = 2
RMS_EPS = 1e-6
NEG_INF = -1e30
REMOVED = -3e38

LANES = 128
ROW_TILE = 512
POOL_SUB = 128
ATTN_HEADS_PER_STEP = 4
LOG2E = 1.4426950408889634
DMA_UNROLL = 8
VMEM_LIMIT = 56 * 1024 * 1024

bf16 = jnp.bfloat16
f32 = jnp.float32


def _dot(a, b):
    return jnp.dot(a, b, preferred_element_type=f32)


def _rms(x, g):
    var = jnp.mean(x * x, axis=-1, keepdims=True)
    return x * lax.rsqrt(var + RMS_EPS) * g


def _params(*sem):
    return pltpu.CompilerParams(dimension_semantics=sem, vmem_limit_bytes=VMEM_LIMIT)


def _const_spec(shape):
    nd = len(shape)
    return pl.BlockSpec(shape, lambda *_: (0,) * nd)


def _pool_kernel(x_ref, halo_ref, g_ref, pw_ref, ps_ref, o_ref, pooled_ref, *, ts, seq):
    i = pl.program_id(0)
    g = g_ref[...]
    x = x_ref[...]
    xn = _rms(x, g)
    keep = jnp.where((i * ts) % seq == 0, 0.0, 1.0)
    hnb = (_rms(halo_ref[...], g) * keep).astype(bf16)
    xnb = xn.astype(bf16)
    gd = x.shape[1] // len(POOL_WINDOWS)
    r = lax.broadcasted_iota(jnp.int32, (POOL_SUB, 2 * POOL_SUB), 0)
    c = lax.broadcasted_iota(jnp.int32, (POOL_SUB, 2 * POOL_SUB), 1)
    dist = r + POOL_SUB - c
    bands = [((dist >= 0) & (dist < w)).astype(bf16) for w in POOL_WINDOWS]
    rows = lax.broadcasted_iota(jnp.int32, (POOL_SUB, 1), 0)
    for sb in range(ts // POOL_SUB):
        lo, hi = sb * POOL_SUB, (sb + 1) * POOL_SUB
        prev = hnb if sb == 0 else xnb[lo - POOL_SUB:lo]
        ext = jnp.concatenate([prev, xnb[lo:hi]], axis=0)
        tpos = (i * ts + lo) % seq + rows
        for gi, w in enumerate(POOL_WINDOWS):
            cs = slice(gi * gd, (gi + 1) * gd)
            wsum = _dot(bands[gi], ext[:, cs])
            cnt = jnp.minimum(tpos + 1, w).astype(f32)
            pooled_ref[lo:hi, cs] = (wsum / cnt - xn[lo:hi, cs]).astype(bf16)
    for gi in range(len(POOL_WINDOWS)):
        cs = slice(gi * gd, (gi + 1) * gd)
        mixed = _dot(pooled_ref[:, cs], pw_ref[gi])
        o_ref[:, cs] = x[:, cs] + mixed * ps_ref[:, cs]


def _pool_layer(h, norm, pool_w, pool_scale, seq):
    t, d = h.shape
    ts = ROW_TILE
    ng = len(POOL_WINDOWS)
    gd = d // ng
    per = ts // POOL_SUB
    return pl.pallas_call(
        functools.partial(_pool_kernel, ts=ts, seq=seq),
        grid=(t // ts,),
        in_specs=[
            pl.BlockSpec((ts, d), lambda i: (i, 0)),
            pl.BlockSpec((POOL_SUB, d), lambda i: (jnp.maximum(i * per - 1, 0), 0)),
            _const_spec((1, d)),
            _const_spec((ng, gd, gd)),
            _const_spec((1, d)),
        ],
        out_specs=pl.BlockSpec((ts, d), lambda i: (i, 0)),
        out_shape=jax.ShapeDtypeStruct((t, d), f32),
        scratch_shapes=[pltpu.VMEM((ts, d), bf16)],
        compiler_params=_params("parallel"),
        name="pool",
    )(h, h, norm.reshape(1, d), pool_w.astype(bf16), pool_scale.reshape(1, d))


def _swiglu_kernel(h_ref, g_ref, w1_ref, w3_ref, w2_ref, o_ref, hn_ref, acc_ref, *, nf):
    f = pl.program_id(1)

    @pl.when(f == 0)
    def _():
        x = h_ref[...]
        hn_ref[...] = _rms(x, g_ref[...]).astype(bf16)
        acc_ref[...] = x

    hn = hn_ref[...]
    a = _dot(hn, w1_ref[...])
    b = _dot(hn, w3_ref[...])
    acc_ref[...] += _dot((jax.nn.silu(a) * b).astype(bf16), w2_ref[...])

    @pl.when(f == nf - 1)
    def _():
        o_ref[...] = acc_ref[...]


def _ff_chunk(ff, target):
    units = ff // LANES
    best = 1
    for k in range(1, units + 1):
        if units % k == 0 and k * LANES <= target:
            best = k
    return best * LANES


def _swiglu_layer(h, norm, w1, w3, w2):
    t, d = h.shape
    ff = w1.shape[1]
    tm = ROW_TILE
    fc = _ff_chunk(ff, 1408)
    nf = ff // fc
    w1r = w1.astype(bf16).reshape(d, nf, fc).transpose(1, 0, 2)
    w3r = w3.astype(bf16).reshape(d, nf, fc).transpose(1, 0, 2)
    w2r = w2.astype(bf16).reshape(nf, fc, d)
    return pl.pallas_call(
        functools.partial(_swiglu_kernel, nf=nf),
        grid=(t // tm, nf),
        in_specs=[
            pl.BlockSpec((tm, d), lambda i, f: (i, 0)),
            _const_spec((1, d)),
            pl.BlockSpec((None, d, fc), lambda i, f: (f, 0, 0)),
            pl.BlockSpec((None, d, fc), lambda i, f: (f, 0, 0)),
            pl.BlockSpec((None, fc, d), lambda i, f: (f, 0, 0)),
        ],
        out_specs=pl.BlockSpec((tm, d), lambda i, f: (i, 0)),
        out_shape=jax.ShapeDtypeStruct((t, d), f32),
        scratch_shapes=[pltpu.VMEM((tm, d), bf16), pltpu.VMEM((tm, d), f32)],
        compiler_params=_params("parallel", "arbitrary"),
        name="swiglu",
    )(h, norm.reshape(1, d), w1r, w3r, w2r)


def _rope_tables(seq):
    half = ROPE_DIM // 2
    inv_freq = jnp.float32(ROPE_THETA) ** (-(jnp.arange(0, ROPE_DIM, 2, dtype=f32) / ROPE_DIM))
    ang = jnp.arange(seq, dtype=f32)[:, None] * inv_freq[None, :]
    cos, sin = jnp.cos(ang), jnp.sin(ang)
    ones = jnp.ones((seq, HEAD_DIM - ROPE_DIM), f32)
    zeros = jnp.zeros((seq, HEAD_DIM - half), f32)
    cos_t = jnp.concatenate([cos, cos, ones], axis=1)
    up_t = jnp.concatenate([-sin, zeros], axis=1)
    dn_t = jnp.concatenate([jnp.zeros((seq, half), f32), sin, jnp.zeros((seq, HEAD_DIM - ROPE_DIM), f32)], axis=1)
    return cos_t, up_t, dn_t


def _rope(xh, cos_t, up_t, dn_t):
    half = ROPE_DIM // 2
    return (xh * cos_t + pltpu.roll(xh, HEAD_DIM - half, 1) * up_t + pltpu.roll(xh, half, 1) * dn_t)


def _ple_qkv_kernel(h_ref, p_ref, pn_ref, wg_ref, wp_ref, kvn_ref, wk_ref, wv_ref, an_ref, wq_ref,
                    cos_ref, up_ref, dn_ref, h_out, q_out, k_out, v_out, km_out):
    h = h_ref[...]
    hn = _rms(h, pn_ref[...]).astype(bf16)
    h3 = h + jax.nn.sigmoid(_dot(hn, wg_ref[...])) * _dot(p_ref[...].astype(bf16), wp_ref[...])
    h_out[...] = h3
    base = h3 * lax.rsqrt(jnp.mean(h3 * h3, axis=-1, keepdims=True) + RMS_EPS)
    kn = (base * kvn_ref[...]).astype(bf16)
    qn = (base * an_ref[...]).astype(bf16)
    v = _dot(kn, wv_ref[...])
    for bi in range(h.shape[0] // MOBA_BLOCK):
        v_out[bi] = v[bi * MOBA_BLOCK:(bi + 1) * MOBA_BLOCK].T.astype(bf16)
    k = _dot(kn, wk_ref[...])
    q = _dot(qn, wq_ref[...])
    cos_t, up_t, dn_t = cos_ref[...], up_ref[...], dn_ref[...]
    tm, d = h.shape
    sub = km_out.shape[0] // (tm // MOBA_BLOCK)
    for hh in range(d // HEAD_DIM):
        cs = slice(hh * HEAD_DIM, (hh + 1) * HEAD_DIM)
        q_out[:, cs] = _rope(q[:, cs], cos_t, up_t, dn_t)
        kr = _rope(k[:, cs], cos_t, up_t, dn_t)
        k_out[:, cs] = kr.astype(bf16)
        for bi in range(tm // MOBA_BLOCK):
            m = jnp.mean(kr[bi * MOBA_BLOCK:(bi + 1) * MOBA_BLOCK], axis=0, keepdims=True)
            km_out[bi * sub:(bi + 1) * sub, cs] = jnp.broadcast_to(m, (sub, HEAD_DIM))


def _ple_qkv_layer(h, p_i, ple_norm, ple_gate, ple_proj, kv_norm, w_k, w_v, attn_norm, w_q, seq):
    t, d = h.shape
    pd = p_i.shape[1]
    tm = ROW_TILE
    sub = 8
    cos_t, up_t, dn_t = _rope_tables(seq)
    tiles_per_seq = seq // tm
    row = lambda i: (i, 0)
    tab = pl.BlockSpec((tm, HEAD_DIM), lambda i: (i % tiles_per_seq, 0))
    nkm = t // MOBA_BLOCK * sub
    outs = pl.pallas_call(
        _ple_qkv_kernel,
        grid=(t // tm,),
        in_specs=[
            pl.BlockSpec((tm, d), row), pl.BlockSpec((tm, pd), row),
            _const_spec((1, d)), _const_spec((d, d)), _const_spec((pd, d)),
            _const_spec((1, d)), _const_spec((d, d)), _const_spec((d, d)),
            _const_spec((1, d)), _const_spec((d, d)),
            tab, tab, tab,
        ],
        out_specs=[
            pl.BlockSpec((tm, d), row), pl.BlockSpec((tm, d), row),
            pl.BlockSpec((tm, d), row),
            pl.BlockSpec((tm // MOBA_BLOCK, d, MOBA_BLOCK), lambda i: (i, 0, 0)),
            pl.BlockSpec((tm // MOBA_BLOCK * sub, d), row),
        ],
        out_shape=[
            jax.ShapeDtypeStruct((t, d), f32), jax.ShapeDtypeStruct((t, d), f32),
            jax.ShapeDtypeStruct((t, d), bf16),
            jax.ShapeDtypeStruct((t // MOBA_BLOCK, d, MOBA_BLOCK), bf16),
            jax.ShapeDtypeStruct((nkm, d), f32),
        ],
        compiler_params=_params("parallel"),
        name="ple_qkv",
    )(h, p_i, ple_norm.reshape(1, d), ple_gate.astype(bf16), ple_proj.astype(bf16),
      kv_norm.reshape(1, d), w_k.astype(bf16), w_v.astype(bf16),
      attn_norm.reshape(1, d), w_q.astype(bf16), cos_t, up_t, dn_t)
    h3, q, k, v, km = outs
    return h3, q, k, v, km.reshape(t // MOBA_BLOCK, sub, d)[:, 0, :]


def _attn_kernel(q_ref, k_ref, vt_ref, km_ref, o_ref, qa_ref, sa_ref, sb_ref, pa_ref, pb_ref, acc_ref, *, hg):
    j = pl.program_id(2)
    nb = km_ref.shape[0]
    bs = q_ref.shape[0]
    hd = HEAD_DIM
    qscale = (hd ** -0.5) * LOG2E
    blk = lax.broadcasted_iota(jnp.int32, (nb, bs), 0).astype(f32)
    jf = j.astype(f32)
    krow = lax.broadcasted_iota(jnp.int32, (bs, bs), 0)
    qcol = lax.broadcasted_iota(jnp.int32, (bs, bs), 1)
    start_j = pl.multiple_of(j * bs, bs)

    carry0 = []
    for h in range(hg):
        cs = slice(h * hd, (h + 1) * hd)
        qf = q_ref[:, cs]
        gate = lax.dot_general(km_ref[:, cs], qf, (((1,), (1,)), ((), ())),
                               precision=lax.Precision.HIGHEST, preferred_element_type=f32)
        cand = jnp.where(blk < jf, gate, NEG_INF)
        sel = blk < 0.0
        for _ in range(min(MOBA_TOPK, nb)):
            mx = jnp.max(cand, axis=0, keepdims=True)
            pick = jnp.min(jnp.where(cand == mx, blk, float(nb)), axis=0, keepdims=True)
            hit = blk == pick
            sel = sel | hit
            cand = jnp.where(hit, REMOVED, cand)
        bias = jnp.where(sel & (blk < jf), 0.0, NEG_INF)
        if nb < hd:
            bias = jnp.concatenate([bias, jnp.zeros((hd - nb, bs), f32)], axis=0)
        qbt = (qf * qscale).T.astype(bf16)
        qa_ref[h] = jnp.concatenate([qbt, bias.astype(bf16)], axis=0)

        s = _dot(k_ref[pl.ds(start_j, bs), cs], qbt)
        s = jnp.where(krow <= qcol, s, NEG_INF)
        m0 = jnp.max(s, axis=0, keepdims=True)
        p = jnp.exp2(s - m0)
        l0 = jnp.sum(p, axis=0, keepdims=True)
        acc0 = _dot(vt_ref[j, cs, :], p.astype(bf16))
        carry0 += [m0, l0, acc0]

    lane = lax.broadcasted_iota(jnp.int32, (bs, hd), 1)

    def block_scores(n, h):
        start = pl.multiple_of(n * bs, bs)
        onehot = (lane == n).astype(bf16)
        ka = jnp.concatenate([k_ref[pl.ds(start, bs), h * hd:(h + 1) * hd], onehot], axis=1)
        return _dot(ka, qa_ref[h])

    def weighted_values(n, h, pb):
        return _dot(vt_ref[n, h * hd:(h + 1) * hd, :], pb)

    def stage(n, s_in, s_out, p_prev, p_out, stats):
        nn = jnp.minimum(n + 1, nb - 1)
        for h in range(hg):
            s_out[h] = block_scores(nn, h)
        prev = jnp.maximum(n - 1, 0)
        new = []
        for h in range(hg):
            m, l, alpha_p = stats[3 * h:3 * h + 3]
            pv = weighted_values(prev, h, p_prev[h])
            s = s_in[h]
            m_new = jnp.maximum(m, jnp.max(s, axis=0, keepdims=True))
            alpha = jnp.exp2(m - m_new)
            p = jnp.exp2(s - m_new)
            l = alpha * l + jnp.sum(p, axis=0, keepdims=True)
            p_out[h] = p.astype(bf16)
            acc_ref[h] = alpha_p * acc_ref[h] + pv
            new += [m_new, l, alpha]
        return new

    def body(i, stats):
        stats = stage(2 * i, sa_ref, sb_ref, pb_ref, pa_ref, stats)
        stats = stage(2 * i + 1, sb_ref, sa_ref, pa_ref, pb_ref, stats)
        return tuple(stats)

    init = []
    for h in range(hg):
        m0, l0, acc0 = carry0[3 * h:3 * h + 3]
        acc_ref[h] = acc0
        sa_ref[h] = block_scores(0, h)
        pb_ref[h] = jnp.zeros((bs, bs), bf16)
        init += [m0, l0, jnp.ones_like(m0)]
    trips = (j + 1) // 2
    res = lax.fori_loop(0, trips, body, tuple(init))
    last = jnp.maximum(2 * trips - 1, 0)
    for h in range(hg):
        _, l, alpha_p = res[3 * h:3 * h + 3]
        acc = alpha_p * acc_ref[h] + weighted_values(last, h, pb_ref[h])
        o_ref[:, h * hd:(h + 1) * hd] = (acc / l).T.astype(o_ref.dtype)


def _attention(q, k, vt, km, batch, seq):
    t, d = q.shape
    nh = d // HEAD_DIM
    nb = seq // MOBA_BLOCK
    hg = ATTN_HEADS_PER_STEP
    assert nb <= HEAD_DIM and nh % hg == 0
    w = hg * HEAD_DIM
    return pl.pallas_call(
        functools.partial(_attn_kernel, hg=hg),
        grid=(batch, nh // hg, nb),
        in_specs=[
            pl.BlockSpec((MOBA_BLOCK, w), lambda b, g, j: (b * nb + j, g)),
            pl.BlockSpec((seq, w), lambda b, g, j: (b, g), pipeline_mode=pl.Buffered(1)),
            pl.BlockSpec((nb, w, MOBA_BLOCK), lambda b, g, j: (b, g, 0), pipeline_mode=pl.Buffered(1)),
            pl.BlockSpec((nb, w), lambda b, g, j: (b, g)),
        ],
        out_specs=pl.BlockSpec((MOBA_BLOCK, w), lambda b, g, j: (b * nb + j, g)),
        out_shape=jax.ShapeDtypeStruct((t, d), bf16),
        scratch_shapes=[
            pltpu.VMEM((hg, 2 * HEAD_DIM, MOBA_BLOCK), bf16),
            pltpu.VMEM((hg, MOBA_BLOCK, MOBA_BLOCK), f32), pltpu.VMEM((hg, MOBA_BLOCK, MOBA_BLOCK), f32),
            pltpu.VMEM((hg, MOBA_BLOCK, MOBA_BLOCK), bf16), pltpu.VMEM((hg, MOBA_BLOCK, MOBA_BLOCK), bf16),
            pltpu.VMEM((hg, HEAD_DIM, MOBA_BLOCK), f32),
        ],
        compiler_params=_params("parallel", "parallel", "arbitrary"),
        name="attn",
    )(q, k, vt, km)


def _oproj_router_kernel(h_ref, a_ref, wo_ref, g_ref, r_ref, h_out, hn_out, route_out, *, n_exp):
    h4 = h_ref[...] + _dot(a_ref[...], wo_ref[...])
    h_out[...] = h4
    hn = _rms(h4, g_ref[...])
    hn_out[...] = hn
    logits = jnp.dot(hn, r_ref[...], precision=lax.Precision.HIGHEST, preferred_element_type=f32)
    lane = lax.broadcasted_iota(jnp.int32, logits.shape, 1).astype(f32)
    cand = jnp.where(lane < n_exp, logits, NEG_INF)
    m1 = jnp.max(cand, axis=1, keepdims=True)
    i1 = jnp.min(jnp.where(cand == m1, lane, float(LANES)), axis=1, keepdims=True)
    cand = jnp.where(lane == i1, REMOVED, cand)
    m2 = jnp.max(cand, axis=1, keepdims=True)
    i2 = jnp.min(jnp.where(cand == m2, lane, float(LANES)), axis=1, keepdims=True)
    e2 = jnp.exp(m2 - m1)
    den = 1.0 + e2
    route_out[...] = jnp.where(lane == 0, i1, jnp.where(lane == 1, i2, jnp.where(
        lane == 2, 1.0 / den, jnp.where(lane == 3, e2 / den, 0.0))))


def _oproj_router_layer(h, attn, w_o, norm, router):
    t, d = h.shape
    tm = ROW_TILE
    n_exp = router.shape[1]
    r_pad = jnp.zeros((d, LANES), f32).at[:, :n_exp].set(router)
    row = lambda i: (i, 0)
    return pl.pallas_call(
        functools.partial(_oproj_router_kernel, n_exp=n_exp),
        grid=(t // tm,),
        in_specs=[pl.BlockSpec((tm, d), row), pl.BlockSpec((tm, d), row), _const_spec((d, d)),
                  _const_spec((1, d)), _const_spec((d, LANES))],
        out_specs=[pl.BlockSpec((tm, d), row), pl.BlockSpec((tm, d), row), pl.BlockSpec((tm, LANES), row)],
        out_shape=[jax.ShapeDtypeStruct((t, d), f32), jax.ShapeDtypeStruct((t, d), f32),
                   jax.ShapeDtypeStruct((t, LANES), f32)],
        compiler_params=_params("parallel"),
        name="oproj_router",
    )(h, attn, w_o.astype(bf16), norm.reshape(1, d), r_pad)


def _expert_kernel(te_ref, tv_ref, src_ref, nsrc_ref, dst_ref, hn_hbm, w1_ref, w3_ref, w2_ref, out_hbm,
                   xg_ref, xb_ref, acc_ref, yo_ref, gsem, ssem, *, tm, nf):
    i = pl.program_id(0)
    f = pl.program_id(1)
    nt = pl.num_programs(0)
    valid = tv_ref[i] > 0
    has_next = tv_ref[jnp.minimum(i + 1, nt - 1)] * (i + 1 < nt).astype(jnp.int32) > 0
    slot = i % 2

    def gather_rows(idx_ref, buf):
        def issue(c, carry):
            for u in range(DMA_UNROLL):
                r = c * DMA_UNROLL + u
                pltpu.make_async_copy(hn_hbm.at[pl.ds(idx_ref[0, 0, r], 1), :],
                                      xg_ref.at[buf, pl.ds(r, 1), :], gsem.at[buf]).start()
            return carry
        lax.fori_loop(0, tm // DMA_UNROLL, issue, 0)

    def wait_gather(buf):
        pltpu.make_async_copy(hn_hbm.at[pl.ds(0, tm), :], xg_ref.at[buf], gsem.at[buf]).wait()

    def scatter_rows():
        def issue(c, carry):
            for u in range(DMA_UNROLL):
                r = c * DMA_UNROLL + u
                pltpu.make_async_copy(yo_ref.at[pl.ds(r, 1), :],
                                      out_hbm.at[pl.ds(dst_ref[0, 0, r], 1), :], ssem).start()
            return carry
        lax.fori_loop(0, tm // DMA_UNROLL, issue, 0)

    def wait_scatter():
        pltpu.make_async_copy(yo_ref, out_hbm.at[pl.ds(0, tm), :], ssem).wait()

    @pl.when(valid & (f == 0) & (i == 0))
    def _():
        gather_rows(src_ref, 0)
        yo_ref[...] = jnp.zeros_like(yo_ref)
        spare = pltpu.make_async_copy(yo_ref, out_hbm.at[pl.ds(out_hbm.shape[0] - tm, tm), :], ssem)
        spare.start()
        spare.wait()

    @pl.when(valid & (f == 0))
    def _():
        wait_gather(slot)
        xb_ref[...] = xg_ref[slot].astype(bf16)
        acc_ref[...] = jnp.zeros_like(acc_ref)

    @pl.when(valid & (f == 0) & has_next)
    def _():
        gather_rows(nsrc_ref, 1 - slot)

    @pl.when(valid)
    def _():
        x = xb_ref[...]
        a = _dot(x, w1_ref[...])
        b = _dot(x, w3_ref[...])
        acc_ref[...] += _dot((jax.nn.silu(a) * b).astype(bf16), w2_ref[...])

    @pl.when(valid & (f == nf - 1) & (i > 0))
    def _():
        wait_scatter()

    @pl.when(valid & (f == nf - 1))
    def _():
        yo_ref[...] = acc_ref[...]
        scatter_rows()

    @pl.when(valid & (f == nf - 1) & jnp.logical_not(has_next))
    def _():
        wait_scatter()


def _route_tables(e12, n_exp, tm):
    nslots = e12.size
    ef = e12.reshape(-1)
    order = jnp.argsort(ef, stable=True).astype(jnp.int32)
    counts = jnp.sum(ef[:, None] == jnp.arange(n_exp, dtype=jnp.int32)[None, :], axis=0, dtype=jnp.int32)
    tiles_e = (counts + tm - 1) // tm
    tile_end = jnp.cumsum(tiles_e)
    tile_start = tile_end - tiles_e
    group_start = jnp.cumsum(counts) - counts
    nt = nslots // tm + n_exp
    total = tile_end[-1]
    ti = jnp.arange(nt, dtype=jnp.int32)
    tv = (ti < total).astype(jnp.int32)
    tc = jnp.minimum(ti, total - 1)
    te = jnp.minimum(jnp.searchsorted(tile_end, tc, side="right").astype(jnp.int32), n_exp - 1)
    rank0 = (tc - tile_start[te]) * tm
    lane = jnp.arange(tm, dtype=jnp.int32)[None, :]
    rank = rank0[:, None] + lane
    real = (rank < counts[te][:, None]) & (tv[:, None] > 0)
    sidx = jnp.clip(group_start[te][:, None] + rank, 0, nslots - 1)
    slot = order[sidx]
    src = jnp.where(real, slot // TOP_K_EXPERTS, 0).reshape(nt, 1, tm)
    dst = jnp.where(real, slot, nslots + lane).reshape(nt, 1, tm)
    return te, tv, src, dst, nt


def _experts_layer(hn, e12, w1, w3, w2):
    t, d = hn.shape
    n_exp, _, ff = w1.shape
    tm = ROW_TILE
    fc = _ff_chunk(ff, 896)
    nf = ff // fc
    assert tm % DMA_UNROLL == 0 and tm % TOP_K_EXPERTS == 0
    te, tv, src, dst, nt = _route_tables(e12, n_exp, tm)
    w1r = w1.astype(bf16).reshape(n_exp, d, nf, fc).transpose(0, 2, 1, 3)
    w3r = w3.astype(bf16).reshape(n_exp, d, nf, fc).transpose(0, 2, 1, 3)
    w2r = w2.astype(bf16).reshape(n_exp, nf, fc, d)

    def wmap(i, f, te_r, tv_r):
        return (te_r[i], jnp.where(tv_r[i] > 0, f, nf - 1), 0, 0)

    idx_spec = pl.BlockSpec((1, 1, tm), lambda i, f, *_: (i, 0, 0), memory_space=pltpu.SMEM)
    next_spec = pl.BlockSpec((1, 1, tm), lambda i, f, *_: (jnp.minimum(i + 1, nt - 1), 0, 0),
                             memory_space=pltpu.SMEM)
    grid_spec = pltpu.PrefetchScalarGridSpec(
        num_scalar_prefetch=2,
        grid=(nt, nf),
        in_specs=[
            idx_spec, next_spec, idx_spec,
            pl.BlockSpec(memory_space=pl.ANY),
            pl.BlockSpec((None, None, d, fc), wmap),
            pl.BlockSpec((None, None, d, fc), wmap),
            pl.BlockSpec((None, None, fc, d), wmap),
        ],
        out_specs=pl.BlockSpec(memory_space=pl.ANY),
        scratch_shapes=[pltpu.VMEM((2, tm, d), f32), pltpu.VMEM((tm, d), bf16), pltpu.VMEM((tm, d), f32),
                        pltpu.VMEM((tm, d), f32),
                        pltpu.SemaphoreType.DMA((2,)), pltpu.SemaphoreType.DMA(())],
    )
    return pl.pallas_call(
        functools.partial(_expert_kernel, tm=tm, nf=nf),
        grid_spec=grid_spec,
        out_shape=jax.ShapeDtypeStruct((t * TOP_K_EXPERTS + tm, d), f32),
        compiler_params=_params("arbitrary", "arbitrary"),
        name="experts",
    )(te, tv, src, src, dst, hn, w1r, w3r, w2r)


def _final_kernel(h_ref, y_ref, route_ref, p_ref, pn_ref, wg_ref, wp_ref, fn_ref, o_ref):
    d = h_ref.shape[1]
    route = route_ref[...]
    h5 = h_ref[...] + route[:, 2:3] * y_ref[:, :d] + route[:, 3:4] * y_ref[:, d:]
    hn = _rms(h5, pn_ref[...]).astype(bf16)
    h6 = h5 + jax.nn.sigmoid(_dot(hn, wg_ref[...])) * _dot(p_ref[...].astype(bf16), wp_ref[...])
    o_ref[...] = _rms(h6, fn_ref[...])


def _final_layer(h, y2, route, p_i, ple_norm, ple_gate, ple_proj, final_norm):
    t, d = h.shape
    pd = p_i.shape[1]
    tm = ROW_TILE
    row = lambda i: (i, 0)
    return pl.pallas_call(
        _final_kernel,
        grid=(t // tm,),
        in_specs=[pl.BlockSpec((tm, d), row), pl.BlockSpec((tm, TOP_K_EXPERTS * d), row),
                  pl.BlockSpec((tm, LANES), row), pl.BlockSpec((tm, pd), row),
                  _const_spec((1, d)), _const_spec((d, d)), _const_spec((pd, d)), _const_spec((1, d))],
        out_specs=pl.BlockSpec((tm, d), row),
        out_shape=jax.ShapeDtypeStruct((t, d), f32),
        compiler_params=_params("parallel"),
        name="final",
    )(h, y2, route, p_i, ple_norm.reshape(1, d), ple_gate.astype(bf16), ple_proj.astype(bf16),
      final_norm.reshape(1, d))


def kernel(x, p, pool_norm, pool_w, pool_scale, kv_norm, w_k, w_v, attn_norm, w_q, w_o, ffn_norm, ffn_w1, ffn_w3, ffn_w2, router, exp_w1, exp_w3, exp_w2, ple_norm, ple_gate, ple_proj, final_norm):
    batch, seq, d = x.shape
    t = batch * seq
    assert seq % ROW_TILE == 0 and ROW_TILE % MOBA_BLOCK == 0 and d % HEAD_DIM == 0
    assert p.shape[0] == 2 and router.shape[2] <= LANES
    h = x.reshape(t, d)
    pf = p.reshape(p.shape[0], t, p.shape[-1])

    h = _pool_layer(h, pool_norm[0], pool_w[0], pool_scale[0], seq)
    h = _swiglu_layer(h, ffn_norm[0], ffn_w1[0], ffn_w3[0], ffn_w2[0])
    h, q, k, v, km = _ple_qkv_layer(h, pf[0], ple_norm[0], ple_gate[0], ple_proj[0],
                                    kv_norm, w_k, w_v, attn_norm[0], w_q[0], seq)
    attn = _attention(q, k, v, km, batch, seq)
    h, hn, route = _oproj_router_layer(h, attn, w_o[0], ffn_norm[1], router[0])
    e12 = route[:, :TOP_K_EXPERTS].astype(jnp.int32)
    y = _experts_layer(hn, e12, exp_w1[0], exp_w3[0], exp_w2[0])
    out = _final_layer(h, y.reshape(-1, TOP_K_EXPERTS * d), route, pf[1], ple_norm[1], ple_gate[1],
                       ple_proj[1], final_norm)
    return out.reshape(batch, seq, d)
```

```python
import functools

import jax
import jax.numpy as jnp
from jax import lax
from jax.experimental import pallas as pl
from jax.experimental.pallas import tpu as pltpu
from jax.experimental.pallas import tpu_sc as plsc

POOL_WINDOWS = (2, 4, 8, 16)
HEAD_DIM = 128
MOBA_BLOCK = 256
MOBA_TOPK = 3
ROPE_THETA = 500000.0
ROPE_DIM = HEAD_DIM // 4
TOP_K_EXPERTS = 2
RMS_EPS = 1e-6
NEG_INF = -1e30
REMOVED = -3e38

LANES = 128
ROW_TILE = 512
POOL_SUB = 128
ATTN_HEADS_PER_STEP = 4
LOG2E = 1.4426950408889634
SC_ROW = 128
SC_WINDOW = 128
VMEM_LIMIT = 56 * 1024 * 1024

bf16 = jnp.bfloat16
f32 = jnp.float32


def _dot(a, b):
    return jnp.dot(a, b, preferred_element_type=f32)


def _rms(x, g):
    var = jnp.mean(x * x, axis=-1, keepdims=True)
    return x * lax.rsqrt(var + RMS_EPS) * g


def _params(*sem):
    return pltpu.CompilerParams(dimension_semantics=sem, vmem_limit_bytes=VMEM_LIMIT)


def _const_spec(shape):
    nd = len(shape)
    return pl.BlockSpec(shape, lambda *_: (0,) * nd)


def _pool_kernel(x_ref, halo_ref, g_ref, pw_ref, ps_ref, o_ref, pooled_ref, *, ts, seq):
    i = pl.program_id(0)
    g = g_ref[...]
    x = x_ref[...]
    xn = _rms(x, g)
    keep = jnp.where((i * ts) % seq == 0, 0.0, 1.0)
    hnb = (_rms(halo_ref[...], g) * keep).astype(bf16)
    xnb = xn.astype(bf16)
    gd = x.shape[1] // len(POOL_WINDOWS)
    r = lax.broadcasted_iota(jnp.int32, (POOL_SUB, 2 * POOL_SUB), 0)
    c = lax.broadcasted_iota(jnp.int32, (POOL_SUB, 2 * POOL_SUB), 1)
    dist = r + POOL_SUB - c
    bands = [((dist >= 0) & (dist < w)).astype(bf16) for w in POOL_WINDOWS]
    rows = lax.broadcasted_iota(jnp.int32, (POOL_SUB, 1), 0)
    for sb in range(ts // POOL_SUB):
        lo, hi = sb * POOL_SUB, (sb + 1) * POOL_SUB
        prev = hnb if sb == 0 else xnb[lo - POOL_SUB:lo]
        ext = jnp.concatenate([prev, xnb[lo:hi]], axis=0)
        tpos = (i * ts + lo) % seq + rows
        for gi, w in enumerate(POOL_WINDOWS):
            cs = slice(gi * gd, (gi + 1) * gd)
            wsum = _dot(bands[gi], ext[:, cs])
            cnt = jnp.minimum(tpos + 1, w).astype(f32)
            pooled_ref[lo:hi, cs] = (wsum / cnt - xn[lo:hi, cs]).astype(bf16)
    for gi in range(len(POOL_WINDOWS)):
        cs = slice(gi * gd, (gi + 1) * gd)
        mixed = _dot(pooled_ref[:, cs], pw_ref[gi])
        o_ref[:, cs] = x[:, cs] + mixed * ps_ref[:, cs]


def _pool_layer(h, norm, pool_w, pool_scale, seq):
    t, d = h.shape
    ts = ROW_TILE
    ng = len(POOL_WINDOWS)
    gd = d // ng
    per = ts // POOL_SUB
    return pl.pallas_call(
        functools.partial(_pool_kernel, ts=ts, seq=seq),
        grid=(t // ts,),
        in_specs=[
            pl.BlockSpec((ts, d), lambda i: (i, 0)),
            pl.BlockSpec((POOL_SUB, d), lambda i: (jnp.maximum(i * per - 1, 0), 0)),
            _const_spec((1, d)),
            _const_spec((ng, gd, gd)),
            _const_spec((1, d)),
        ],
        out_specs=pl.BlockSpec((ts, d), lambda i: (i, 0)),
        out_shape=jax.ShapeDtypeStruct((t, d), f32),
        scratch_shapes=[pltpu.VMEM((ts, d), bf16)],
        compiler_params=_params("parallel"),
        name="pool",
    )(h, h, norm.reshape(1, d), pool_w.astype(bf16), pool_scale.reshape(1, d))


def _swiglu_kernel(h_ref, g_ref, w1_ref, w3_ref, w2_ref, o_ref, hn_ref, acc_ref, *, nf):
    f = pl.program_id(1)

    @pl.when(f == 0)
    def _():
        x = h_ref[...]
        hn_ref[...] = _rms(x, g_ref[...]).astype(bf16)
        acc_ref[...] = x

    hn = hn_ref[...]
    a = _dot(hn, w1_ref[...])
    b = _dot(hn, w3_ref[...])
    acc_ref[...] += _dot((jax.nn.silu(a) * b).astype(bf16), w2_ref[...])

    @pl.when(f == nf - 1)
    def _():
        o_ref[...] = acc_ref[...]


def _ff_chunk(ff, target):
    units = ff // LANES
    best = 1
    for k in range(1, units + 1):
        if units % k == 0 and k * LANES <= target:
            best = k
    return best * LANES


def _swiglu_layer(h, norm, w1, w3, w2):
    t, d = h.shape
    ff = w1.shape[1]
    tm = ROW_TILE
    fc = _ff_chunk(ff, 1408)
    nf = ff // fc
    w1r, w3r, w2r = w1.astype(bf16), w3.astype(bf16), w2.astype(bf16)
    return pl.pallas_call(
        functools.partial(_swiglu_kernel, nf=nf),
        grid=(t // tm, nf),
        in_specs=[
            pl.BlockSpec((tm, d), lambda i, f: (i, 0)),
            _const_spec((1, d)),
            pl.BlockSpec((d, fc), lambda i, f: (0, f)),
            pl.BlockSpec((d, fc), lambda i, f: (0, f)),
            pl.BlockSpec((fc, d), lambda i, f: (f, 0)),
        ],
        out_specs=pl.BlockSpec((tm, d), lambda i, f: (i, 0)),
        out_shape=jax.ShapeDtypeStruct((t, d), f32),
        scratch_shapes=[pltpu.VMEM((tm, d), bf16), pltpu.VMEM((tm, d), f32)],
        compiler_params=_params("parallel", "arbitrary"),
        name="swiglu",
    )(h, norm.reshape(1, d), w1r, w3r, w2r)


def _rope_tables(seq):
    half = ROPE_DIM // 2
    inv_freq = jnp.float32(ROPE_THETA) ** (-(jnp.arange(0, ROPE_DIM, 2, dtype=f32) / ROPE_DIM))
    ang = jnp.arange(seq, dtype=f32)[:, None] * inv_freq[None, :]
    cos, sin = jnp.cos(ang), jnp.sin(ang)
    ones = jnp.ones((seq, HEAD_DIM - ROPE_DIM), f32)
    zeros = jnp.zeros((seq, HEAD_DIM - half), f32)
    cos_t = jnp.concatenate([cos, cos, ones], axis=1)
    up_t = jnp.concatenate([-sin, zeros], axis=1)
    dn_t = jnp.concatenate([jnp.zeros((seq, half), f32), sin, jnp.zeros((seq, HEAD_DIM - ROPE_DIM), f32)], axis=1)
    return cos_t, up_t, dn_t


def _rope(xh, cos_t, up_t, dn_t):
    half = ROPE_DIM // 2
    return (xh * cos_t + pltpu.roll(xh, HEAD_DIM - half, 1) * up_t + pltpu.roll(xh, half, 1) * dn_t)


def _ple_qkv_kernel(h_ref, p_ref, pn_ref, wg_ref, wp_ref, kvn_ref, wk_ref, wv_ref, an_ref, wq_ref,
                    cos_ref, up_ref, dn_ref, h_out, q_out, k_out, v_out, km_out):
    h = h_ref[...]
    hn = _rms(h, pn_ref[...]).astype(bf16)
    h3 = h + jax.nn.sigmoid(_dot(hn, wg_ref[...])) * _dot(p_ref[...].astype(bf16), wp_ref[...])
    h_out[...] = h3
    base = h3 * lax.rsqrt(jnp.mean(h3 * h3, axis=-1, keepdims=True) + RMS_EPS)
    kn = (base * kvn_ref[...]).astype(bf16)
    qn = (base * an_ref[...]).astype(bf16)
    v = _dot(kn, wv_ref[...])
    for bi in range(h.shape[0] // MOBA_BLOCK):
        v_out[bi] = v[bi * MOBA_BLOCK:(bi + 1) * MOBA_BLOCK].T.astype(bf16)
    k = _dot(kn, wk_ref[...])
    q = _dot(qn, wq_ref[...])
    cos_t, up_t, dn_t = cos_ref[...], up_ref[...], dn_ref[...]
    tm, d = h.shape
    sub = km_out.shape[0] // (tm // MOBA_BLOCK)
    for hh in range(d // HEAD_DIM):
        cs = slice(hh * HEAD_DIM, (hh + 1) * HEAD_DIM)
        q_out[:, cs] = _rope(q[:, cs], cos_t, up_t, dn_t)
        kr = _rope(k[:, cs], cos_t, up_t, dn_t)
        k_out[:, cs] = kr.astype(bf16)
        for bi in range(tm // MOBA_BLOCK):
            m = jnp.mean(kr[bi * MOBA_BLOCK:(bi + 1) * MOBA_BLOCK], axis=0, keepdims=True)
            km_out[bi * sub:(bi + 1) * sub, cs] = jnp.broadcast_to(m, (sub, HEAD_DIM))


def _ple_qkv_layer(h, p_i, ple_norm, ple_gate, ple_proj, kv_norm, w_k, w_v, attn_norm, w_q, seq):
    t, d = h.shape
    pd = p_i.shape[1]
    tm = ROW_TILE
    sub = 8
    cos_t, up_t, dn_t = _rope_tables(seq)
    tiles_per_seq = seq // tm
    row = lambda i: (i, 0)
    tab = pl.BlockSpec((tm, HEAD_DIM), lambda i: (i % tiles_per_seq, 0))
    nkm = t // MOBA_BLOCK * sub
    outs = pl.pallas_call(
        _ple_qkv_kernel,
        grid=(t // tm,),
        in_specs=[
            pl.BlockSpec((tm, d), row), pl.BlockSpec((tm, pd), row),
            _const_spec((1, d)), _const_spec((d, d)), _const_spec((pd, d)),
            _const_spec((1, d)), _const_spec((d, d)), _const_spec((d, d)),
            _const_spec((1, d)), _const_spec((d, d)),
            tab, tab, tab,
        ],
        out_specs=[
            pl.BlockSpec((tm, d), row), pl.BlockSpec((tm, d), row),
            pl.BlockSpec((tm, d), row),
            pl.BlockSpec((tm // MOBA_BLOCK, d, MOBA_BLOCK), lambda i: (i, 0, 0)),
            pl.BlockSpec((tm // MOBA_BLOCK * sub, d), row),
        ],
        out_shape=[
            jax.ShapeDtypeStruct((t, d), f32), jax.ShapeDtypeStruct((t, d), f32),
            jax.ShapeDtypeStruct((t, d), bf16),
            jax.ShapeDtypeStruct((t // MOBA_BLOCK, d, MOBA_BLOCK), bf16),
            jax.ShapeDtypeStruct((nkm, d), f32),
        ],
        compiler_params=_params("parallel"),
        name="ple_qkv",
    )(h, p_i, ple_norm.reshape(1, d), ple_gate.astype(bf16), ple_proj.astype(bf16),
      kv_norm.reshape(1, d), w_k.astype(bf16), w_v.astype(bf16),
      attn_norm.reshape(1, d), w_q.astype(bf16), cos_t, up_t, dn_t)
    h3, q, k, v, km = outs
    return h3, q, k, v, km.reshape(t // MOBA_BLOCK, sub, d)[:, 0, :]


def _attn_kernel(q_ref, k_ref, vt_ref, km_ref, o_ref, qa_ref, sa_ref, sb_ref, pa_ref, pb_ref, acc_ref, *, hg):
    j = pl.program_id(2)
    nb = km_ref.shape[0]
    bs = q_ref.shape[0]
    hd = HEAD_DIM
    qscale = (hd ** -0.5) * LOG2E
    blk = lax.broadcasted_iota(jnp.int32, (nb, bs), 0).astype(f32)
    jf = j.astype(f32)
    krow = lax.broadcasted_iota(jnp.int32, (bs, bs), 0)
    qcol = lax.broadcasted_iota(jnp.int32, (bs, bs), 1)
    start_j = pl.multiple_of(j * bs, bs)

    carry0 = []
    for h in range(hg):
        cs = slice(h * hd, (h + 1) * hd)
        qf = q_ref[:, cs]
        gate = lax.dot_general(km_ref[:, cs], qf, (((1,), (1,)), ((), ())),
                               precision=lax.Precision.HIGHEST, preferred_element_type=f32)
        cand = jnp.where(blk < jf, gate, NEG_INF)
        sel = blk < 0.0
        for _ in range(min(MOBA_TOPK, nb)):
            mx = jnp.max(cand, axis=0, keepdims=True)
            pick = jnp.min(jnp.where(cand == mx, blk, float(nb)), axis=0, keepdims=True)
            hit = blk == pick
            sel = sel | hit
            cand = jnp.where(hit, REMOVED, cand)
        bias = jnp.where(sel & (blk < jf), 0.0, NEG_INF)
        if nb < hd:
            bias = jnp.concatenate([bias, jnp.zeros((hd - nb, bs), f32)], axis=0)
        qbt = (qf * qscale).T.astype(bf16)
        qa_ref[h] = jnp.concatenate([qbt, bias.astype(bf16)], axis=0)

        s = _dot(k_ref[pl.ds(start_j, bs), cs], qbt)
        s = jnp.where(krow <= qcol, s, NEG_INF)
        m0 = jnp.max(s, axis=0, keepdims=True)
        p = jnp.exp2(s - m0)
        l0 = jnp.sum(p, axis=0, keepdims=True)
        acc0 = _dot(vt_ref[j, cs, :], p.astype(bf16))
        carry0 += [m0, l0, acc0]

    lane = lax.broadcasted_iota(jnp.int32, (bs, hd), 1)

    def block_scores(n, h):
        start = pl.multiple_of(n * bs, bs)
        onehot = (lane == n).astype(bf16)
        ka = jnp.concatenate([k_ref[pl.ds(start, bs), h * hd:(h + 1) * hd], onehot], axis=1)
        return _dot(ka, qa_ref[h])

    def weighted_values(n, h, pb):
        return _dot(vt_ref[n, h * hd:(h + 1) * hd, :], pb)

    def stage(n, s_in, s_out, p_prev, p_out, stats):
        nn = jnp.minimum(n + 1, nb - 1)
        for h in range(hg):
            s_out[h] = block_scores(nn, h)
        prev = jnp.maximum(n - 1, 0)
        new = []
        for h in range(hg):
            m, l, alpha_p = stats[3 * h:3 * h + 3]
            pv = weighted_values(prev, h, p_prev[h])
            s = s_in[h]
            m_new = jnp.maximum(m, jnp.max(s, axis=0, keepdims=True))
            alpha = jnp.exp2(m - m_new)
            p = jnp.exp2(s - m_new)
            l = alpha * l + jnp.sum(p, axis=0, keepdims=True)
            p_out[h] = p.astype(bf16)
            acc_ref[h] = alpha_p * acc_ref[h] + pv
            new += [m_new, l, alpha]
        return new

    def body(i, stats):
        stats = stage(2 * i, sa_ref, sb_ref, pb_ref, pa_ref, stats)
        stats = stage(2 * i + 1, sb_ref, sa_ref, pa_ref, pb_ref, stats)
        return tuple(stats)

    init = []
    for h in range(hg):
        m0, l0, acc0 = carry0[3 * h:3 * h + 3]
        acc_ref[h] = acc0
        sa_ref[h] = block_scores(0, h)
        pb_ref[h] = jnp.zeros((bs, bs), bf16)
        init += [m0, l0, jnp.ones_like(m0)]
    trips = (j + 1) // 2
    res = lax.fori_loop(0, trips, body, tuple(init))
    last = jnp.maximum(2 * trips - 1, 0)
    for h in range(hg):
        _, l, alpha_p = res[3 * h:3 * h + 3]
        acc = alpha_p * acc_ref[h] + weighted_values(last, h, pb_ref[h])
        o_ref[:, h * hd:(h + 1) * hd] = (acc / l).T.astype(o_ref.dtype)


def _attention(q, k, vt, km, batch, seq):
    t, d = q.shape
    nh = d // HEAD_DIM
    nb = seq // MOBA_BLOCK
    hg = ATTN_HEADS_PER_STEP
    assert nb <= HEAD_DIM and nh % hg == 0
    w = hg * HEAD_DIM
    return pl.pallas_call(
        functools.partial(_attn_kernel, hg=hg),
        grid=(batch, nh // hg, nb),
        in_specs=[
            pl.BlockSpec((MOBA_BLOCK, w), lambda b, g, j: (b * nb + j, g)),
            pl.BlockSpec((seq, w), lambda b, g, j: (b, g), pipeline_mode=pl.Buffered(1)),
            pl.BlockSpec((nb, w, MOBA_BLOCK), lambda b, g, j: (b, g, 0), pipeline_mode=pl.Buffered(1)),
            pl.BlockSpec((nb, w), lambda b, g, j: (b, g)),
        ],
        out_specs=pl.BlockSpec((MOBA_BLOCK, w), lambda b, g, j: (b * nb + j, g)),
        out_shape=jax.ShapeDtypeStruct((t, d), bf16),
        scratch_shapes=[
            pltpu.VMEM((hg, 2 * HEAD_DIM, MOBA_BLOCK), bf16),
            pltpu.VMEM((hg, MOBA_BLOCK, MOBA_BLOCK), f32), pltpu.VMEM((hg, MOBA_BLOCK, MOBA_BLOCK), f32),
            pltpu.VMEM((hg, MOBA_BLOCK, MOBA_BLOCK), bf16), pltpu.VMEM((hg, MOBA_BLOCK, MOBA_BLOCK), bf16),
            pltpu.VMEM((hg, HEAD_DIM, MOBA_BLOCK), f32),
        ],
        compiler_params=_params("parallel", "parallel", "arbitrary"),
        name="attn",
    )(q, k, vt, km)


def _oproj_router_kernel(h_ref, a_ref, wo_ref, g_ref, r_ref, h_out, hn_out, route_out, *, n_exp):
    h4 = h_ref[...] + _dot(a_ref[...], wo_ref[...])
    h_out[...] = h4
    hn = _rms(h4, g_ref[...])
    hn_out[...] = hn
    logits = jnp.dot(hn, r_ref[...], precision=lax.Precision.HIGHEST, preferred_element_type=f32)
    lane = lax.broadcasted_iota(jnp.int32, logits.shape, 1).astype(f32)
    cand = jnp.where(lane < n_exp, logits, NEG_INF)
    m1 = jnp.max(cand, axis=1, keepdims=True)
    i1 = jnp.min(jnp.where(cand == m1, lane, float(LANES)), axis=1, keepdims=True)
    cand = jnp.where(lane == i1, REMOVED, cand)
    m2 = jnp.max(cand, axis=1, keepdims=True)
    i2 = jnp.min(jnp.where(cand == m2, lane, float(LANES)), axis=1, keepdims=True)
    e2 = jnp.exp(m2 - m1)
    den = 1.0 + e2
    route_out[...] = jnp.where(lane == 0, i1, jnp.where(lane == 1, i2, jnp.where(
        lane == 2, 1.0 / den, jnp.where(lane == 3, e2 / den, 0.0))))


def _oproj_router_layer(h, attn, w_o, norm, router):
    t, d = h.shape
    tm = ROW_TILE
    n_exp = router.shape[1]
    r_pad = jnp.zeros((d, LANES), f32).at[:, :n_exp].set(router)
    row = lambda i: (i, 0)
    return pl.pallas_call(
        functools.partial(_oproj_router_kernel, n_exp=n_exp),
        grid=(t // tm,),
        in_specs=[pl.BlockSpec((tm, d), row), pl.BlockSpec((tm, d), row), _const_spec((d, d)),
                  _const_spec((1, d)), _const_spec((d, LANES))],
        out_specs=[pl.BlockSpec((tm, d), row), pl.BlockSpec((tm, d), row), pl.BlockSpec((tm, LANES), row)],
        out_shape=[jax.ShapeDtypeStruct((t, d), f32), jax.ShapeDtypeStruct((t, d), f32),
                   jax.ShapeDtypeStruct((t, LANES), f32)],
        compiler_params=_params("parallel"),
        name="oproj_router",
    )(h, attn, w_o.astype(bf16), norm.reshape(1, d), r_pad)


def _sc_gather(x, idx):
    n = idx.shape[0]
    assert x.shape[1] == SC_ROW and n % SC_WINDOW == 0
    mesh = plsc.VectorSubcoreMesh(core_axis_name="core", subcore_axis_name="subcore")

    @pl.kernel(out_type=jax.ShapeDtypeStruct((n, SC_ROW), x.dtype), mesh=mesh, scratch_types=[])
    def gather(x_hbm, i_hbm, o_hbm):
        def body(i_vmem, o_vmem):
            pltpu.sync_copy(x_hbm.at[i_vmem.at[0]], o_vmem)

        pltpu.emit_pipeline(
            body,
            grid=(n // SC_WINDOW,),
            in_specs=[pl.BlockSpec((1, SC_WINDOW), index_map=lambda i: (0, i))],
            out_specs=[pl.BlockSpec((SC_WINDOW, SC_ROW), index_map=lambda i: (i, 0))],
            core_axis_name=("core", "subcore"),
            dimension_semantics=(pltpu.PARALLEL,),
        )(i_hbm, o_hbm)

    return gather(x, idx.reshape(1, n))


def _gather_tokens(x, rows):
    n, d = x.shape
    per = d // SC_ROW
    idx = (rows[:, None] * per + jnp.arange(per, dtype=jnp.int32)[None, :]).reshape(-1)
    return _sc_gather(x.reshape(n * per, SC_ROW), idx).reshape(rows.shape[0], d)


def _expert_kernel(te_ref, tv_ref, x_ref, w1_ref, w3_ref, w2_ref, o_ref, xb_ref, *, nf):
    i = pl.program_id(0)
    f = pl.program_id(1)
    valid = tv_ref[i] > 0

    @pl.when(f == 0)
    def _():
        xb_ref[...] = x_ref[...].astype(bf16)
        o_ref[...] = jnp.zeros_like(o_ref)

    @pl.when(valid)
    def _():
        x = xb_ref[...]
        a = _dot(x, w1_ref[...])
        b = _dot(x, w3_ref[...])
        o_ref[...] += _dot((jax.nn.silu(a) * b).astype(bf16), w2_ref[...])


def _route_tables(e12, n_exp, tm):
    nslots = e12.size
    ef = e12.reshape(-1)
    onehot = (ef[:, None] == jnp.arange(n_exp, dtype=jnp.int32)[None, :]).astype(jnp.int32)
    counts = jnp.sum(onehot, axis=0)
    tiles_e = (counts + tm - 1) // tm
    tile_end = jnp.cumsum(tiles_e)
    tile_start = tile_end - tiles_e
    group_start = jnp.cumsum(counts) - counts
    nt = nslots // tm + n_exp
    total = tile_end[-1]
    ti = jnp.arange(nt, dtype=jnp.int32)
    tv = (ti < total).astype(jnp.int32)
    tc = jnp.minimum(ti, total - 1)
    te = jnp.minimum(jnp.searchsorted(tile_end, tc, side="right").astype(jnp.int32), n_exp - 1)
    order = jnp.argsort(ef, stable=True).astype(jnp.int32)
    rank = ((tc - tile_start[te]) * tm)[:, None] + jnp.arange(tm, dtype=jnp.int32)[None, :]
    real = (rank < counts[te][:, None]) & (tv[:, None] > 0)
    sidx = jnp.clip(group_start[te][:, None] + rank, 0, nslots - 1)
    src = jnp.where(real, order[sidx] // TOP_K_EXPERTS, 0).reshape(-1)
    before = jnp.sum((jnp.cumsum(onehot, axis=0) - onehot) * onehot, axis=1)
    pos = tile_start[ef] * tm + before
    return te, tv, src, pos, nt


def _experts_layer(hn, e12, w1, w3, w2):
    t, d = hn.shape
    n_exp, _, ff = w1.shape
    tm = ROW_TILE
    fc = _ff_chunk(ff, 896)
    nf = ff // fc
    te, tv, src, pos, nt = _route_tables(e12, n_exp, tm)
    xs = _gather_tokens(hn, src)
    w1r, w3r, w2r = w1.astype(bf16), w3.astype(bf16), w2.astype(bf16)

    def fsel(i, f, tv_r):
        return jnp.where(tv_r[i] > 0, f, nf - 1)

    def wmap_in(i, f, te_r, tv_r):
        return (te_r[i], 0, fsel(i, f, tv_r))

    def wmap_out(i, f, te_r, tv_r):
        return (te_r[i], fsel(i, f, tv_r), 0)

    grid_spec = pltpu.PrefetchScalarGridSpec(
        num_scalar_prefetch=2,
        grid=(nt, nf),
        in_specs=[
            pl.BlockSpec((tm, d), lambda i, f, *_: (i, 0)),
            pl.BlockSpec((None, d, fc), wmap_in),
            pl.BlockSpec((None, d, fc), wmap_in),
            pl.BlockSpec((None, fc, d), wmap_out),
        ],
        out_specs=pl.BlockSpec((tm, d), lambda i, f, *_: (i, 0)),
        scratch_shapes=[pltpu.VMEM((tm, d), bf16)],
    )
    ys = pl.pallas_call(
        functools.partial(_expert_kernel, nf=nf),
        grid_spec=grid_spec,
        out_shape=jax.ShapeDtypeStruct((nt * tm, d), f32),
        compiler_params=_params("parallel", "arbitrary"),
        name="experts",
    )(te, tv, xs, w1r, w3r, w2r)
    return _gather_tokens(ys, pos)


def _final_kernel(h_ref, y_ref, route_ref, p_ref, pn_ref, wg_ref, wp_ref, fn_ref, o_ref):
    d = h_ref.shape[1]
    route = route_ref[...]
    h5 = h_ref[...] + route[:, 2:3] * y_ref[:, :d] + route[:, 3:4] * y_ref[:, d:]
    hn = _rms(h5, pn_ref[...]).astype(bf16)
    h6 = h5 + jax.nn.sigmoid(_dot(hn, wg_ref[...])) * _dot(p_ref[...].astype(bf16), wp_ref[...])
    o_ref[...] = _rms(h6, fn_ref[...])


def _final_layer(h, y2, route, p_i, ple_norm, ple_gate, ple_proj, final_norm):
    t, d = h.shape
    pd = p_i.shape[1]
    tm = ROW_TILE
    row = lambda i: (i, 0)
    return pl.pallas_call(
        _final_kernel,
        grid=(t // tm,),
        in_specs=[pl.BlockSpec((tm, d), row), pl.BlockSpec((tm, TOP_K_EXPERTS * d), row),
                  pl.BlockSpec((tm, LANES), row), pl.BlockSpec((tm, pd), row),
                  _const_spec((1, d)), _const_spec((d, d)), _const_spec((pd, d)), _const_spec((1, d))],
        out_specs=pl.BlockSpec((tm, d), row),
        out_shape=jax.ShapeDtypeStruct((t, d), f32),
        compiler_params=_params("parallel"),
        name="final",
    )(h, y2, route, p_i, ple_norm.reshape(1, d), ple_gate.astype(bf16), ple_proj.astype(bf16),
      final_norm.reshape(1, d))


def kernel(x, p, pool_norm, pool_w, pool_scale, kv_norm, w_k, w_v, attn_norm, w_q, w_o, ffn_norm, ffn_w1, ffn_w3, ffn_w2, router, exp_w1, exp_w3, exp_w2, ple_norm, ple_gate, ple_proj, final_norm):
    batch, seq, d = x.shape
    t = batch * seq
    assert seq % ROW_TILE == 0 and ROW_TILE % MOBA_BLOCK == 0 and d % HEAD_DIM == 0
    assert p.shape[0] == 2 and router.shape[2] <= LANES
    h = x.reshape(t, d)
    pf = p.reshape(p.shape[0], t, p.shape[-1])

    h = _pool_layer(h, pool_norm[0], pool_w[0], pool_scale[0], seq)
    h = _swiglu_layer(h, ffn_norm[0], ffn_w1[0], ffn_w3[0], ffn_w2[0])
    h, q, k, v, km = _ple_qkv_layer(h, pf[0], ple_norm[0], ple_gate[0], ple_proj[0],
                                    kv_norm, w_k, w_v, attn_norm[0], w_q[0], seq)
    attn = _attention(q, k, v, km, batch, seq)
    h, hn, route = _oproj_router_layer(h, attn, w_o[0], ffn_norm[1], router[0])
    e12 = route[:, :TOP_K_EXPERTS].astype(jnp.int32)
    y = _experts_layer(hn, e12, exp_w1[0], exp_w3[0], exp_w2[0])
    out = _final_layer(h, y.reshape(-1, TOP_K_EXPERTS * d), route, pf[1], ple_norm[1], ple_gate[1],
                       ple_proj[1], final_norm)
    return out.reshape(batch, seq, d)
```

```python
import functools

import jax
import jax.numpy as jnp
from jax import lax
from jax.experimental import pallas as pl
from jax.experimental.pallas import tpu as pltpu
from jax.experimental.pallas import tpu_sc as plsc

POOL_WINDOWS = (2, 4, 8, 16)
HEAD_DIM = 128
MOBA_BLOCK = 256
MOBA_TOPK = 3
ROPE_THETA = 500000.0
ROPE_DIM = HEAD_DIM // 4
TOP_K_EXPERTS = 2
RMS_EPS = 1e-6
NEG_INF = -1e30
REMOVED = -3e38

LANES = 128
SUBLANES = 8
ROW_TILE = 512
POOL_SUB = 128
ATTN_HEADS_PER_STEP = 4
ATTN_BLOCKS_PER_TRIP = 4
SUM_ROWS = 16
LOG2E = 1.4426950408889634
SC_ROW = 128
SC_WINDOW = 128
VMEM_LIMIT = 56 * 1024 * 1024

bf16 = jnp.bfloat16
f32 = jnp.float32


def _dot(a, b):
    return jnp.dot(a, b, preferred_element_type=f32)


def _rms(x, g):
    var = jnp.mean(x * x, axis=-1, keepdims=True)
    return x * lax.rsqrt(var + RMS_EPS) * g


def _params(*sem):
    return pltpu.CompilerParams(dimension_semantics=sem, vmem_limit_bytes=VMEM_LIMIT)


def _const_spec(shape):
    nd = len(shape)
    return pl.BlockSpec(shape, lambda *_: (0,) * nd)


def _pool_kernel(x_ref, halo_ref, g_ref, pw_ref, ps_ref, o_ref, pooled_ref, *, ts, seq):
    i = pl.program_id(0)
    g = g_ref[...]
    x = x_ref[...]
    xn = _rms(x, g)
    keep = jnp.where((i * ts) % seq == 0, 0.0, 1.0)
    hnb = (_rms(halo_ref[...], g) * keep).astype(bf16)
    xnb = xn.astype(bf16)
    gd = x.shape[1] // len(POOL_WINDOWS)
    r = lax.broadcasted_iota(jnp.int32, (POOL_SUB, 2 * POOL_SUB), 0)
    c = lax.broadcasted_iota(jnp.int32, (POOL_SUB, 2 * POOL_SUB), 1)
    dist = r + POOL_SUB - c
    bands = [((dist >= 0) & (dist < w)).astype(bf16) for w in POOL_WINDOWS]
    rows = lax.broadcasted_iota(jnp.int32, (POOL_SUB, 1), 0)
    for sb in range(ts // POOL_SUB):
        lo, hi = sb * POOL_SUB, (sb + 1) * POOL_SUB
        prev = hnb if sb == 0 else xnb[lo - POOL_SUB:lo]
        ext = jnp.concatenate([prev, xnb[lo:hi]], axis=0)
        tpos = (i * ts + lo) % seq + rows
        for gi, w in enumerate(POOL_WINDOWS):
            cs = slice(gi * gd, (gi + 1) * gd)
            wsum = _dot(bands[gi], ext[:, cs])
            cnt = jnp.minimum(tpos + 1, w).astype(f32)
            pooled_ref[lo:hi, cs] = (wsum / cnt - xn[lo:hi, cs]).astype(bf16)
    for gi in range(len(POOL_WINDOWS)):
        cs = slice(gi * gd, (gi + 1) * gd)
        mixed = _dot(pooled_ref[:, cs], pw_ref[gi])
        o_ref[:, cs] = x[:, cs] + mixed * ps_ref[:, cs]


def _pool_layer(h, norm, pool_w, pool_scale, seq):
    t, d = h.shape
    ts = ROW_TILE
    ng = len(POOL_WINDOWS)
    gd = d // ng
    per = ts // POOL_SUB
    return pl.pallas_call(
        functools.partial(_pool_kernel, ts=ts, seq=seq),
        grid=(t // ts,),
        in_specs=[
            pl.BlockSpec((ts, d), lambda i: (i, 0)),
            pl.BlockSpec((POOL_SUB, d), lambda i: (jnp.maximum(i * per - 1, 0), 0)),
            _const_spec((1, d)),
            _const_spec((ng, gd, gd)),
            _const_spec((1, d)),
        ],
        out_specs=pl.BlockSpec((ts, d), lambda i: (i, 0)),
        out_shape=jax.ShapeDtypeStruct((t, d), f32),
        scratch_shapes=[pltpu.VMEM((ts, d), bf16)],
        compiler_params=_params("parallel"),
        name="pool",
    )(h, h, norm.reshape(1, d), pool_w.astype(bf16), pool_scale.reshape(1, d))


def _swiglu_kernel(h_ref, g_ref, w1_ref, w3_ref, w2_ref, o_ref, hn_ref, acc_ref, *, nf):
    f = pl.program_id(1)

    @pl.when(f == 0)
    def _():
        x = h_ref[...]
        hn_ref[...] = _rms(x, g_ref[...]).astype(bf16)
        acc_ref[...] = x

    hn = hn_ref[...]
    a = _dot(hn, w1_ref[...])
    b = _dot(hn, w3_ref[...])
    acc_ref[...] += _dot((jax.nn.silu(a) * b).astype(bf16), w2_ref[...])

    @pl.when(f == nf - 1)
    def _():
        o_ref[...] = acc_ref[...]


def _ff_chunk(ff, target):
    units = ff // LANES
    best = 1
    for k in range(1, units + 1):
        if units % k == 0 and k * LANES <= target:
            best = k
    return best * LANES


def _swiglu_layer(h, norm, w1, w3, w2):
    t, d = h.shape
    ff = w1.shape[1]
    tm = ROW_TILE
    fc = _ff_chunk(ff, 1408)
    nf = ff // fc
    w1r, w3r, w2r = w1.astype(bf16), w3.astype(bf16), w2.astype(bf16)
    return pl.pallas_call(
        functools.partial(_swiglu_kernel, nf=nf),
        grid=(t // tm, nf),
        in_specs=[
            pl.BlockSpec((tm, d), lambda i, f: (i, 0)),
            _const_spec((1, d)),
            pl.BlockSpec((d, fc), lambda i, f: (0, f)),
            pl.BlockSpec((d, fc), lambda i, f: (0, f)),
            pl.BlockSpec((fc, d), lambda i, f: (f, 0)),
        ],
        out_specs=pl.BlockSpec((tm, d), lambda i, f: (i, 0)),
        out_shape=jax.ShapeDtypeStruct((t, d), f32),
        scratch_shapes=[pltpu.VMEM((tm, d), bf16), pltpu.VMEM((tm, d), f32)],
        compiler_params=_params("parallel", "arbitrary"),
        name="swiglu",
    )(h, norm.reshape(1, d), w1r, w3r, w2r)


def _rope_tables(seq):
    half = ROPE_DIM // 2
    inv_freq = jnp.float32(ROPE_THETA) ** (-(jnp.arange(0, ROPE_DIM, 2, dtype=f32) / ROPE_DIM))
    ang = jnp.arange(seq, dtype=f32)[:, None] * inv_freq[None, :]
    cos, sin = jnp.cos(ang), jnp.sin(ang)
    ones = jnp.ones((seq, HEAD_DIM - ROPE_DIM), f32)
    zeros = jnp.zeros((seq, HEAD_DIM - half), f32)
    cos_t = jnp.concatenate([cos, cos, ones], axis=1)
    up_t = jnp.concatenate([-sin, zeros], axis=1)
    dn_t = jnp.concatenate([jnp.zeros((seq, half), f32), sin, jnp.zeros((seq, HEAD_DIM - ROPE_DIM), f32)], axis=1)
    return cos_t, up_t, dn_t


def _rope(xh, cos_t, up_t, dn_t):
    half = ROPE_DIM // 2
    return (xh * cos_t + pltpu.roll(xh, HEAD_DIM - half, 1) * up_t + pltpu.roll(xh, half, 1) * dn_t)


def _ple_qkv_kernel(h_ref, p_ref, pn_ref, wg_ref, wp_ref, kvn_ref, wk_ref, wv_ref, an_ref, wq_ref,
                    cos_ref, up_ref, dn_ref, h_out, q_out, k_out, v_out, km_out):
    h = h_ref[...]
    hn = _rms(h, pn_ref[...]).astype(bf16)
    h3 = h + jax.nn.sigmoid(_dot(hn, wg_ref[...])) * _dot(p_ref[...].astype(bf16), wp_ref[...])
    h_out[...] = h3
    base = h3 * lax.rsqrt(jnp.mean(h3 * h3, axis=-1, keepdims=True) + RMS_EPS)
    kn = (base * kvn_ref[...]).astype(bf16)
    qn = (base * an_ref[...]).astype(bf16)
    v = _dot(kn, wv_ref[...])
    for bi in range(h.shape[0] // MOBA_BLOCK):
        v_out[bi] = v[bi * MOBA_BLOCK:(bi + 1) * MOBA_BLOCK].T.astype(bf16)
    k = _dot(kn, wk_ref[...])
    q = _dot(qn, wq_ref[...])
    cos_t, up_t, dn_t = cos_ref[...], up_ref[...], dn_ref[...]
    tm, d = h.shape
    sub = km_out.shape[0] // (tm // MOBA_BLOCK)
    for hh in range(d // HEAD_DIM):
        cs = slice(hh * HEAD_DIM, (hh + 1) * HEAD_DIM)
        q_out[:, cs] = _rope(q[:, cs], cos_t, up_t, dn_t)
        kr = _rope(k[:, cs], cos_t, up_t, dn_t)
        k_out[:, cs] = kr.astype(bf16)
        for bi in range(tm // MOBA_BLOCK):
            m = jnp.mean(kr[bi * MOBA_BLOCK:(bi + 1) * MOBA_BLOCK], axis=0, keepdims=True)
            km_out[bi * sub:(bi + 1) * sub, cs] = jnp.broadcast_to(m, (sub, HEAD_DIM))


def _ple_qkv_layer(h, p_i, ple_norm, ple_gate, ple_proj, kv_norm, w_k, w_v, attn_norm, w_q, seq):
    t, d = h.shape
    pd = p_i.shape[1]
    tm = ROW_TILE
    sub = 8
    cos_t, up_t, dn_t = _rope_tables(seq)
    tiles_per_seq = seq // tm
    row = lambda i: (i, 0)
    tab = pl.BlockSpec((tm, HEAD_DIM), lambda i: (i % tiles_per_seq, 0))
    nkm = t // MOBA_BLOCK * sub
    outs = pl.pallas_call(
        _ple_qkv_kernel,
        grid=(t // tm,),
        in_specs=[
            pl.BlockSpec((tm, d), row), pl.BlockSpec((tm, pd), row),
            _const_spec((1, d)), _const_spec((d, d)), _const_spec((pd, d)),
            _const_spec((1, d)), _const_spec((d, d)), _const_spec((d, d)),
            _const_spec((1, d)), _const_spec((d, d)),
            tab, tab, tab,
        ],
        out_specs=[
            pl.BlockSpec((tm, d), row), pl.BlockSpec((tm, d), row),
            pl.BlockSpec((tm, d), row),
            pl.BlockSpec((tm // MOBA_BLOCK, d, MOBA_BLOCK), lambda i: (i, 0, 0)),
            pl.BlockSpec((tm // MOBA_BLOCK * sub, d), row),
        ],
        out_shape=[
            jax.ShapeDtypeStruct((t, d), f32), jax.ShapeDtypeStruct((t, d), f32),
            jax.ShapeDtypeStruct((t, d), bf16),
            jax.ShapeDtypeStruct((t // MOBA_BLOCK, d, MOBA_BLOCK), bf16),
            jax.ShapeDtypeStruct((nkm, d), f32),
        ],
        compiler_params=_params("parallel"),
        name="ple_qkv",
    )(h, p_i, ple_norm.reshape(1, d), ple_gate.astype(bf16), ple_proj.astype(bf16),
      kv_norm.reshape(1, d), w_k.astype(bf16), w_v.astype(bf16),
      attn_norm.reshape(1, d), w_q.astype(bf16), cos_t, up_t, dn_t)
    h3, q, k, v, km = outs
    return h3, q, k, v, km.reshape(t // MOBA_BLOCK, sub, d)[:, 0, :]


def _attn_kernel(q_ref, k_ref, vt_ref, km_ref, o_ref, qa_ref, sa_ref, sb_ref, pa_ref, pb_ref, acc_ref, *, hg):
    j = pl.program_id(2)
    nb = km_ref.shape[0]
    bs = q_ref.shape[0]
    hd = HEAD_DIM
    qscale = (hd ** -0.5) * LOG2E
    blk = lax.broadcasted_iota(jnp.int32, (nb, bs), 0).astype(f32)
    jf = j.astype(f32)
    krow = lax.broadcasted_iota(jnp.int32, (bs, bs), 0)
    qcol = lax.broadcasted_iota(jnp.int32, (bs, bs), 1)
    start_j = pl.multiple_of(j * bs, bs)

    carry0 = []
    for h in range(hg):
        cs = slice(h * hd, (h + 1) * hd)
        qf = q_ref[:, cs]
        gate = lax.dot_general(km_ref[:, cs], qf, (((1,), (1,)), ((), ())),
                               precision=lax.Precision.HIGHEST, preferred_element_type=f32)
        cand = jnp.where(blk < jf, gate, NEG_INF)
        sel = blk < 0.0
        for _ in range(min(MOBA_TOPK, nb)):
            mx = jnp.max(cand, axis=0, keepdims=True)
            pick = jnp.min(jnp.where(cand == mx, blk, float(nb)), axis=0, keepdims=True)
            hit = blk == pick
            sel = sel | hit
            cand = jnp.where(hit, REMOVED, cand)
        bias = jnp.where(sel & (blk < jf), 0.0, NEG_INF)
        if nb < hd:
            bias = jnp.concatenate([bias, jnp.zeros((hd - nb, bs), f32)], axis=0)
        qbt = (qf * qscale).T.astype(bf16)
        qa_ref[h] = jnp.concatenate([qbt, bias.astype(bf16)], axis=0)

        s = _dot(k_ref[pl.ds(start_j, bs), cs], qbt)
        s = jnp.where(krow <= qcol, s, NEG_INF)
        m0 = jnp.max(s, axis=0, keepdims=True)
        p = jnp.exp2(s - m0)
        l0 = jnp.sum(p, axis=0, keepdims=True)
        acc0 = _dot(vt_ref[j, cs, :], p.astype(bf16))
        carry0 += [m0, l0, acc0]

    lane = lax.broadcasted_iota(jnp.int32, (bs, hd), 1)

    def block_scores(n, h):
        start = pl.multiple_of(n * bs, bs)
        onehot = (lane == n).astype(bf16)
        ka = jnp.concatenate([k_ref[pl.ds(start, bs), h * hd:(h + 1) * hd], onehot], axis=1)
        return _dot(ka, qa_ref[h])

    ones_rows = jnp.ones((SUM_ROWS, bs), bf16)

    def weighted_values(n, h, pb):
        return _dot(jnp.concatenate([vt_ref[n, h * hd:(h + 1) * hd, :], ones_rows], axis=0), pb)

    def stage(n, s_in, s_out, p_prev, p_out, stats):
        nn = jnp.minimum(n + 1, nb - 1)
        prev = jnp.clip(n - 1, 0, nb - 1)
        new = []
        for h in range(hg):
            m, alpha_p = stats[2 * h:2 * h + 2]
            acc_ref[h] = alpha_p * acc_ref[h] + weighted_values(prev, h, p_prev[h])
            s = s_in[h]
            m_new = jnp.maximum(m, jnp.max(s, axis=0, keepdims=True))
            p_out[h] = jnp.exp2((s - m_new).astype(bf16))
            s_out[h] = block_scores(nn, h)
            new += [m_new, jnp.exp2(m - m_new)]
        return new

    per_trip = ATTN_BLOCKS_PER_TRIP

    def body(i, stats):
        for u in range(0, per_trip, 2):
            stats = stage(per_trip * i + u, sa_ref, sb_ref, pb_ref, pa_ref, stats)
            stats = stage(per_trip * i + u + 1, sb_ref, sa_ref, pa_ref, pb_ref, stats)
        return tuple(stats)

    init = []
    for h in range(hg):
        m0, l0, acc0 = carry0[3 * h:3 * h + 3]
        acc_ref[h] = jnp.concatenate([acc0, jnp.broadcast_to(l0, (SUM_ROWS, bs))], axis=0)
        sa_ref[h] = block_scores(0, h)
        pb_ref[h] = jnp.zeros((bs, bs), bf16)
        init += [m0, jnp.ones_like(m0)]
    trips = (j + per_trip - 1) // per_trip
    res = lax.fori_loop(0, trips, body, tuple(init))
    last = jnp.clip(per_trip * trips - 1, 0, nb - 1)
    for h in range(hg):
        alpha_p = res[2 * h + 1]
        acc = alpha_p * acc_ref[h] + weighted_values(last, h, pb_ref[h])
        o_ref[:, h * hd:(h + 1) * hd] = (acc[:hd] / acc[hd:hd + 1]).T.astype(o_ref.dtype)


def _attention(q, k, vt, km, batch, seq):
    t, d = q.shape
    nh = d // HEAD_DIM
    nb = seq // MOBA_BLOCK
    hg = ATTN_HEADS_PER_STEP
    assert nb <= HEAD_DIM and nh % hg == 0
    w = hg * HEAD_DIM
    return pl.pallas_call(
        functools.partial(_attn_kernel, hg=hg),
        grid=(batch, nh // hg, nb),
        in_specs=[
            pl.BlockSpec((MOBA_BLOCK, w), lambda b, g, j: (b * nb + j, g)),
            pl.BlockSpec((seq, w), lambda b, g, j: (b, g), pipeline_mode=pl.Buffered(1)),
            pl.BlockSpec((nb, w, MOBA_BLOCK), lambda b, g, j: (b, g, 0), pipeline_mode=pl.Buffered(1)),
            pl.BlockSpec((nb, w), lambda b, g, j: (b, g)),
        ],
        out_specs=pl.BlockSpec((MOBA_BLOCK, w), lambda b, g, j: (b * nb + j, g)),
        out_shape=jax.ShapeDtypeStruct((t, d), bf16),
        scratch_shapes=[
            pltpu.VMEM((hg, 2 * HEAD_DIM, MOBA_BLOCK), bf16),
            pltpu.VMEM((hg, MOBA_BLOCK, MOBA_BLOCK), f32), pltpu.VMEM((hg, MOBA_BLOCK, MOBA_BLOCK), f32),
            pltpu.VMEM((hg, MOBA_BLOCK, MOBA_BLOCK), bf16), pltpu.VMEM((hg, MOBA_BLOCK, MOBA_BLOCK), bf16),
            pltpu.VMEM((hg, HEAD_DIM + SUM_ROWS, MOBA_BLOCK), f32),
        ],
        compiler_params=_params("parallel", "parallel", "arbitrary"),
        name="attn",
    )(q, k, vt, km)


def _oproj_router_kernel(h_ref, a_ref, wo_ref, g_ref, r_ref, h_out, hn_out, route_out, *, n_exp):
    h4 = h_ref[...] + _dot(a_ref[...], wo_ref[...])
    h_out[...] = h4
    hn = _rms(h4, g_ref[...])
    hn_out[...] = hn
    logits = jnp.dot(hn, r_ref[...], precision=lax.Precision.HIGHEST, preferred_element_type=f32)
    lane = lax.broadcasted_iota(jnp.int32, logits.shape, 1).astype(f32)
    cand = jnp.where(lane < n_exp, logits, NEG_INF)
    m1 = jnp.max(cand, axis=1, keepdims=True)
    i1 = jnp.min(jnp.where(cand == m1, lane, float(LANES)), axis=1, keepdims=True)
    cand = jnp.where(lane == i1, REMOVED, cand)
    m2 = jnp.max(cand, axis=1, keepdims=True)
    i2 = jnp.min(jnp.where(cand == m2, lane, float(LANES)), axis=1, keepdims=True)
    e2 = jnp.exp(m2 - m1)
    den = 1.0 + e2
    route_out[...] = jnp.where(lane == 0, i1, jnp.where(lane == 1, i2, jnp.where(
        lane == 2, 1.0 / den, jnp.where(lane == 3, e2 / den, 0.0))))


def _oproj_router_layer(h, attn, w_o, norm, router):
    t, d = h.shape
    tm = ROW_TILE
    n_exp = router.shape[1]
    r_pad = jnp.zeros((d, LANES), f32).at[:, :n_exp].set(router)
    row = lambda i: (i, 0)
    return pl.pallas_call(
        functools.partial(_oproj_router_kernel, n_exp=n_exp),
        grid=(t // tm,),
        in_specs=[pl.BlockSpec((tm, d), row), pl.BlockSpec((tm, d), row), _const_spec((d, d)),
                  _const_spec((1, d)), _const_spec((d, LANES))],
        out_specs=[pl.BlockSpec((tm, d), row), pl.BlockSpec((tm, d), row), pl.BlockSpec((tm, LANES), row)],
        out_shape=[jax.ShapeDtypeStruct((t, d), f32), jax.ShapeDtypeStruct((t, d), f32),
                   jax.ShapeDtypeStruct((t, LANES), f32)],
        compiler_params=_params("parallel"),
        name="oproj_router",
    )(h, attn, w_o.astype(bf16), norm.reshape(1, d), r_pad)


def _sc_gather(x, idx):
    n = idx.shape[0]
    assert x.shape[1] == SC_ROW and n % SC_WINDOW == 0
    mesh = plsc.VectorSubcoreMesh(core_axis_name="core", subcore_axis_name="subcore")

    @pl.kernel(out_type=jax.ShapeDtypeStruct((n, SC_ROW), x.dtype), mesh=mesh, scratch_types=[])
    def gather(x_hbm, i_hbm, o_hbm):
        def body(i_vmem, o_vmem):
            pltpu.sync_copy(x_hbm.at[i_vmem.at[0]], o_vmem)

        pltpu.emit_pipeline(
            body,
            grid=(n // SC_WINDOW,),
            in_specs=[pl.BlockSpec((1, SC_WINDOW), index_map=lambda i: (0, i))],
            out_specs=[pl.BlockSpec((SC_WINDOW, SC_ROW), index_map=lambda i: (i, 0))],
            core_axis_name=("core", "subcore"),
            dimension_semantics=(pltpu.PARALLEL,),
        )(i_hbm, o_hbm)

    return gather(x, idx.reshape(1, n))


def _pieces(a):
    n, d = a.shape
    return a.reshape(n // SUBLANES, SUBLANES, d // SC_ROW, SC_ROW).transpose(0, 2, 1, 3).reshape(-1, SC_ROW)


def _unpieces(p, d):
    per = d // SC_ROW
    n = p.shape[0] // per
    return p.reshape(n // SUBLANES, per, SUBLANES, SC_ROW).transpose(0, 2, 1, 3).reshape(n, d)


def _gather_tokens(x, rows):
    n, d = x.shape
    m, k = rows.shape
    per = d // SC_ROW
    assert n % SUBLANES == 0 and m % SUBLANES == 0
    col = jnp.arange(per, dtype=jnp.int32) * SUBLANES
    src = (rows // SUBLANES * (SUBLANES * per) + rows % SUBLANES)[:, :, None] + col
    idx = src.reshape(m // SUBLANES, SUBLANES, k * per).transpose(0, 2, 1).reshape(-1)
    return _unpieces(_sc_gather(_pieces(x), idx), k * d)


def _expert_kernel(te_ref, tv_ref, x_ref, w1_ref, w3_ref, w2_ref, o_ref, xb_ref, *, nf):
    i = pl.program_id(0)
    f = pl.program_id(1)
    valid = tv_ref[i] > 0

    @pl.when(f == 0)
    def _():
        xb_ref[...] = x_ref[...].astype(bf16)
        o_ref[...] = jnp.zeros_like(o_ref)

    @pl.when(valid)
    def _():
        x = xb_ref[...]
        a = _dot(x, w1_ref[...])
        b = _dot(x, w3_ref[...])
        o_ref[...] += _dot((jax.nn.silu(a) * b).astype(bf16), w2_ref[...])


def _route_tables(e12, n_exp, tm):
    nslots = e12.size
    ef = e12.reshape(-1)
    onehot = (ef[:, None] == jnp.arange(n_exp, dtype=jnp.int32)[None, :]).astype(jnp.int32)
    counts = jnp.sum(onehot, axis=0)
    tiles_e = (counts + tm - 1) // tm
    tile_end = jnp.cumsum(tiles_e)
    tile_start = tile_end - tiles_e
    group_start = jnp.cumsum(counts) - counts
    nt = nslots // tm + n_exp
    total = tile_end[-1]
    ti = jnp.arange(nt, dtype=jnp.int32)
    tv = (ti < total).astype(jnp.int32)
    tc = jnp.minimum(ti, total - 1)
    te = jnp.minimum(jnp.searchsorted(tile_end, tc, side="right").astype(jnp.int32), n_exp - 1)
    order = jnp.argsort(ef, stable=True).astype(jnp.int32)
    rank = ((tc - tile_start[te]) * tm)[:, None] + jnp.arange(tm, dtype=jnp.int32)[None, :]
    real = (rank < counts[te][:, None]) & (tv[:, None] > 0)
    sidx = jnp.clip(group_start[te][:, None] + rank, 0, nslots - 1)
    src = jnp.where(real, order[sidx] // TOP_K_EXPERTS, 0).reshape(-1, 1)
    before = jnp.sum((jnp.cumsum(onehot, axis=0) - onehot) * onehot, axis=1)
    pos = (tile_start[ef] * tm + before).reshape(e12.shape)
    return te, tv, src, pos, nt


def _experts_layer(hn, e12, w1, w3, w2):
    t, d = hn.shape
    n_exp, _, ff = w1.shape
    tm = ROW_TILE
    fc = _ff_chunk(ff, 896)
    nf = ff // fc
    te, tv, src, pos, nt = _route_tables(e12, n_exp, tm)
    xs = _gather_tokens(hn, src)
    w1r, w3r, w2r = w1.astype(bf16), w3.astype(bf16), w2.astype(bf16)

    def fsel(i, f, tv_r):
        return jnp.where(tv_r[i] > 0, f, nf - 1)

    def wmap_in(i, f, te_r, tv_r):
        return (te_r[i], 0, fsel(i, f, tv_r))

    def wmap_out(i, f, te_r, tv_r):
        return (te_r[i], fsel(i, f, tv_r), 0)

    grid_spec = pltpu.PrefetchScalarGridSpec(
        num_scalar_prefetch=2,
        grid=(nt, nf),
        in_specs=[
            pl.BlockSpec((tm, d), lambda i, f, *_: (i, 0)),
            pl.BlockSpec((None, d, fc), wmap_in),
            pl.BlockSpec((None, d, fc), wmap_in),
            pl.BlockSpec((None, fc, d), wmap_out),
        ],
        out_specs=pl.BlockSpec((tm, d), lambda i, f, *_: (i, 0)),
        scratch_shapes=[pltpu.VMEM((tm, d), bf16)],
    )
    ys = pl.pallas_call(
        functools.partial(_expert_kernel, nf=nf),
        grid_spec=grid_spec,
        out_shape=jax.ShapeDtypeStruct((nt * tm, d), f32),
        compiler_params=_params("parallel", "arbitrary"),
        name="experts",
    )(te, tv, xs, w1r, w3r, w2r)
    return _gather_tokens(ys, pos)


def _final_kernel(h_ref, y_ref, route_ref, p_ref, pn_ref, wg_ref, wp_ref, fn_ref, o_ref):
    d = h_ref.shape[1]
    route = route_ref[...]
    h5 = h_ref[...] + route[:, 2:3] * y_ref[:, :d] + route[:, 3:4] * y_ref[:, d:]
    hn = _rms(h5, pn_ref[...]).astype(bf16)
    h6 = h5 + jax.nn.sigmoid(_dot(hn, wg_ref[...])) * _dot(p_ref[...].astype(bf16), wp_ref[...])
    o_ref[...] = _rms(h6, fn_ref[...])


def _final_layer(h, y2, route, p_i, ple_norm, ple_gate, ple_proj, final_norm):
    t, d = h.shape
    pd = p_i.shape[1]
    tm = ROW_TILE
    row = lambda i: (i, 0)
    return pl.pallas_call(
        _final_kernel,
        grid=(t // tm,),
        in_specs=[pl.BlockSpec((tm, d), row), pl.BlockSpec((tm, TOP_K_EXPERTS * d), row),
                  pl.BlockSpec((tm, LANES), row), pl.BlockSpec((tm, pd), row),
                  _const_spec((1, d)), _const_spec((d, d)), _const_spec((pd, d)), _const_spec((1, d))],
        out_specs=pl.BlockSpec((tm, d), row),
        out_shape=jax.ShapeDtypeStruct((t, d), f32),
        compiler_params=_params("parallel"),
        name="final",
    )(h, y2, route, p_i, ple_norm.reshape(1, d), ple_gate.astype(bf16), ple_proj.astype(bf16),
      final_norm.reshape(1, d))


def kernel(x, p, pool_norm, pool_w, pool_scale, kv_norm, w_k, w_v, attn_norm, w_q, w_o, ffn_norm, ffn_w1, ffn_w3, ffn_w2, router, exp_w1, exp_w3, exp_w2, ple_norm, ple_gate, ple_proj, final_norm):
    batch, seq, d = x.shape
    t = batch * seq
    assert seq % ROW_TILE == 0 and ROW_TILE % MOBA_BLOCK == 0 and d % HEAD_DIM == 0
    assert p.shape[0] == 2 and router.shape[2] <= LANES
    h = x.reshape(t, d)
    pf = p.reshape(p.shape[0], t, p.shape[-1])

    h = _pool_layer(h, pool_norm[0], pool_w[0], pool_scale[0], seq)
    h = _swiglu_layer(h, ffn_norm[0], ffn_w1[0], ffn_w3[0], ffn_w2[0])
    h, q, k, v, km = _ple_qkv_layer(h, pf[0], ple_norm[0], ple_gate[0], ple_proj[0],
                                    kv_norm, w_k, w_v, attn_norm[0], w_q[0], seq)
    attn = _attention(q, k, v, km, batch, seq)
    h, hn, route = _oproj_router_layer(h, attn, w_o[0], ffn_norm[1], router[0])
    e12 = route[:, :TOP_K_EXPERTS].astype(jnp.int32)
    y = _experts_layer(hn, e12, exp_w1[0], exp_w3[0], exp_w2[0])
    out = _final_layer(h, y, route, pf[1], ple_norm[1], ple_gate[1],
                       ple_proj[1], final_norm)
    return out.reshape(batch, seq, d)
```

```python
import functools

import jax
import jax.numpy as jnp
from jax import lax
from jax.experimental import pallas as pl
from jax.experimental.pallas import tpu as pltpu
from jax.experimental.pallas import tpu_sc as plsc

POOL_WINDOWS = (2, 4, 8, 16)
HEAD_DIM = 128
MOBA_BLOCK = 256
MOBA_TOPK = 3
ROPE_THETA = 500000.0
ROPE_DIM = HEAD_DIM // 4
TOP_K_EXPERTS = 2
RMS_EPS = 1e-6
NEG_INF = -1e30
REMOVED = -3e38

LANES = 128
SUBLANES = 8
ROW_TILE = 512
POOL_SUB = 128
ATTN_HEADS_PER_STEP = 4
ATTN_BLOCKS_PER_TRIP = 4
SUM_ROWS = 16
LOG2E = 1.4426950408889634
EXPERT_FF_CHUNK = 1792
SC_ROW = 128
SC_WINDOW = 128
VMEM_LIMIT = 56 * 1024 * 1024

bf16 = jnp.bfloat16
f32 = jnp.float32


def _dot(a, b):
    return jnp.dot(a, b, preferred_element_type=f32)


def _dot_split(a, b):
    a_hi = a.astype(bf16)
    a_lo = (a - a_hi.astype(f32)).astype(bf16)
    b_hi = b.astype(bf16)
    b_lo = (b - b_hi.astype(f32)).astype(bf16)
    return _dot(a_hi, b_hi) + (_dot(a_lo, b_hi) + _dot(a_hi, b_lo))


def _rms(x, g):
    var = jnp.mean(x * x, axis=-1, keepdims=True)
    return x * lax.rsqrt(var + RMS_EPS) * g


def _params(*sem):
    return pltpu.CompilerParams(dimension_semantics=sem, vmem_limit_bytes=VMEM_LIMIT)


def _const_spec(shape):
    nd = len(shape)
    return pl.BlockSpec(shape, lambda *_: (0,) * nd)


def _pool_kernel(x_ref, halo_ref, g_ref, pw_ref, ps_ref, o_ref, pooled_ref, *, ts, seq):
    i = pl.program_id(0)
    g = g_ref[...]
    x = x_ref[...]
    xn = _rms(x, g)
    keep = jnp.where((i * ts) % seq == 0, 0.0, 1.0)
    hnb = (_rms(halo_ref[...], g) * keep).astype(bf16)
    xnb = xn.astype(bf16)
    gd = x.shape[1] // len(POOL_WINDOWS)
    r = lax.broadcasted_iota(jnp.int32, (POOL_SUB, 2 * POOL_SUB), 0)
    c = lax.broadcasted_iota(jnp.int32, (POOL_SUB, 2 * POOL_SUB), 1)
    dist = r + POOL_SUB - c
    bands = [((dist >= 0) & (dist < w)).astype(bf16) for w in POOL_WINDOWS]
    rows = lax.broadcasted_iota(jnp.int32, (POOL_SUB, 1), 0)
    for sb in range(ts // POOL_SUB):
        lo, hi = sb * POOL_SUB, (sb + 1) * POOL_SUB
        prev = hnb if sb == 0 else xnb[lo - POOL_SUB:lo]
        ext = jnp.concatenate([prev, xnb[lo:hi]], axis=0)
        tpos = (i * ts + lo) % seq + rows
        for gi, w in enumerate(POOL_WINDOWS):
            cs = slice(gi * gd, (gi + 1) * gd)
            wsum = _dot(bands[gi], ext[:, cs])
            cnt = jnp.minimum(tpos + 1, w).astype(f32)
            pooled_ref[lo:hi, cs] = (wsum / cnt - xn[lo:hi, cs]).astype(bf16)
    for gi in range(len(POOL_WINDOWS)):
        cs = slice(gi * gd, (gi + 1) * gd)
        mixed = _dot(pooled_ref[:, cs], pw_ref[gi])
        o_ref[:, cs] = x[:, cs] + mixed * ps_ref[:, cs]


def _pool_layer(h, norm, pool_w, pool_scale, seq):
    t, d = h.shape
    ts = ROW_TILE
    ng = len(POOL_WINDOWS)
    gd = d // ng
    per = ts // POOL_SUB
    return pl.pallas_call(
        functools.partial(_pool_kernel, ts=ts, seq=seq),
        grid=(t // ts,),
        in_specs=[
            pl.BlockSpec((ts, d), lambda i: (i, 0)),
            pl.BlockSpec((POOL_SUB, d), lambda i: (jnp.maximum(i * per - 1, 0), 0)),
            _const_spec((1, d)),
            _const_spec((ng, gd, gd)),
            _const_spec((1, d)),
        ],
        out_specs=pl.BlockSpec((ts, d), lambda i: (i, 0)),
        out_shape=jax.ShapeDtypeStruct((t, d), f32),
        scratch_shapes=[pltpu.VMEM((ts, d), bf16)],
        compiler_params=_params("parallel"),
        name="pool",
    )(h, h, norm.reshape(1, d), pool_w.astype(bf16), pool_scale.reshape(1, d))


def _swiglu_kernel(h_ref, g_ref, w1_ref, w3_ref, w2_ref, o_ref):
    x = h_ref[...]
    hn = _rms(x, g_ref[...]).astype(bf16)
    a = _dot(hn, w1_ref[...])
    b = _dot(hn, w3_ref[...])
    o_ref[...] = x + _dot((jax.nn.silu(a) * b).astype(bf16), w2_ref[...])


def _ff_chunk(ff, target):
    units = ff // LANES
    best = 1
    for k in range(1, units + 1):
        if units % k == 0 and k * LANES <= target:
            best = k
    return best * LANES


def _swiglu_layer(h, norm, w1, w3, w2):
    t, d = h.shape
    ff = w1.shape[1]
    tm = ROW_TILE
    w1r, w3r, w2r = w1.astype(bf16), w3.astype(bf16), w2.astype(bf16)
    resident = lambda shape: pl.BlockSpec(shape, lambda i: (0, 0), pipeline_mode=pl.Buffered(1))
    return pl.pallas_call(
        _swiglu_kernel,
        grid=(t // tm,),
        in_specs=[
            pl.BlockSpec((tm, d), lambda i: (i, 0)),
            _const_spec((1, d)),
            resident((d, ff)), resident((d, ff)), resident((ff, d)),
        ],
        out_specs=pl.BlockSpec((tm, d), lambda i: (i, 0)),
        out_shape=jax.ShapeDtypeStruct((t, d), f32),
        compiler_params=_params("parallel"),
        name="swiglu",
    )(h, norm.reshape(1, d), w1r, w3r, w2r)


def _rope_tables(seq):
    half = ROPE_DIM // 2
    inv_freq = jnp.float32(ROPE_THETA) ** (-(jnp.arange(0, ROPE_DIM, 2, dtype=f32) / ROPE_DIM))
    ang = jnp.arange(seq, dtype=f32)[:, None] * inv_freq[None, :]
    cos, sin = jnp.cos(ang), jnp.sin(ang)
    ones = jnp.ones((seq, HEAD_DIM - ROPE_DIM), f32)
    zeros = jnp.zeros((seq, HEAD_DIM - half), f32)
    cos_t = jnp.concatenate([cos, cos, ones], axis=1)
    up_t = jnp.concatenate([-sin, zeros], axis=1)
    dn_t = jnp.concatenate([jnp.zeros((seq, half), f32), sin, jnp.zeros((seq, HEAD_DIM - ROPE_DIM), f32)], axis=1)
    return cos_t, up_t, dn_t


def _rope(xh, cos_t, up_t, dn_t):
    half = ROPE_DIM // 2
    return (xh * cos_t + pltpu.roll(xh, HEAD_DIM - half, 1) * up_t + pltpu.roll(xh, half, 1) * dn_t)


def _ple_qkv_kernel(h_ref, p_ref, pn_ref, wg_ref, wp_ref, kvn_ref, wk_ref, wv_ref, an_ref, wq_ref,
                    cos_ref, up_ref, dn_ref, h_out, q_out, k_out, v_out, km_out):
    h = h_ref[...]
    hn = _rms(h, pn_ref[...]).astype(bf16)
    h3 = h + jax.nn.sigmoid(_dot(hn, wg_ref[...])) * _dot(p_ref[...].astype(bf16), wp_ref[...])
    h_out[...] = h3
    base = h3 * lax.rsqrt(jnp.mean(h3 * h3, axis=-1, keepdims=True) + RMS_EPS)
    kn = (base * kvn_ref[...]).astype(bf16)
    qn = (base * an_ref[...]).astype(bf16)
    v = _dot(kn, wv_ref[...])
    for bi in range(h.shape[0] // MOBA_BLOCK):
        v_out[bi] = v[bi * MOBA_BLOCK:(bi + 1) * MOBA_BLOCK].T.astype(bf16)
    k = _dot(kn, wk_ref[...])
    q = _dot(qn, wq_ref[...])
    cos_t, up_t, dn_t = cos_ref[...], up_ref[...], dn_ref[...]
    tm, d = h.shape
    sub = km_out.shape[0] // (tm // MOBA_BLOCK)
    for hh in range(d // HEAD_DIM):
        cs = slice(hh * HEAD_DIM, (hh + 1) * HEAD_DIM)
        q_out[:, cs] = _rope(q[:, cs], cos_t, up_t, dn_t)
        kr = _rope(k[:, cs], cos_t, up_t, dn_t)
        k_out[:, cs] = kr.astype(bf16)
        for bi in range(tm // MOBA_BLOCK):
            m = jnp.mean(kr[bi * MOBA_BLOCK:(bi + 1) * MOBA_BLOCK], axis=0, keepdims=True)
            km_out[bi * sub:(bi + 1) * sub, cs] = jnp.broadcast_to(m, (sub, HEAD_DIM))


def _ple_qkv_layer(h, p_i, ple_norm, ple_gate, ple_proj, kv_norm, w_k, w_v, attn_norm, w_q, seq):
    t, d = h.shape
    pd = p_i.shape[1]
    tm = ROW_TILE
    sub = 8
    cos_t, up_t, dn_t = _rope_tables(seq)
    tiles_per_seq = seq // tm
    row = lambda i: (i, 0)
    tab = pl.BlockSpec((tm, HEAD_DIM), lambda i: (i % tiles_per_seq, 0))
    nkm = t // MOBA_BLOCK * sub
    outs = pl.pallas_call(
        _ple_qkv_kernel,
        grid=(t // tm,),
        in_specs=[
            pl.BlockSpec((tm, d), row), pl.BlockSpec((tm, pd), row),
            _const_spec((1, d)), _const_spec((d, d)), _const_spec((pd, d)),
            _const_spec((1, d)), _const_spec((d, d)), _const_spec((d, d)),
            _const_spec((1, d)), _const_spec((d, d)),
            tab, tab, tab,
        ],
        out_specs=[
            pl.BlockSpec((tm, d), row), pl.BlockSpec((tm, d), row),
            pl.BlockSpec((tm, d), row),
            pl.BlockSpec((tm // MOBA_BLOCK, d, MOBA_BLOCK), lambda i: (i, 0, 0)),
            pl.BlockSpec((tm // MOBA_BLOCK * sub, d), row),
        ],
        out_shape=[
            jax.ShapeDtypeStruct((t, d), f32), jax.ShapeDtypeStruct((t, d), f32),
            jax.ShapeDtypeStruct((t, d), bf16),
            jax.ShapeDtypeStruct((t // MOBA_BLOCK, d, MOBA_BLOCK), bf16),
            jax.ShapeDtypeStruct((nkm, d), f32),
        ],
        compiler_params=_params("parallel"),
        name="ple_qkv",
    )(h, p_i, ple_norm.reshape(1, d), ple_gate.astype(bf16), ple_proj.astype(bf16),
      kv_norm.reshape(1, d), w_k.astype(bf16), w_v.astype(bf16),
      attn_norm.reshape(1, d), w_q.astype(bf16), cos_t, up_t, dn_t)
    h3, q, k, v, km = outs
    return h3, q, k, v, km.reshape(t // MOBA_BLOCK, sub, d)[:, 0, :]


def _attn_kernel(q_ref, k_ref, vt_ref, km_ref, o_ref, qa_ref, sa_ref, sb_ref, pa_ref, pb_ref, acc_ref, *, hg):
    j = pl.program_id(2)
    nb = km_ref.shape[0]
    bs = q_ref.shape[0]
    hd = HEAD_DIM
    qscale = (hd ** -0.5) * LOG2E
    blk = lax.broadcasted_iota(jnp.int32, (nb, bs), 0).astype(f32)
    jf = j.astype(f32)
    krow = lax.broadcasted_iota(jnp.int32, (bs, bs), 0)
    qcol = lax.broadcasted_iota(jnp.int32, (bs, bs), 1)
    start_j = pl.multiple_of(j * bs, bs)

    carry0 = []
    for h in range(hg):
        cs = slice(h * hd, (h + 1) * hd)
        qf = q_ref[:, cs]
        gate = lax.dot_general(km_ref[:, cs], qf, (((1,), (1,)), ((), ())),
                               precision=lax.Precision.HIGHEST, preferred_element_type=f32)
        cand = jnp.where(blk < jf, gate, NEG_INF)
        sel = blk < 0.0
        for _ in range(min(MOBA_TOPK, nb)):
            mx = jnp.max(cand, axis=0, keepdims=True)
            pick = jnp.min(jnp.where(cand == mx, blk, float(nb)), axis=0, keepdims=True)
            hit = blk == pick
            sel = sel | hit
            cand = jnp.where(hit, REMOVED, cand)
        bias = jnp.where(sel & (blk < jf), 0.0, NEG_INF)
        if nb < hd:
            bias = jnp.concatenate([bias, jnp.zeros((hd - nb, bs), f32)], axis=0)
        qbt = (qf * qscale).T.astype(bf16)
        qa_ref[h] = jnp.concatenate([qbt, bias.astype(bf16)], axis=0)

        s = _dot(k_ref[pl.ds(start_j, bs), cs], qbt)
        s = jnp.where(krow <= qcol, s, NEG_INF)
        m0 = jnp.max(s, axis=0, keepdims=True)
        p = jnp.exp2(s - m0)
        l0 = jnp.sum(p, axis=0, keepdims=True)
        acc0 = _dot(vt_ref[j, cs, :], p.astype(bf16))
        carry0 += [m0, l0, acc0]

    lane = lax.broadcasted_iota(jnp.int32, (bs, hd), 1)

    def block_scores(n, h):
        start = pl.multiple_of(n * bs, bs)
        onehot = (lane == n).astype(bf16)
        ka = jnp.concatenate([k_ref[pl.ds(start, bs), h * hd:(h + 1) * hd], onehot], axis=1)
        return _dot(ka, qa_ref[h])

    ones_rows = jnp.ones((SUM_ROWS, bs), bf16)

    def weighted_values(n, h, pb):
        return _dot(jnp.concatenate([vt_ref[n, h * hd:(h + 1) * hd, :], ones_rows], axis=0), pb)

    def stage(n, s_in, s_out, p_prev, p_out, stats):
        nn = jnp.minimum(n + 1, nb - 1)
        prev = jnp.clip(n - 1, 0, nb - 1)
        new = []
        for h in range(hg):
            m, alpha_p = stats[2 * h:2 * h + 2]
            acc_ref[h] = alpha_p * acc_ref[h] + weighted_values(prev, h, p_prev[h])
            s = s_in[h]
            m_new = jnp.maximum(m, jnp.max(s, axis=0, keepdims=True))
            p_out[h] = jnp.exp2((s - m_new).astype(bf16))
            s_out[h] = block_scores(nn, h)
            new += [m_new, jnp.exp2(m - m_new)]
        return new

    per_trip = ATTN_BLOCKS_PER_TRIP

    def body(i, stats):
        for u in range(0, per_trip, 2):
            stats = stage(per_trip * i + u, sa_ref, sb_ref, pb_ref, pa_ref, stats)
            stats = stage(per_trip * i + u + 1, sb_ref, sa_ref, pa_ref, pb_ref, stats)
        return tuple(stats)

    init = []
    for h in range(hg):
        m0, l0, acc0 = carry0[3 * h:3 * h + 3]
        acc_ref[h] = jnp.concatenate([acc0, jnp.broadcast_to(l0, (SUM_ROWS, bs))], axis=0)
        sa_ref[h] = block_scores(0, h)
        pb_ref[h] = jnp.zeros((bs, bs), bf16)
        init += [m0, jnp.ones_like(m0)]
    trips = (j + per_trip - 1) // per_trip
    res = lax.fori_loop(0, trips, body, tuple(init))
    last = jnp.clip(per_trip * trips - 1, 0, nb - 1)
    for h in range(hg):
        alpha_p = res[2 * h + 1]
        acc = alpha_p * acc_ref[h] + weighted_values(last, h, pb_ref[h])
        o_ref[:, h * hd:(h + 1) * hd] = (acc[:hd] / acc[hd:hd + 1]).T.astype(o_ref.dtype)


def _attention(q, k, vt, km, batch, seq):
    t, d = q.shape
    nh = d // HEAD_DIM
    nb = seq // MOBA_BLOCK
    hg = ATTN_HEADS_PER_STEP
    assert nb <= HEAD_DIM and nh % hg == 0
    w = hg * HEAD_DIM
    return pl.pallas_call(
        functools.partial(_attn_kernel, hg=hg),
        grid=(batch, nh // hg, nb),
        in_specs=[
            pl.BlockSpec((MOBA_BLOCK, w), lambda b, g, j: (b * nb + j, g)),
            pl.BlockSpec((seq, w), lambda b, g, j: (b, g), pipeline_mode=pl.Buffered(1)),
            pl.BlockSpec((nb, w, MOBA_BLOCK), lambda b, g, j: (b, g, 0), pipeline_mode=pl.Buffered(1)),
            pl.BlockSpec((nb, w), lambda b, g, j: (b, g)),
        ],
        out_specs=pl.BlockSpec((MOBA_BLOCK, w), lambda b, g, j: (b * nb + j, g)),
        out_shape=jax.ShapeDtypeStruct((t, d), bf16),
        scratch_shapes=[
            pltpu.VMEM((hg, 2 * HEAD_DIM, MOBA_BLOCK), bf16),
            pltpu.VMEM((hg, MOBA_BLOCK, MOBA_BLOCK), f32), pltpu.VMEM((hg, MOBA_BLOCK, MOBA_BLOCK), f32),
            pltpu.VMEM((hg, MOBA_BLOCK, MOBA_BLOCK), bf16), pltpu.VMEM((hg, MOBA_BLOCK, MOBA_BLOCK), bf16),
            pltpu.VMEM((hg, HEAD_DIM + SUM_ROWS, MOBA_BLOCK), f32),
        ],
        compiler_params=_params("parallel", "parallel", "arbitrary"),
        name="attn",
    )(q, k, vt, km)


def _oproj_router_kernel(h_ref, a_ref, wo_ref, g_ref, r_ref, h_out, hn_out, route_out, *, n_exp):
    h4 = h_ref[...] + _dot(a_ref[...], wo_ref[...])
    h_out[...] = h4
    hn = _rms(h4, g_ref[...])
    hn_out[...] = hn
    logits = _dot_split(hn, r_ref[...])
    lane = lax.broadcasted_iota(jnp.int32, logits.shape, 1).astype(f32)
    cand = jnp.where(lane < n_exp, logits, NEG_INF)
    m1 = jnp.max(cand, axis=1, keepdims=True)
    i1 = jnp.min(jnp.where(cand == m1, lane, float(LANES)), axis=1, keepdims=True)
    cand = jnp.where(lane == i1, REMOVED, cand)
    m2 = jnp.max(cand, axis=1, keepdims=True)
    i2 = jnp.min(jnp.where(cand == m2, lane, float(LANES)), axis=1, keepdims=True)
    e2 = jnp.exp(m2 - m1)
    den = 1.0 + e2
    route_out[...] = jnp.where(lane == 0, i1, jnp.where(lane == 1, i2, jnp.where(
        lane == 2, 1.0 / den, jnp.where(lane == 3, e2 / den, 0.0))))


def _oproj_router_layer(h, attn, w_o, norm, router):
    t, d = h.shape
    tm = ROW_TILE
    n_exp = router.shape[1]
    r_pad = jnp.zeros((d, LANES), f32).at[:, :n_exp].set(router)
    row = lambda i: (i, 0)
    return pl.pallas_call(
        functools.partial(_oproj_router_kernel, n_exp=n_exp),
        grid=(t // tm,),
        in_specs=[pl.BlockSpec((tm, d), row), pl.BlockSpec((tm, d), row), _const_spec((d, d)),
                  _const_spec((1, d)), _const_spec((d, LANES))],
        out_specs=[pl.BlockSpec((tm, d), row), pl.BlockSpec((tm, d), row), pl.BlockSpec((tm, LANES), row)],
        out_shape=[jax.ShapeDtypeStruct((t, d), f32), jax.ShapeDtypeStruct((t, d), f32),
                   jax.ShapeDtypeStruct((t, LANES), f32)],
        compiler_params=_params("parallel"),
        name="oproj_router",
    )(h, attn, w_o.astype(bf16), norm.reshape(1, d), r_pad)


def _sc_gather(x, idx):
    n = idx.shape[0]
    assert x.shape[1] == SC_ROW and n % SC_WINDOW == 0
    mesh = plsc.VectorSubcoreMesh(core_axis_name="core", subcore_axis_name="subcore")

    @pl.kernel(out_type=jax.ShapeDtypeStruct((n, SC_ROW), x.dtype), mesh=mesh, scratch_types=[])
    def gather(x_hbm, i_hbm, o_hbm):
        def body(i_vmem, o_vmem):
            pltpu.sync_copy(x_hbm.at[i_vmem.at[0]], o_vmem)

        pltpu.emit_pipeline(
            body,
            grid=(n // SC_WINDOW,),
            in_specs=[pl.BlockSpec((1, SC_WINDOW), index_map=lambda i: (0, i))],
            out_specs=[pl.BlockSpec((SC_WINDOW, SC_ROW), index_map=lambda i: (i, 0))],
            core_axis_name=("core", "subcore"),
            dimension_semantics=(pltpu.PARALLEL,),
        )(i_hbm, o_hbm)

    return gather(x, idx.reshape(n // LANES, LANES).reshape(1, n))


def _pieces(a):
    n, d = a.shape
    return a.reshape(n // SUBLANES, SUBLANES, d // SC_ROW, SC_ROW).transpose(0, 2, 1, 3).reshape(-1, SC_ROW)


def _unpieces(p, d):
    per = d // SC_ROW
    n = p.shape[0] // per
    return p.reshape(n // SUBLANES, per, SUBLANES, SC_ROW).transpose(0, 2, 1, 3).reshape(n, d)


def _gather_tokens(x, rows):
    n, d = x.shape
    m, k = rows.shape
    per = d // SC_ROW
    assert n % SUBLANES == 0 and m % SUBLANES == 0
    col = jnp.arange(per, dtype=jnp.int32) * SUBLANES
    src = (rows // SUBLANES * (SUBLANES * per) + rows % SUBLANES)[:, :, None] + col
    idx = src.reshape(m // SUBLANES, SUBLANES, k * per).transpose(0, 2, 1).reshape(-1)
    return _unpieces(_sc_gather(_pieces(x), idx), k * d)


def _expert_kernel(te_ref, tv_ref, x_ref, w1_ref, w3_ref, w2_ref, o_ref, xb_ref, *, nf):
    i = pl.program_id(0)
    f = pl.program_id(1)
    valid = tv_ref[i] > 0

    @pl.when(f == 0)
    def _():
        xb_ref[...] = x_ref[...].astype(bf16)
        o_ref[...] = jnp.zeros_like(o_ref)

    @pl.when(valid)
    def _():
        x = xb_ref[...]
        a = _dot(x, w1_ref[...])
        b = _dot(x, w3_ref[...])
        o_ref[...] += _dot((jax.nn.silu(a) * b).astype(bf16), w2_ref[...])


def _route_tables(e12, n_exp, tm):
    nslots = e12.size
    ef = e12.reshape(-1)
    onehot = (ef[:, None] == jnp.arange(n_exp, dtype=jnp.int32)[None, :]).astype(jnp.int32)
    counts = jnp.sum(onehot, axis=0)
    tiles_e = (counts + tm - 1) // tm
    tile_end = jnp.cumsum(tiles_e)
    tile_start = tile_end - tiles_e
    group_start = jnp.cumsum(counts) - counts
    nt = nslots // tm + n_exp
    total = tile_end[-1]
    ti = jnp.arange(nt, dtype=jnp.int32)
    tv = (ti < total).astype(jnp.int32)
    tc = jnp.minimum(ti, total - 1)
    te = jnp.minimum(jnp.searchsorted(tile_end, tc, side="right").astype(jnp.int32), n_exp - 1)
    order = jnp.argsort(ef, stable=True).astype(jnp.int32)
    rank = ((tc - tile_start[te]) * tm)[:, None] + jnp.arange(tm, dtype=jnp.int32)[None, :]
    real = (rank < counts[te][:, None]) & (tv[:, None] > 0)
    sidx = jnp.clip(group_start[te][:, None] + rank, 0, nslots - 1)
    src = jnp.where(real, order[sidx] // TOP_K_EXPERTS, 0).reshape(-1, 1)
    before = jnp.sum((jnp.cumsum(onehot, axis=0) - onehot) * onehot, axis=1)
    pos = (tile_start[ef] * tm + before).reshape(e12.shape)
    return te, tv, src, pos, nt


def _experts_layer(hn, e12, w1, w3, w2):
    t, d = hn.shape
    n_exp, _, ff = w1.shape
    tm = ROW_TILE
    fc = _ff_chunk(ff, EXPERT_FF_CHUNK)
    nf = ff // fc
    te, tv, src, pos, nt = _route_tables(e12, n_exp, tm)
    xs = _gather_tokens(hn, src)
    w1r, w3r, w2r = w1.astype(bf16), w3.astype(bf16), w2.astype(bf16)

    def fsel(i, f, tv_r):
        return jnp.where(tv_r[i] > 0, f, nf - 1)

    def wmap_in(i, f, te_r, tv_r):
        return (te_r[i], 0, fsel(i, f, tv_r))

    def wmap_out(i, f, te_r, tv_r):
        return (te_r[i], fsel(i, f, tv_r), 0)

    grid_spec = pltpu.PrefetchScalarGridSpec(
        num_scalar_prefetch=2,
        grid=(nt, nf),
        in_specs=[
            pl.BlockSpec((tm, d), lambda i, f, *_: (i, 0)),
            pl.BlockSpec((None, d, fc), wmap_in),
            pl.BlockSpec((None, d, fc), wmap_in),
            pl.BlockSpec((None, fc, d), wmap_out),
        ],
        out_specs=pl.BlockSpec((tm, d), lambda i, f, *_: (i, 0)),
        scratch_shapes=[pltpu.VMEM((tm, d), bf16)],
    )
    ys = pl.pallas_call(
        functools.partial(_expert_kernel, nf=nf),
        grid_spec=grid_spec,
        out_shape=jax.ShapeDtypeStruct((nt * tm, d), f32),
        compiler_params=_params("parallel", "arbitrary"),
        name="experts",
    )(te, tv, xs, w1r, w3r, w2r)
    return _gather_tokens(ys, pos)


def _final_kernel(h_ref, y_ref, route_ref, p_ref, pn_ref, wg_ref, wp_ref, fn_ref, o_ref):
    d = h_ref.shape[1]
    route = route_ref[...]
    h5 = h_ref[...] + route[:, 2:3] * y_ref[:, :d] + route[:, 3:4] * y_ref[:, d:]
    hn = _rms(h5, pn_ref[...]).astype(bf16)
    h6 = h5 + jax.nn.sigmoid(_dot(hn, wg_ref[...])) * _dot(p_ref[...].astype(bf16), wp_ref[...])
    o_ref[...] = _rms(h6, fn_ref[...])


def _final_layer(h, y2, route, p_i, ple_norm, ple_gate, ple_proj, final_norm):
    t, d = h.shape
    pd = p_i.shape[1]
    tm = ROW_TILE
    row = lambda i: (i, 0)
    return pl.pallas_call(
        _final_kernel,
        grid=(t // tm,),
        in_specs=[pl.BlockSpec((tm, d), row), pl.BlockSpec((tm, TOP_K_EXPERTS * d), row),
                  pl.BlockSpec((tm, LANES), row), pl.BlockSpec((tm, pd), row),
                  _const_spec((1, d)), _const_spec((d, d)), _const_spec((pd, d)), _const_spec((1, d))],
        out_specs=pl.BlockSpec((tm, d), row),
        out_shape=jax.ShapeDtypeStruct((t, d), f32),
        compiler_params=_params("parallel"),
        name="final",
    )(h, y2, route, p_i, ple_norm.reshape(1, d), ple_gate.astype(bf16), ple_proj.astype(bf16),
      final_norm.reshape(1, d))


def kernel(x, p, pool_norm, pool_w, pool_scale, kv_norm, w_k, w_v, attn_norm, w_q, w_o, ffn_norm, ffn_w1, ffn_w3, ffn_w2, router, exp_w1, exp_w3, exp_w2, ple_norm, ple_gate, ple_proj, final_norm):
    batch, seq, d = x.shape
    t = batch * seq
    assert seq % ROW_TILE == 0 and ROW_TILE % MOBA_BLOCK == 0 and d % HEAD_DIM == 0
    assert p.shape[0] == 2 and router.shape[2] <= LANES
    h = x.reshape(t, d)
    pf = p.reshape(p.shape[0], t, p.shape[-1])

    h = _pool_layer(h, pool_norm[0], pool_w[0], pool_scale[0], seq)
    h = _swiglu_layer(h, ffn_norm[0], ffn_w1[0], ffn_w3[0], ffn_w2[0])
    h, q, k, v, km = _ple_qkv_layer(h, pf[0], ple_norm[0], ple_gate[0], ple_proj[0],
                                    kv_norm, w_k, w_v, attn_norm[0], w_q[0], seq)
    attn = _attention(q, k, v, km, batch, seq)
    h, hn, route = _oproj_router_layer(h, attn, w_o[0], ffn_norm[1], router[0])
    e12 = route[:, :TOP_K_EXPERTS].astype(jnp.int32)
    y = _experts_layer(hn, e12, exp_w1[0], exp_w3[0], exp_w2[0])
    out = _final_layer(h, y, route, pf[1], ple_norm[1], ple_gate[1],
                       ple_proj[1], final_norm)
    return out.reshape(batch, seq, d)
```

```python
import functools

import jax
import jax.numpy as jnp
from jax import lax
from jax.experimental import pallas as pl
from jax.experimental.pallas import tpu as pltpu
from jax.experimental.pallas import tpu_sc as plsc

POOL_WINDOWS = (2, 4, 8, 16)
HEAD_DIM = 128
MOBA_BLOCK = 256
MOBA_TOPK = 3
ROPE_THETA = 500000.0
ROPE_DIM = HEAD_DIM // 4
TOP_K_EXPERTS = 2
RMS_EPS = 1e-6
NEG_INF = -1e30
REMOVED = -3e38

LANES = 128
SUBLANES = 8
ROW_TILE = 512
POOL_SUB = 128
ATTN_HEADS_PER_STEP = 4
ATTN_BLOCKS_PER_TRIP = 4
SUM_ROWS = 16
LOG2E = 1.4426950408889634
EXPERT_FF_CHUNK = 1792
SC_ROW = 128
SC_WINDOW = 128
VMEM_LIMIT = 56 * 1024 * 1024

bf16 = jnp.bfloat16
f32 = jnp.float32


def _dot(a, b):
    return jnp.dot(a, b, preferred_element_type=f32)


def _dot_split(a, b):
    a_hi = a.astype(bf16)
    a_lo = (a - a_hi.astype(f32)).astype(bf16)
    b_hi = b.astype(bf16)
    b_lo = (b - b_hi.astype(f32)).astype(bf16)
    return _dot(a_hi, b_hi) + (_dot(a_lo, b_hi) + _dot(a_hi, b_lo))


def _rms(x, g):
    var = jnp.mean(x * x, axis=-1, keepdims=True)
    return x * lax.rsqrt(var + RMS_EPS) * g


def _params(*sem):
    return pltpu.CompilerParams(dimension_semantics=sem, vmem_limit_bytes=VMEM_LIMIT)


def _const_spec(shape):
    nd = len(shape)
    return pl.BlockSpec(shape, lambda *_: (0,) * nd)


def _pool_kernel(x_ref, halo_ref, g_ref, pw_ref, ps_ref, o_ref, pooled_ref, *, ts, seq):
    i = pl.program_id(0)
    g = g_ref[...]
    x = x_ref[...]
    xn = _rms(x, g)
    keep = jnp.where((i * ts) % seq == 0, 0.0, 1.0)
    hnb = (_rms(halo_ref[...], g) * keep).astype(bf16)
    xnb = xn.astype(bf16)
    gd = x.shape[1] // len(POOL_WINDOWS)
    r = lax.broadcasted_iota(jnp.int32, (POOL_SUB, 2 * POOL_SUB), 0)
    c = lax.broadcasted_iota(jnp.int32, (POOL_SUB, 2 * POOL_SUB), 1)
    dist = r + POOL_SUB - c
    bands = [((dist >= 0) & (dist < w)).astype(bf16) for w in POOL_WINDOWS]
    rows = lax.broadcasted_iota(jnp.int32, (POOL_SUB, 1), 0)
    for sb in range(ts // POOL_SUB):
        lo, hi = sb * POOL_SUB, (sb + 1) * POOL_SUB
        prev = hnb if sb == 0 else xnb[lo - POOL_SUB:lo]
        ext = jnp.concatenate([prev, xnb[lo:hi]], axis=0)
        tpos = (i * ts + lo) % seq + rows
        for gi, w in enumerate(POOL_WINDOWS):
            cs = slice(gi * gd, (gi + 1) * gd)
            wsum = _dot(bands[gi], ext[:, cs])
            cnt = jnp.minimum(tpos + 1, w).astype(f32)
            pooled_ref[lo:hi, cs] = (wsum / cnt - xn[lo:hi, cs]).astype(bf16)
    for gi in range(len(POOL_WINDOWS)):
        cs = slice(gi * gd, (gi + 1) * gd)
        mixed = _dot(pooled_ref[:, cs], pw_ref[gi])
        o_ref[:, cs] = x[:, cs] + mixed * ps_ref[:, cs]


def _pool_layer(h, norm, pool_w, pool_scale, seq):
    t, d = h.shape
    ts = ROW_TILE
    ng = len(POOL_WINDOWS)
    gd = d // ng
    per = ts // POOL_SUB
    return pl.pallas_call(
        functools.partial(_pool_kernel, ts=ts, seq=seq),
        grid=(t // ts,),
        in_specs=[
            pl.BlockSpec((ts, d), lambda i: (i, 0)),
            pl.BlockSpec((POOL_SUB, d), lambda i: (jnp.maximum(i * per - 1, 0), 0)),
            _const_spec((1, d)),
            _const_spec((ng, gd, gd)),
            _const_spec((1, d)),
        ],
        out_specs=pl.BlockSpec((ts, d), lambda i: (i, 0)),
        out_shape=jax.ShapeDtypeStruct((t, d), f32),
        scratch_shapes=[pltpu.VMEM((ts, d), bf16)],
        compiler_params=_params("parallel"),
        name="pool",
    )(h, h, norm.reshape(1, d), pool_w.astype(bf16), pool_scale.reshape(1, d))


def _swiglu_kernel(h_ref, g_ref, w1_ref, w3_ref, w2_ref, o_ref):
    x = h_ref[...]
    hn = _rms(x, g_ref[...]).astype(bf16)
    a = _dot(hn, w1_ref[...])
    b = _dot(hn, w3_ref[...])
    o_ref[...] = x + _dot((jax.nn.silu(a) * b).astype(bf16), w2_ref[...])


def _ff_chunk(ff, target):
    units = ff // LANES
    best = 1
    for k in range(1, units + 1):
        if units % k == 0 and k * LANES <= target:
            best = k
    return best * LANES


def _swiglu_layer(h, norm, w1, w3, w2):
    t, d = h.shape
    ff = w1.shape[1]
    tm = ROW_TILE
    w1r, w3r, w2r = w1.astype(bf16), w3.astype(bf16), w2.astype(bf16)
    resident = lambda shape: pl.BlockSpec(shape, lambda i: (0, 0), pipeline_mode=pl.Buffered(1))
    return pl.pallas_call(
        _swiglu_kernel,
        grid=(t // tm,),
        in_specs=[
            pl.BlockSpec((tm, d), lambda i: (i, 0)),
            _const_spec((1, d)),
            resident((d, ff)), resident((d, ff)), resident((ff, d)),
        ],
        out_specs=pl.BlockSpec((tm, d), lambda i: (i, 0)),
        out_shape=jax.ShapeDtypeStruct((t, d), f32),
        compiler_params=_params("parallel"),
        name="swiglu",
    )(h, norm.reshape(1, d), w1r, w3r, w2r)


def _rope_tables(seq):
    half = ROPE_DIM // 2
    inv_freq = jnp.float32(ROPE_THETA) ** (-(jnp.arange(0, ROPE_DIM, 2, dtype=f32) / ROPE_DIM))
    ang = jnp.arange(seq, dtype=f32)[:, None] * inv_freq[None, :]
    cos, sin = jnp.cos(ang), jnp.sin(ang)
    ones = jnp.ones((seq, HEAD_DIM - ROPE_DIM), f32)
    zeros = jnp.zeros((seq, HEAD_DIM - half), f32)
    cos_t = jnp.concatenate([cos, cos, ones], axis=1)
    up_t = jnp.concatenate([-sin, zeros], axis=1)
    dn_t = jnp.concatenate([jnp.zeros((seq, half), f32), sin, jnp.zeros((seq, HEAD_DIM - ROPE_DIM), f32)], axis=1)
    return cos_t, up_t, dn_t


def _rope(xh, cos_t, up_t, dn_t):
    half = ROPE_DIM // 2
    return (xh * cos_t + pltpu.roll(xh, HEAD_DIM - half, 1) * up_t + pltpu.roll(xh, half, 1) * dn_t)


def _ple_qkv_kernel(h_ref, p_ref, pn_ref, wg_ref, wp_ref, kvn_ref, wk_ref, wv_ref, an_ref, wq_ref,
                    cos_ref, up_ref, dn_ref, h_out, q_out, k_out, v_out, km_out):
    h = h_ref[...]
    hn = _rms(h, pn_ref[...]).astype(bf16)
    h3 = h + jax.nn.sigmoid(_dot(hn, wg_ref[...])) * _dot(p_ref[...].astype(bf16), wp_ref[...])
    h_out[...] = h3
    base = h3 * lax.rsqrt(jnp.mean(h3 * h3, axis=-1, keepdims=True) + RMS_EPS)
    kn = (base * kvn_ref[...]).astype(bf16)
    qn = (base * an_ref[...]).astype(bf16)
    v = _dot(kn, wv_ref[...])
    for bi in range(h.shape[0] // MOBA_BLOCK):
        v_out[bi] = v[bi * MOBA_BLOCK:(bi + 1) * MOBA_BLOCK].T.astype(bf16)
    k = _dot(kn, wk_ref[...])
    q = _dot(qn, wq_ref[...])
    cos_t, up_t, dn_t = cos_ref[...], up_ref[...], dn_ref[...]
    tm, d = h.shape
    sub = km_out.shape[0] // (tm // MOBA_BLOCK)
    for hh in range(d // HEAD_DIM):
        cs = slice(hh * HEAD_DIM, (hh + 1) * HEAD_DIM)
        q_out[:, cs] = _rope(q[:, cs], cos_t, up_t, dn_t)
        kr = _rope(k[:, cs], cos_t, up_t, dn_t)
        k_out[:, cs] = kr.astype(bf16)
        for bi in range(tm // MOBA_BLOCK):
            m = jnp.mean(kr[bi * MOBA_BLOCK:(bi + 1) * MOBA_BLOCK], axis=0, keepdims=True)
            km_out[bi * sub:(bi + 1) * sub, cs] = jnp.broadcast_to(m, (sub, HEAD_DIM))


def _layer_rows_spec(p_all, layer, tm):
    return pl.BlockSpec((None, tm, p_all.shape[2]), lambda i: (layer, i, 0))


def _ple_qkv_layer(h, p_all, layer, ple_norm, ple_gate, ple_proj, kv_norm, w_k, w_v, attn_norm, w_q, seq):
    t, d = h.shape
    pd = p_all.shape[2]
    tm = ROW_TILE
    sub = 8
    cos_t, up_t, dn_t = _rope_tables(seq)
    tiles_per_seq = seq // tm
    row = lambda i: (i, 0)
    tab = pl.BlockSpec((tm, HEAD_DIM), lambda i: (i % tiles_per_seq, 0))
    nkm = t // MOBA_BLOCK * sub
    outs = pl.pallas_call(
        _ple_qkv_kernel,
        grid=(t // tm,),
        in_specs=[
            pl.BlockSpec((tm, d), row), _layer_rows_spec(p_all, layer, tm),
            _const_spec((1, d)), _const_spec((d, d)), _const_spec((pd, d)),
            _const_spec((1, d)), _const_spec((d, d)), _const_spec((d, d)),
            _const_spec((1, d)), _const_spec((d, d)),
            tab, tab, tab,
        ],
        out_specs=[
            pl.BlockSpec((tm, d), row), pl.BlockSpec((tm, d), row),
            pl.BlockSpec((tm, d), row),
            pl.BlockSpec((tm // MOBA_BLOCK, d, MOBA_BLOCK), lambda i: (i, 0, 0)),
            pl.BlockSpec((tm // MOBA_BLOCK * sub, d), row),
        ],
        out_shape=[
            jax.ShapeDtypeStruct((t, d), f32), jax.ShapeDtypeStruct((t, d), f32),
            jax.ShapeDtypeStruct((t, d), bf16),
            jax.ShapeDtypeStruct((t // MOBA_BLOCK, d, MOBA_BLOCK), bf16),
            jax.ShapeDtypeStruct((nkm, d), f32),
        ],
        compiler_params=_params("parallel"),
        name="ple_qkv",
    )(h, p_all, ple_norm.reshape(1, d), ple_gate.astype(bf16), ple_proj.astype(bf16),
      kv_norm.reshape(1, d), w_k.astype(bf16), w_v.astype(bf16),
      attn_norm.reshape(1, d), w_q.astype(bf16), cos_t, up_t, dn_t)
    h3, q, k, v, km = outs
    return h3, q, k, v, km.reshape(t // MOBA_BLOCK, sub, d)[:, 0, :]


def _attn_kernel(q_ref, k_ref, vt_ref, km_ref, o_ref, qa_ref, sa_ref, sb_ref, pa_ref, pb_ref, acc_ref, *, hg):
    j = pl.program_id(2)
    nb = km_ref.shape[0]
    bs = q_ref.shape[0]
    hd = HEAD_DIM
    qscale = (hd ** -0.5) * LOG2E
    blk = lax.broadcasted_iota(jnp.int32, (nb, bs), 0).astype(f32)
    jf = j.astype(f32)
    krow = lax.broadcasted_iota(jnp.int32, (bs, bs), 0)
    qcol = lax.broadcasted_iota(jnp.int32, (bs, bs), 1)
    start_j = pl.multiple_of(j * bs, bs)

    carry0 = []
    for h in range(hg):
        cs = slice(h * hd, (h + 1) * hd)
        qf = q_ref[:, cs]
        gate = lax.dot_general(km_ref[:, cs], qf, (((1,), (1,)), ((), ())),
                               precision=lax.Precision.HIGHEST, preferred_element_type=f32)
        cand = jnp.where(blk < jf, gate, NEG_INF)
        sel = blk < 0.0
        for _ in range(min(MOBA_TOPK, nb)):
            mx = jnp.max(cand, axis=0, keepdims=True)
            pick = jnp.min(jnp.where(cand == mx, blk, float(nb)), axis=0, keepdims=True)
            hit = blk == pick
            sel = sel | hit
            cand = jnp.where(hit, REMOVED, cand)
        bias = jnp.where(sel & (blk < jf), 0.0, NEG_INF)
        if nb < hd:
            bias = jnp.concatenate([bias, jnp.zeros((hd - nb, bs), f32)], axis=0)
        qbt = (qf * qscale).T.astype(bf16)
        qa_ref[h] = jnp.concatenate([qbt, bias.astype(bf16)], axis=0)

        s = _dot(k_ref[pl.ds(start_j, bs), cs], qbt)
        s = jnp.where(krow <= qcol, s, NEG_INF)
        m0 = jnp.max(s, axis=0, keepdims=True)
        p = jnp.exp2(s - m0)
        l0 = jnp.sum(p, axis=0, keepdims=True)
        acc0 = _dot(vt_ref[j, cs, :], p.astype(bf16))
        carry0 += [m0, l0, acc0]

    lane = lax.broadcasted_iota(jnp.int32, (bs, hd), 1)

    def block_scores(n, h):
        start = pl.multiple_of(n * bs, bs)
        onehot = (lane == n).astype(bf16)
        ka = jnp.concatenate([k_ref[pl.ds(start, bs), h * hd:(h + 1) * hd], onehot], axis=1)
        return _dot(ka, qa_ref[h])

    ones_rows = jnp.ones((SUM_ROWS, bs), bf16)

    def weighted_values(n, h, pb):
        return _dot(jnp.concatenate([vt_ref[n, h * hd:(h + 1) * hd, :], ones_rows], axis=0), pb)

    def stage(n, s_in, s_out, p_prev, p_out, stats):
        nn = jnp.minimum(n + 1, nb - 1)
        prev = jnp.clip(n - 1, 0, nb - 1)
        new = []
        for h in range(hg):
            m, alpha_p = stats[2 * h:2 * h + 2]
            acc_ref[h] = alpha_p * acc_ref[h] + weighted_values(prev, h, p_prev[h])
            s = s_in[h]
            m_new = jnp.maximum(m, jnp.max(s, axis=0, keepdims=True))
            p_out[h] = jnp.exp2((s - m_new).astype(bf16))
            s_out[h] = block_scores(nn, h)
            new += [m_new, jnp.exp2(m - m_new)]
        return new

    per_trip = ATTN_BLOCKS_PER_TRIP

    def body(i, stats):
        for u in range(0, per_trip, 2):
            stats = stage(per_trip * i + u, sa_ref, sb_ref, pb_ref, pa_ref, stats)
            stats = stage(per_trip * i + u + 1, sb_ref, sa_ref, pa_ref, pb_ref, stats)
        return tuple(stats)

    init = []
    for h in range(hg):
        m0, l0, acc0 = carry0[3 * h:3 * h + 3]
        acc_ref[h] = jnp.concatenate([acc0, jnp.broadcast_to(l0, (SUM_ROWS, bs))], axis=0)
        sa_ref[h] = block_scores(0, h)
        pb_ref[h] = jnp.zeros((bs, bs), bf16)
        init += [m0, jnp.ones_like(m0)]
    trips = (j + per_trip - 1) // per_trip
    res = lax.fori_loop(0, trips, body, tuple(init))
    last = jnp.clip(per_trip * trips - 1, 0, nb - 1)
    for h in range(hg):
        alpha_p = res[2 * h + 1]
        acc = alpha_p * acc_ref[h] + weighted_values(last, h, pb_ref[h])
        o_ref[:, h * hd:(h + 1) * hd] = (acc[:hd] / acc[hd:hd + 1]).T.astype(o_ref.dtype)


def _attention(q, k, vt, km, batch, seq):
    t, d = q.shape
    nh = d // HEAD_DIM
    nb = seq // MOBA_BLOCK
    hg = ATTN_HEADS_PER_STEP
    assert nb <= HEAD_DIM and nh % hg == 0
    w = hg * HEAD_DIM
    return pl.pallas_call(
        functools.partial(_attn_kernel, hg=hg),
        grid=(batch, nh // hg, nb),
        in_specs=[
            pl.BlockSpec((MOBA_BLOCK, w), lambda b, g, j: (b * nb + j, g)),
            pl.BlockSpec((seq, w), lambda b, g, j: (b, g), pipeline_mode=pl.Buffered(1)),
            pl.BlockSpec((nb, w, MOBA_BLOCK), lambda b, g, j: (b, g, 0), pipeline_mode=pl.Buffered(1)),
            pl.BlockSpec((nb, w), lambda b, g, j: (b, g)),
        ],
        out_specs=pl.BlockSpec((MOBA_BLOCK, w), lambda b, g, j: (b * nb + j, g)),
        out_shape=jax.ShapeDtypeStruct((t, d), bf16),
        scratch_shapes=[
            pltpu.VMEM((hg, 2 * HEAD_DIM, MOBA_BLOCK), bf16),
            pltpu.VMEM((hg, MOBA_BLOCK, MOBA_BLOCK), f32), pltpu.VMEM((hg, MOBA_BLOCK, MOBA_BLOCK), f32),
            pltpu.VMEM((hg, MOBA_BLOCK, MOBA_BLOCK), bf16), pltpu.VMEM((hg, MOBA_BLOCK, MOBA_BLOCK), bf16),
            pltpu.VMEM((hg, HEAD_DIM + SUM_ROWS, MOBA_BLOCK), f32),
        ],
        compiler_params=_params("parallel", "parallel", "arbitrary"),
        name="attn",
    )(q, k, vt, km)


def _oproj_router_kernel(h_ref, a_ref, wo_ref, g_ref, r_ref, h_out, hn_out, route_out, route_t_out, *, n_exp):
    h4 = h_ref[...] + _dot(a_ref[...], wo_ref[...])
    h_out[...] = h4
    hn = _rms(h4, g_ref[...])
    hn_out[...] = hn
    logits = _dot_split(hn, r_ref[...])
    lane = lax.broadcasted_iota(jnp.int32, logits.shape, 1).astype(f32)
    cand = jnp.where(lane < n_exp, logits, NEG_INF)
    m1 = jnp.max(cand, axis=1, keepdims=True)
    i1 = jnp.min(jnp.where(cand == m1, lane, float(LANES)), axis=1, keepdims=True)
    cand = jnp.where(lane == i1, REMOVED, cand)
    m2 = jnp.max(cand, axis=1, keepdims=True)
    i2 = jnp.min(jnp.where(cand == m2, lane, float(LANES)), axis=1, keepdims=True)
    e2 = jnp.exp(m2 - m1)
    den = 1.0 + e2
    route = jnp.where(lane == 0, i1, jnp.where(lane == 1, i2, jnp.where(
        lane == 2, 1.0 / den, jnp.where(lane == 3, e2 / den, 0.0))))
    route_out[...] = route
    route_t_out[...] = route.T[:SUBLANES]


def _oproj_router_layer(h, attn, w_o, norm, router):
    t, d = h.shape
    tm = ROW_TILE
    n_exp = router.shape[1]
    r_pad = jnp.zeros((d, LANES), f32).at[:, :n_exp].set(router)
    row = lambda i: (i, 0)
    return pl.pallas_call(
        functools.partial(_oproj_router_kernel, n_exp=n_exp),
        grid=(t // tm,),
        in_specs=[pl.BlockSpec((tm, d), row), pl.BlockSpec((tm, d), row), _const_spec((d, d)),
                  _const_spec((1, d)), _const_spec((d, LANES))],
        out_specs=[pl.BlockSpec((tm, d), row), pl.BlockSpec((tm, d), row), pl.BlockSpec((tm, LANES), row),
                   pl.BlockSpec((SUBLANES, tm), lambda i: (0, i))],
        out_shape=[jax.ShapeDtypeStruct((t, d), f32), jax.ShapeDtypeStruct((t, d), f32),
                   jax.ShapeDtypeStruct((t, LANES), f32), jax.ShapeDtypeStruct((SUBLANES, t), f32)],
        compiler_params=_params("parallel"),
        name="oproj_router",
    )(h, attn, w_o.astype(bf16), norm.reshape(1, d), r_pad)


def _sc_gather(x, idx):
    n = idx.shape[0]
    assert x.shape[1] == SC_ROW and n % SC_WINDOW == 0
    mesh = plsc.VectorSubcoreMesh(core_axis_name="core", subcore_axis_name="subcore")

    @pl.kernel(out_type=jax.ShapeDtypeStruct((n, SC_ROW), x.dtype), mesh=mesh, scratch_types=[])
    def gather(x_hbm, i_hbm, o_hbm):
        def body(i_vmem, o_vmem):
            pltpu.sync_copy(x_hbm.at[i_vmem.at[0]], o_vmem)

        pltpu.emit_pipeline(
            body,
            grid=(n // SC_WINDOW,),
            in_specs=[pl.BlockSpec((1, SC_WINDOW), index_map=lambda i: (0, i))],
            out_specs=[pl.BlockSpec((SC_WINDOW, SC_ROW), index_map=lambda i: (i, 0))],
            core_axis_name=("core", "subcore"),
            dimension_semantics=(pltpu.PARALLEL,),
        )(i_hbm, o_hbm)

    return gather(x, idx.reshape(n // LANES, LANES).reshape(1, n))


def _pieces(a):
    n, d = a.shape
    return a.reshape(n // SUBLANES, SUBLANES, d // SC_ROW, SC_ROW).transpose(0, 2, 1, 3).reshape(-1, SC_ROW)


def _unpieces(p, d):
    per = d // SC_ROW
    n = p.shape[0] // per
    return p.reshape(n // SUBLANES, per, SUBLANES, SC_ROW).transpose(0, 2, 1, 3).reshape(n, d)


def _gather_tokens(x, rows):
    n, d = x.shape
    m = rows[0].shape[0]
    per = d // SC_ROW
    assert n % SUBLANES == 0 and m % SUBLANES == 0
    tiled = []
    for r in rows:
        first = (r // SUBLANES * (SUBLANES * per) + r % SUBLANES).reshape(m // SUBLANES, SUBLANES)
        tiled.append(jnp.tile(first, (1, per)))
    chunk = jnp.arange(per * SUBLANES, dtype=jnp.int32) // SUBLANES * SUBLANES
    idx = jnp.concatenate(tiled, axis=1) + jnp.tile(chunk, len(rows))[None, :]
    return _unpieces(_sc_gather(_pieces(x), idx.reshape(-1)), len(rows) * d)


def _expert_kernel(te_ref, tv_ref, x_ref, w1_ref, w3_ref, w2_ref, o_ref, xb_ref, *, nf):
    i = pl.program_id(0)
    f = pl.program_id(1)
    valid = tv_ref[i] > 0

    @pl.when(f == 0)
    def _():
        xb_ref[...] = x_ref[...].astype(bf16)
        o_ref[...] = jnp.zeros_like(o_ref)

    @pl.when(valid)
    def _():
        x = xb_ref[...]
        a = _dot(x, w1_ref[...])
        b = _dot(x, w3_ref[...])
        o_ref[...] += _dot((jax.nn.silu(a) * b).astype(bf16), w2_ref[...])


def _route_tables(choices, n_exp, tm):
    t = choices[0].shape[0]
    nslots = len(choices) * t
    experts = jnp.arange(n_exp, dtype=jnp.int32)[:, None]
    onehots = [(c[None, :] == experts).astype(jnp.int32) for c in choices]
    cums = [jnp.cumsum(oh, axis=1) for oh in onehots]
    totals = [cu[:, -1] for cu in cums]
    counts = sum(totals)
    tiles_e = (counts + tm - 1) // tm
    tile_end = jnp.cumsum(tiles_e)
    tile_start = tile_end - tiles_e
    group_start = jnp.cumsum(counts) - counts
    nt = nslots // tm + n_exp
    total = tile_end[-1]
    ti = jnp.arange(nt, dtype=jnp.int32)
    tv = (ti < total).astype(jnp.int32)
    tc = jnp.minimum(ti, total - 1)
    te = jnp.minimum(jnp.sum((tc[:, None] >= tile_end[None, :]).astype(jnp.int32), axis=1), n_exp - 1)
    order = jnp.argsort(jnp.concatenate(choices), stable=True).astype(jnp.int32)
    rank = ((tc - tile_start[te]) * tm)[:, None] + jnp.arange(tm, dtype=jnp.int32)[None, :]
    real = (rank < counts[te][:, None]) & (tv[:, None] > 0)
    sidx = jnp.clip(group_start[te][:, None] + rank, 0, nslots - 1)
    src = jnp.where(real, order[sidx] % t, 0).reshape(-1)
    pos = []
    earlier = jnp.zeros((n_exp,), jnp.int32)
    for oh, cu, tot in zip(onehots, cums, totals):
        pos.append(jnp.sum(oh * ((tile_start * tm + earlier)[:, None] + cu - oh), axis=0))
        earlier = earlier + tot
    return te, tv, src, pos, nt


def _experts_layer(hn, choices, w1, w3, w2):
    t, d = hn.shape
    n_exp, _, ff = w1.shape
    tm = ROW_TILE
    fc = _ff_chunk(ff, EXPERT_FF_CHUNK)
    nf = ff // fc
    te, tv, src, pos, nt = _route_tables(choices, n_exp, tm)
    xs = _gather_tokens(hn, [src])
    w1r, w3r, w2r = w1.astype(bf16), w3.astype(bf16), w2.astype(bf16)

    def fsel(i, f, tv_r):
        return jnp.where(tv_r[i] > 0, f, nf - 1)

    def wmap_in(i, f, te_r, tv_r):
        return (te_r[i], 0, fsel(i, f, tv_r))

    def wmap_out(i, f, te_r, tv_r):
        return (te_r[i], fsel(i, f, tv_r), 0)

    grid_spec = pltpu.PrefetchScalarGridSpec(
        num_scalar_prefetch=2,
        grid=(nt, nf),
        in_specs=[
            pl.BlockSpec((tm, d), lambda i, f, *_: (i, 0)),
            pl.BlockSpec((None, d, fc), wmap_in),
            pl.BlockSpec((None, d, fc), wmap_in),
            pl.BlockSpec((None, fc, d), wmap_out),
        ],
        out_specs=pl.BlockSpec((tm, d), lambda i, f, *_: (i, 0)),
        scratch_shapes=[pltpu.VMEM((tm, d), bf16)],
    )
    ys = pl.pallas_call(
        functools.partial(_expert_kernel, nf=nf),
        grid_spec=grid_spec,
        out_shape=jax.ShapeDtypeStruct((nt * tm, d), f32),
        compiler_params=_params("parallel", "arbitrary"),
        name="experts",
    )(te, tv, xs, w1r, w3r, w2r)
    return _gather_tokens(ys, pos)


def _final_kernel(h_ref, y_ref, route_ref, p_ref, pn_ref, wg_ref, wp_ref, fn_ref, o_ref):
    d = h_ref.shape[1]
    route = route_ref[...]
    h5 = h_ref[...] + route[:, 2:3] * y_ref[:, :d] + route[:, 3:4] * y_ref[:, d:]
    hn = _rms(h5, pn_ref[...]).astype(bf16)
    h6 = h5 + jax.nn.sigmoid(_dot(hn, wg_ref[...])) * _dot(p_ref[...].astype(bf16), wp_ref[...])
    o_ref[...] = _rms(h6, fn_ref[...])


def _final_layer(h, y2, route, p_all, layer, ple_norm, ple_gate, ple_proj, final_norm):
    t, d = h.shape
    pd = p_all.shape[2]
    tm = ROW_TILE
    row = lambda i: (i, 0)
    return pl.pallas_call(
        _final_kernel,
        grid=(t // tm,),
        in_specs=[pl.BlockSpec((tm, d), row), pl.BlockSpec((tm, TOP_K_EXPERTS * d), row),
                  pl.BlockSpec((tm, LANES), row), _layer_rows_spec(p_all, layer, tm),
                  _const_spec((1, d)), _const_spec((d, d)), _const_spec((pd, d)), _const_spec((1, d))],
        out_specs=pl.BlockSpec((tm, d), row),
        out_shape=jax.ShapeDtypeStruct((t, d), f32),
        compiler_params=_params("parallel"),
        name="final",
    )(h, y2, route, p_all, ple_norm.reshape(1, d), ple_gate.astype(bf16), ple_proj.astype(bf16),
      final_norm.reshape(1, d))


def kernel(x, p, pool_norm, pool_w, pool_scale, kv_norm, w_k, w_v, attn_norm, w_q, w_o, ffn_norm, ffn_w1, ffn_w3, ffn_w2, router, exp_w1, exp_w3, exp_w2, ple_norm, ple_gate, ple_proj, final_norm):
    batch, seq, d = x.shape
    t = batch * seq
    assert seq % ROW_TILE == 0 and ROW_TILE % MOBA_BLOCK == 0 and d % HEAD_DIM == 0
    assert p.shape[0] == 2 and router.shape[2] <= LANES
    h = x.reshape(t, d)
    pf = p.reshape(p.shape[0], t, p.shape[-1])

    h = _pool_layer(h, pool_norm[0], pool_w[0], pool_scale[0], seq)
    h = _swiglu_layer(h, ffn_norm[0], ffn_w1[0], ffn_w3[0], ffn_w2[0])
    h, q, k, v, km = _ple_qkv_layer(h, pf, 0, ple_norm[0], ple_gate[0], ple_proj[0],
                                    kv_norm, w_k, w_v, attn_norm[0], w_q[0], seq)
    attn = _attention(q, k, v, km, batch, seq)
    h, hn, route, route_t = _oproj_router_layer(h, attn, w_o[0], ffn_norm[1], router[0])
    choices = [route_t[k].astype(jnp.int32) for k in range(TOP_K_EXPERTS)]
    y = _experts_layer(hn, choices, exp_w1[0], exp_w3[0], exp_w2[0])
    out = _final_layer(h, y, route, pf, 1, ple_norm[1], ple_gate[1],
                       ple_proj[1], final_norm)
    return out.reshape(batch, seq, d)
```

```python
import functools

import jax
import jax.numpy as jnp
from jax import lax
from jax.experimental import pallas as pl
from jax.experimental.pallas import tpu as pltpu
from jax.experimental.pallas import tpu_sc as plsc

POOL_WINDOWS = (2, 4, 8, 16)
HEAD_DIM = 128
MOBA_BLOCK = 256
MOBA_TOPK = 3
ROPE_THETA = 500000.0
ROPE_DIM = HEAD_DIM // 4
TOP_K_EXPERTS = 2
RMS_EPS = 1e-6
NEG_INF = -1e30
REMOVED = -3e38

LANES = 128
SUBLANES = 8
ROW_TILE = 512
POOL_SUB = 128
ATTN_HEADS_PER_STEP = 4
ATTN_BLOCKS_PER_TRIP = 4
SUM_ROWS = 16
LOG2E = 1.4426950408889634
EXPERT_FF_CHUNK = 1792
SC_ROW = 128
SC_WINDOW = 128
VMEM_LIMIT = 56 * 1024 * 1024

bf16 = jnp.bfloat16
f32 = jnp.float32


def _dot(a, b):
    return jnp.dot(a, b, preferred_element_type=f32)


def _dot_split(a, b):
    a_hi = a.astype(bf16)
    a_lo = (a - a_hi.astype(f32)).astype(bf16)
    b_hi = b.astype(bf16)
    b_lo = (b - b_hi.astype(f32)).astype(bf16)
    return _dot(a_hi, b_hi) + (_dot(a_lo, b_hi) + _dot(a_hi, b_lo))


def _pack_pairs(x):
    half = x.shape[1] // 2
    lo = lax.bitcast_convert_type(x[:, :half].astype(bf16).astype(f32), jnp.uint32)
    hi = lax.bitcast_convert_type(x[:, half:].astype(bf16).astype(f32), jnp.uint32)
    return (hi & jnp.uint32(0xFFFF0000)) | (lo >> 16)


def _unpack_pairs(w):
    lo = lax.bitcast_convert_type(w << 16, f32)
    hi = lax.bitcast_convert_type(w & jnp.uint32(0xFFFF0000), f32)
    return jnp.concatenate([lo, hi], axis=1)


def _rms(x, g):
    var = jnp.mean(x * x, axis=-1, keepdims=True)
    return x * lax.rsqrt(var + RMS_EPS) * g


def _params(*sem):
    return pltpu.CompilerParams(dimension_semantics=sem, vmem_limit_bytes=VMEM_LIMIT)


def _const_spec(shape):
    nd = len(shape)
    return pl.BlockSpec(shape, lambda *_: (0,) * nd)


def _pool_kernel(x_ref, halo_ref, g_ref, pw_ref, ps_ref, o_ref, pooled_ref, *, ts, seq):
    i = pl.program_id(0)
    g = g_ref[...]
    x = x_ref[...]
    xn = _rms(x, g)
    keep = jnp.where((i * ts) % seq == 0, 0.0, 1.0)
    hnb = (_rms(halo_ref[...], g) * keep).astype(bf16)
    xnb = xn.astype(bf16)
    gd = x.shape[1] // len(POOL_WINDOWS)
    r = lax.broadcasted_iota(jnp.int32, (POOL_SUB, 2 * POOL_SUB), 0)
    c = lax.broadcasted_iota(jnp.int32, (POOL_SUB, 2 * POOL_SUB), 1)
    dist = r + POOL_SUB - c
    bands = [((dist >= 0) & (dist < w)).astype(bf16) for w in POOL_WINDOWS]
    rows = lax.broadcasted_iota(jnp.int32, (POOL_SUB, 1), 0)
    for sb in range(ts // POOL_SUB):
        lo, hi = sb * POOL_SUB, (sb + 1) * POOL_SUB
        prev = hnb if sb == 0 else xnb[lo - POOL_SUB:lo]
        ext = jnp.concatenate([prev, xnb[lo:hi]], axis=0)
        tpos = (i * ts + lo) % seq + rows
        for gi, w in enumerate(POOL_WINDOWS):
            cs = slice(gi * gd, (gi + 1) * gd)
            wsum = _dot(bands[gi], ext[:, cs])
            cnt = jnp.minimum(tpos + 1, w).astype(f32)
            pooled_ref[lo:hi, cs] = (wsum / cnt - xn[lo:hi, cs]).astype(bf16)
    for gi in range(len(POOL_WINDOWS)):
        cs = slice(gi * gd, (gi + 1) * gd)
        mixed = _dot(pooled_ref[:, cs], pw_ref[gi])
        o_ref[:, cs] = x[:, cs] + mixed * ps_ref[:, cs]


def _pool_layer(h, norm, pool_w, pool_scale, seq):
    t, d = h.shape
    ts = ROW_TILE
    ng = len(POOL_WINDOWS)
    gd = d // ng
    per = ts // POOL_SUB
    return pl.pallas_call(
        functools.partial(_pool_kernel, ts=ts, seq=seq),
        grid=(t // ts,),
        in_specs=[
            pl.BlockSpec((ts, d), lambda i: (i, 0)),
            pl.BlockSpec((POOL_SUB, d), lambda i: (jnp.maximum(i * per - 1, 0), 0)),
            _const_spec((1, d)),
            _const_spec((ng, gd, gd)),
            _const_spec((1, d)),
        ],
        out_specs=pl.BlockSpec((ts, d), lambda i: (i, 0)),
        out_shape=jax.ShapeDtypeStruct((t, d), f32),
        scratch_shapes=[pltpu.VMEM((ts, d), bf16)],
        compiler_params=_params("parallel"),
        name="pool",
    )(h, h, norm.reshape(1, d), pool_w.astype(bf16), pool_scale.reshape(1, d))


def _swiglu_kernel(h_ref, g_ref, w1_ref, w3_ref, w2_ref, o_ref):
    x = h_ref[...]
    hn = _rms(x, g_ref[...]).astype(bf16)
    a = _dot(hn, w1_ref[...])
    b = _dot(hn, w3_ref[...])
    o_ref[...] = x + _dot((jax.nn.silu(a) * b).astype(bf16), w2_ref[...])


def _ff_chunk(ff, target):
    units = ff // LANES
    best = 1
    for k in range(1, units + 1):
        if units % k == 0 and k * LANES <= target:
            best = k
    return best * LANES


def _swiglu_layer(h, norm, w1, w3, w2):
    t, d = h.shape
    ff = w1.shape[1]
    tm = ROW_TILE
    w1r, w3r, w2r = w1.astype(bf16), w3.astype(bf16), w2.astype(bf16)
    resident = lambda shape: pl.BlockSpec(shape, lambda i: (0, 0), pipeline_mode=pl.Buffered(1))
    return pl.pallas_call(
        _swiglu_kernel,
        grid=(t // tm,),
        in_specs=[
            pl.BlockSpec((tm, d), lambda i: (i, 0)),
            _const_spec((1, d)),
            resident((d, ff)), resident((d, ff)), resident((ff, d)),
        ],
        out_specs=pl.BlockSpec((tm, d), lambda i: (i, 0)),
        out_shape=jax.ShapeDtypeStruct((t, d), f32),
        compiler_params=_params("parallel"),
        name="swiglu",
    )(h, norm.reshape(1, d), w1r, w3r, w2r)


def _rope_tables(seq):
    half = ROPE_DIM // 2
    inv_freq = jnp.float32(ROPE_THETA) ** (-(jnp.arange(0, ROPE_DIM, 2, dtype=f32) / ROPE_DIM))
    ang = jnp.arange(seq, dtype=f32)[:, None] * inv_freq[None, :]
    cos, sin = jnp.cos(ang), jnp.sin(ang)
    ones = jnp.ones((seq, HEAD_DIM - ROPE_DIM), f32)
    zeros = jnp.zeros((seq, HEAD_DIM - half), f32)
    cos_t = jnp.concatenate([cos, cos, ones], axis=1)
    up_t = jnp.concatenate([-sin, zeros], axis=1)
    dn_t = jnp.concatenate([jnp.zeros((seq, half), f32), sin, jnp.zeros((seq, HEAD_DIM - ROPE_DIM), f32)], axis=1)
    return cos_t, up_t, dn_t


def _rope(xh, cos_t, up_t, dn_t):
    half = ROPE_DIM // 2
    return (xh * cos_t + pltpu.roll(xh, HEAD_DIM - half, 1) * up_t + pltpu.roll(xh, half, 1) * dn_t)


def _ple_qkv_kernel(h_ref, p_ref, pn_ref, wg_ref, wp_ref, kvn_ref, wk_ref, wv_ref, an_ref, wq_ref,
                    cos_ref, up_ref, dn_ref, h_out, q_out, k_out, v_out, km_out):
    h = h_ref[...]
    hn = _rms(h, pn_ref[...]).astype(bf16)
    h3 = h + jax.nn.sigmoid(_dot(hn, wg_ref[...])) * _dot(p_ref[...].astype(bf16), wp_ref[...])
    h_out[...] = h3
    base = h3 * lax.rsqrt(jnp.mean(h3 * h3, axis=-1, keepdims=True) + RMS_EPS)
    kn = (base * kvn_ref[...]).astype(bf16)
    qn = (base * an_ref[...]).astype(bf16)
    v = _dot(kn, wv_ref[...])
    for bi in range(h.shape[0] // MOBA_BLOCK):
        v_out[bi] = v[bi * MOBA_BLOCK:(bi + 1) * MOBA_BLOCK].T.astype(bf16)
    k = _dot(kn, wk_ref[...])
    q = _dot(qn, wq_ref[...])
    cos_t, up_t, dn_t = cos_ref[...], up_ref[...], dn_ref[...]
    tm, d = h.shape
    sub = km_out.shape[0] // (tm // MOBA_BLOCK)
    for hh in range(d // HEAD_DIM):
        cs = slice(hh * HEAD_DIM, (hh + 1) * HEAD_DIM)
        q_out[:, cs] = _rope(q[:, cs], cos_t, up_t, dn_t)
        kr = _rope(k[:, cs], cos_t, up_t, dn_t)
        k_out[:, cs] = kr.astype(bf16)
        for bi in range(tm // MOBA_BLOCK):
            m = jnp.mean(kr[bi * MOBA_BLOCK:(bi + 1) * MOBA_BLOCK], axis=0, keepdims=True)
            km_out[bi * sub:(bi + 1) * sub, cs] = jnp.broadcast_to(m, (sub, HEAD_DIM))


def _layer_rows_spec(p_all, layer, tm):
    return pl.BlockSpec((None, tm, p_all.shape[2]), lambda i: (layer, i, 0))


def _ple_qkv_layer(h, p_all, layer, ple_norm, ple_gate, ple_proj, kv_norm, w_k, w_v, attn_norm, w_q, seq):
    t, d = h.shape
    pd = p_all.shape[2]
    tm = ROW_TILE
    sub = 8
    cos_t, up_t, dn_t = _rope_tables(seq)
    tiles_per_seq = seq // tm
    row = lambda i: (i, 0)
    tab = pl.BlockSpec((tm, HEAD_DIM), lambda i: (i % tiles_per_seq, 0))
    nkm = t // MOBA_BLOCK * sub
    outs = pl.pallas_call(
        _ple_qkv_kernel,
        grid=(t // tm,),
        in_specs=[
            pl.BlockSpec((tm, d), row), _layer_rows_spec(p_all, layer, tm),
            _const_spec((1, d)), _const_spec((d, d)), _const_spec((pd, d)),
            _const_spec((1, d)), _const_spec((d, d)), _const_spec((d, d)),
            _const_spec((1, d)), _const_spec((d, d)),
            tab, tab, tab,
        ],
        out_specs=[
            pl.BlockSpec((tm, d), row), pl.BlockSpec((tm, d), row),
            pl.BlockSpec((tm, d), row),
            pl.BlockSpec((tm // MOBA_BLOCK, d, MOBA_BLOCK), lambda i: (i, 0, 0)),
            pl.BlockSpec((tm // MOBA_BLOCK * sub, d), row),
        ],
        out_shape=[
            jax.ShapeDtypeStruct((t, d), f32), jax.ShapeDtypeStruct((t, d), f32),
            jax.ShapeDtypeStruct((t, d), bf16),
            jax.ShapeDtypeStruct((t // MOBA_BLOCK, d, MOBA_BLOCK), bf16),
            jax.ShapeDtypeStruct((nkm, d), f32),
        ],
        compiler_params=_params("parallel"),
        name="ple_qkv",
    )(h, p_all, ple_norm.reshape(1, d), ple_gate.astype(bf16), ple_proj.astype(bf16),
      kv_norm.reshape(1, d), w_k.astype(bf16), w_v.astype(bf16),
      attn_norm.reshape(1, d), w_q.astype(bf16), cos_t, up_t, dn_t)
    h3, q, k, v, km = outs
    return h3, q, k, v, km.reshape(t // MOBA_BLOCK, sub, d)[:, 0, :]


def _attn_kernel(q_ref, k_ref, vt_ref, km_ref, o_ref, qa_ref, sa_ref, sb_ref, pa_ref, pb_ref, acc_ref, *, hg):
    j = pl.program_id(2)
    nb = km_ref.shape[0]
    bs = q_ref.shape[0]
    hd = HEAD_DIM
    qscale = (hd ** -0.5) * LOG2E
    blk = lax.broadcasted_iota(jnp.int32, (nb, bs), 0).astype(f32)
    jf = j.astype(f32)
    krow = lax.broadcasted_iota(jnp.int32, (bs, bs), 0)
    qcol = lax.broadcasted_iota(jnp.int32, (bs, bs), 1)
    start_j = pl.multiple_of(j * bs, bs)

    carry0 = []
    for h in range(hg):
        cs = slice(h * hd, (h + 1) * hd)
        qf = q_ref[:, cs]
        gate = lax.dot_general(km_ref[:, cs], qf, (((1,), (1,)), ((), ())),
                               precision=lax.Precision.HIGHEST, preferred_element_type=f32)
        cand = jnp.where(blk < jf, gate, NEG_INF)
        sel = blk < 0.0
        for _ in range(min(MOBA_TOPK, nb)):
            mx = jnp.max(cand, axis=0, keepdims=True)
            pick = jnp.min(jnp.where(cand == mx, blk, float(nb)), axis=0, keepdims=True)
            hit = blk == pick
            sel = sel | hit
            cand = jnp.where(hit, REMOVED, cand)
        bias = jnp.where(sel & (blk < jf), 0.0, NEG_INF)
        if nb < hd:
            bias = jnp.concatenate([bias, jnp.zeros((hd - nb, bs), f32)], axis=0)
        qbt = (qf * qscale).T.astype(bf16)
        qa_ref[h] = jnp.concatenate([qbt, bias.astype(bf16)], axis=0)

        s = _dot(k_ref[pl.ds(start_j, bs), cs], qbt)
        s = jnp.where(krow <= qcol, s, NEG_INF)
        m0 = jnp.max(s, axis=0, keepdims=True)
        p = jnp.exp2(s - m0)
        l0 = jnp.sum(p, axis=0, keepdims=True)
        acc0 = _dot(vt_ref[j, cs, :], p.astype(bf16))
        carry0 += [m0, l0, acc0]

    lane = lax.broadcasted_iota(jnp.int32, (bs, hd), 1)

    def block_scores(n, h):
        start = pl.multiple_of(n * bs, bs)
        onehot = (lane == n).astype(bf16)
        ka = jnp.concatenate([k_ref[pl.ds(start, bs), h * hd:(h + 1) * hd], onehot], axis=1)
        return _dot(ka, qa_ref[h])

    ones_rows = jnp.ones((SUM_ROWS, bs), bf16)

    def weighted_values(n, h, pb):
        return _dot(jnp.concatenate([vt_ref[n, h * hd:(h + 1) * hd, :], ones_rows], axis=0), pb)

    def stage(n, s_in, s_out, p_prev, p_out, stats):
        nn = jnp.minimum(n + 1, nb - 1)
        prev = jnp.clip(n - 1, 0, nb - 1)
        new = []
        for h in range(hg):
            m, alpha_p = stats[2 * h:2 * h + 2]
            acc_ref[h] = alpha_p * acc_ref[h] + weighted_values(prev, h, p_prev[h])
            s = s_in[h]
            m_new = jnp.maximum(m, jnp.max(s, axis=0, keepdims=True))
            p_out[h] = jnp.exp2((s - m_new).astype(bf16))
            s_out[h] = block_scores(nn, h)
            new += [m_new, jnp.exp2(m - m_new)]
        return new

    per_trip = ATTN_BLOCKS_PER_TRIP

    def body(i, stats):
        for u in range(0, per_trip, 2):
            stats = stage(per_trip * i + u, sa_ref, sb_ref, pb_ref, pa_ref, stats)
            stats = stage(per_trip * i + u + 1, sb_ref, sa_ref, pa_ref, pb_ref, stats)
        return tuple(stats)

    init = []
    for h in range(hg):
        m0, l0, acc0 = carry0[3 * h:3 * h + 3]
        acc_ref[h] = jnp.concatenate([acc0, jnp.broadcast_to(l0, (SUM_ROWS, bs))], axis=0)
        sa_ref[h] = block_scores(0, h)
        pb_ref[h] = jnp.zeros((bs, bs), bf16)
        init += [m0, jnp.ones_like(m0)]
    trips = (j + per_trip - 1) // per_trip
    res = lax.fori_loop(0, trips, body, tuple(init))
    last = jnp.clip(per_trip * trips - 1, 0, nb - 1)
    for h in range(hg):
        alpha_p = res[2 * h + 1]
        acc = alpha_p * acc_ref[h] + weighted_values(last, h, pb_ref[h])
        o_ref[:, h * hd:(h + 1) * hd] = (acc[:hd] / acc[hd:hd + 1]).T.astype(o_ref.dtype)


def _attention(q, k, vt, km, batch, seq):
    t, d = q.shape
    nh = d // HEAD_DIM
    nb = seq // MOBA_BLOCK
    hg = ATTN_HEADS_PER_STEP
    assert nb <= HEAD_DIM and nh % hg == 0
    w = hg * HEAD_DIM
    return pl.pallas_call(
        functools.partial(_attn_kernel, hg=hg),
        grid=(batch, nh // hg, nb),
        in_specs=[
            pl.BlockSpec((MOBA_BLOCK, w), lambda b, g, j: (b * nb + j, g)),
            pl.BlockSpec((seq, w), lambda b, g, j: (b, g), pipeline_mode=pl.Buffered(1)),
            pl.BlockSpec((nb, w, MOBA_BLOCK), lambda b, g, j: (b, g, 0), pipeline_mode=pl.Buffered(1)),
            pl.BlockSpec((nb, w), lambda b, g, j: (b, g)),
        ],
        out_specs=pl.BlockSpec((MOBA_BLOCK, w), lambda b, g, j: (b * nb + j, g)),
        out_shape=jax.ShapeDtypeStruct((t, d), bf16),
        scratch_shapes=[
            pltpu.VMEM((hg, 2 * HEAD_DIM, MOBA_BLOCK), bf16),
            pltpu.VMEM((hg, MOBA_BLOCK, MOBA_BLOCK), f32), pltpu.VMEM((hg, MOBA_BLOCK, MOBA_BLOCK), f32),
            pltpu.VMEM((hg, MOBA_BLOCK, MOBA_BLOCK), bf16), pltpu.VMEM((hg, MOBA_BLOCK, MOBA_BLOCK), bf16),
            pltpu.VMEM((hg, HEAD_DIM + SUM_ROWS, MOBA_BLOCK), f32),
        ],
        compiler_params=_params("parallel", "parallel", "arbitrary"),
        name="attn",
    )(q, k, vt, km)


def _oproj_router_kernel(h_ref, a_ref, wo_ref, g_ref, r_ref, h_out, hn_out, route_out, route_t_out, *, n_exp):
    h4 = h_ref[...] + _dot(a_ref[...], wo_ref[...])
    h_out[...] = h4
    hn = _rms(h4, g_ref[...])
    hn_out[...] = _pack_pairs(hn)
    logits = _dot_split(hn, r_ref[...])
    lane = lax.broadcasted_iota(jnp.int32, logits.shape, 1).astype(f32)
    cand = jnp.where(lane < n_exp, logits, NEG_INF)
    m1 = jnp.max(cand, axis=1, keepdims=True)
    i1 = jnp.min(jnp.where(cand == m1, lane, float(LANES)), axis=1, keepdims=True)
    cand = jnp.where(lane == i1, REMOVED, cand)
    m2 = jnp.max(cand, axis=1, keepdims=True)
    i2 = jnp.min(jnp.where(cand == m2, lane, float(LANES)), axis=1, keepdims=True)
    e2 = jnp.exp(m2 - m1)
    den = 1.0 + e2
    route = jnp.where(lane == 0, i1, jnp.where(lane == 1, i2, jnp.where(
        lane == 2, 1.0 / den, jnp.where(lane == 3, e2 / den, 0.0))))
    route_out[...] = route
    route_t_out[...] = route.T[:SUBLANES]


def _oproj_router_layer(h, attn, w_o, norm, router):
    t, d = h.shape
    tm = ROW_TILE
    n_exp = router.shape[1]
    r_pad = jnp.zeros((d, LANES), f32).at[:, :n_exp].set(router)
    row = lambda i: (i, 0)
    return pl.pallas_call(
        functools.partial(_oproj_router_kernel, n_exp=n_exp),
        grid=(t // tm,),
        in_specs=[pl.BlockSpec((tm, d), row), pl.BlockSpec((tm, d), row), _const_spec((d, d)),
                  _const_spec((1, d)), _const_spec((d, LANES))],
        out_specs=[pl.BlockSpec((tm, d), row), pl.BlockSpec((tm, d // 2), row), pl.BlockSpec((tm, LANES), row),
                   pl.BlockSpec((SUBLANES, tm), lambda i: (0, i))],
        out_shape=[jax.ShapeDtypeStruct((t, d), f32), jax.ShapeDtypeStruct((t, d // 2), jnp.uint32),
                   jax.ShapeDtypeStruct((t, LANES), f32), jax.ShapeDtypeStruct((SUBLANES, t), f32)],
        compiler_params=_params("parallel"),
        name="oproj_router",
    )(h, attn, w_o.astype(bf16), norm.reshape(1, d), r_pad)


def _sc_gather(x, idx):
    n = idx.shape[0]
    assert x.shape[1] == SC_ROW and n % SC_WINDOW == 0
    mesh = plsc.VectorSubcoreMesh(core_axis_name="core", subcore_axis_name="subcore")

    @pl.kernel(out_type=jax.ShapeDtypeStruct((n, SC_ROW), x.dtype), mesh=mesh, scratch_types=[])
    def gather(x_hbm, i_hbm, o_hbm):
        def body(i_vmem, o_vmem):
            pltpu.sync_copy(x_hbm.at[i_vmem.at[0]], o_vmem)

        pltpu.emit_pipeline(
            body,
            grid=(n // SC_WINDOW,),
            in_specs=[pl.BlockSpec((1, SC_WINDOW), index_map=lambda i: (0, i))],
            out_specs=[pl.BlockSpec((SC_WINDOW, SC_ROW), index_map=lambda i: (i, 0))],
            core_axis_name=("core", "subcore"),
            dimension_semantics=(pltpu.PARALLEL,),
        )(i_hbm, o_hbm)

    return gather(x, idx.reshape(n // LANES, LANES).reshape(1, n))


def _pieces(a):
    n, d = a.shape
    return a.reshape(n // SUBLANES, SUBLANES, d // SC_ROW, SC_ROW).transpose(0, 2, 1, 3).reshape(-1, SC_ROW)


def _unpieces(p, d):
    per = d // SC_ROW
    n = p.shape[0] // per
    return p.reshape(n // SUBLANES, per, SUBLANES, SC_ROW).transpose(0, 2, 1, 3).reshape(n, d)


def _gather_tokens(x, rows):
    n, d = x.shape
    m = rows[0].shape[0]
    per = d // SC_ROW
    assert n % SUBLANES == 0 and m % SUBLANES == 0
    tiled = []
    for r in rows:
        first = (r // SUBLANES * (SUBLANES * per) + r % SUBLANES).reshape(m // SUBLANES, SUBLANES)
        tiled.append(jnp.tile(first, (1, per)))
    chunk = jnp.arange(per * SUBLANES, dtype=jnp.int32) // SUBLANES * SUBLANES
    idx = jnp.concatenate(tiled, axis=1) + jnp.tile(chunk, len(rows))[None, :]
    return _unpieces(_sc_gather(_pieces(x), idx.reshape(-1)), len(rows) * d)


def _expert_kernel(te_ref, tv_ref, x_ref, w1_ref, w3_ref, w2_ref, o_ref, xb_ref, acc_ref, *, nf):
    i = pl.program_id(0)
    f = pl.program_id(1)
    valid = tv_ref[i] > 0

    @pl.when(f == 0)
    def _():
        xb_ref[...] = _unpack_pairs(x_ref[...]).astype(bf16)
        acc_ref[...] = jnp.zeros_like(acc_ref)

    @pl.when(valid)
    def _():
        x = xb_ref[...]
        a = _dot(x, w1_ref[...])
        b = _dot(x, w3_ref[...])
        acc_ref[...] += _dot((jax.nn.silu(a) * b).astype(bf16), w2_ref[...])

    @pl.when(f == nf - 1)
    def _():
        o_ref[...] = _pack_pairs(acc_ref[...])


def _route_tables(choices, n_exp, tm):
    t = choices[0].shape[0]
    nslots = len(choices) * t
    experts = jnp.arange(n_exp, dtype=jnp.int32)[:, None]
    onehots = [(c[None, :] == experts).astype(jnp.int32) for c in choices]
    cums = [jnp.cumsum(oh, axis=1) for oh in onehots]
    totals = [cu[:, -1] for cu in cums]
    counts = sum(totals)
    tiles_e = (counts + tm - 1) // tm
    tile_end = jnp.cumsum(tiles_e)
    tile_start = tile_end - tiles_e
    group_start = jnp.cumsum(counts) - counts
    nt = nslots // tm + n_exp
    total = tile_end[-1]
    ti = jnp.arange(nt, dtype=jnp.int32)
    tv = (ti < total).astype(jnp.int32)
    tc = jnp.minimum(ti, total - 1)
    te = jnp.minimum(jnp.sum((tc[:, None] >= tile_end[None, :]).astype(jnp.int32), axis=1), n_exp - 1)
    order = jnp.argsort(jnp.concatenate(choices), stable=True).astype(jnp.int32)
    rank = ((tc - tile_start[te]) * tm)[:, None] + jnp.arange(tm, dtype=jnp.int32)[None, :]
    real = (rank < counts[te][:, None]) & (tv[:, None] > 0)
    sidx = jnp.clip(group_start[te][:, None] + rank, 0, nslots - 1)
    src = jnp.where(real, order[sidx] % t, 0).reshape(-1)
    pos = []
    earlier = jnp.zeros((n_exp,), jnp.int32)
    for oh, cu, tot in zip(onehots, cums, totals):
        pos.append(jnp.sum(oh * ((tile_start * tm + earlier)[:, None] + cu - oh), axis=0))
        earlier = earlier + tot
    return te, tv, src, pos, nt


def _experts_layer(hn, choices, w1, w3, w2):
    n_exp, d, ff = w1.shape
    tm = ROW_TILE
    fc = _ff_chunk(ff, EXPERT_FF_CHUNK)
    nf = ff // fc
    te, tv, src, pos, nt = _route_tables(choices, n_exp, tm)
    xs = _gather_tokens(hn, [src])
    w1r, w3r, w2r = w1.astype(bf16), w3.astype(bf16), w2.astype(bf16)

    def fsel(i, f, tv_r):
        return jnp.where(tv_r[i] > 0, f, nf - 1)

    def wmap_in(i, f, te_r, tv_r):
        return (te_r[i], 0, fsel(i, f, tv_r))

    def wmap_out(i, f, te_r, tv_r):
        return (te_r[i], fsel(i, f, tv_r), 0)

    grid_spec = pltpu.PrefetchScalarGridSpec(
        num_scalar_prefetch=2,
        grid=(nt, nf),
        in_specs=[
            pl.BlockSpec((tm, d // 2), lambda i, f, *_: (i, 0)),
            pl.BlockSpec((None, d, fc), wmap_in),
            pl.BlockSpec((None, d, fc), wmap_in),
            pl.BlockSpec((None, fc, d), wmap_out),
        ],
        out_specs=pl.BlockSpec((tm, d // 2), lambda i, f, *_: (i, 0)),
        scratch_shapes=[pltpu.VMEM((tm, d), bf16), pltpu.VMEM((tm, d), f32)],
    )
    ys = pl.pallas_call(
        functools.partial(_expert_kernel, nf=nf),
        grid_spec=grid_spec,
        out_shape=jax.ShapeDtypeStruct((nt * tm, d // 2), jnp.uint32),
        compiler_params=_params("parallel", "arbitrary"),
        name="experts",
    )(te, tv, xs, w1r, w3r, w2r)
    return _gather_tokens(ys, pos)


def _final_kernel(h_ref, y_ref, route_ref, p_ref, pn_ref, wg_ref, wp_ref, fn_ref, o_ref):
    d = h_ref.shape[1]
    route = route_ref[...]
    y = y_ref[...]
    h5 = h_ref[...] + route[:, 2:3] * _unpack_pairs(y[:, :d // 2]) + route[:, 3:4] * _unpack_pairs(y[:, d // 2:])
    hn = _rms(h5, pn_ref[...]).astype(bf16)
    h6 = h5 + jax.nn.sigmoid(_dot(hn, wg_ref[...])) * _dot(p_ref[...].astype(bf16), wp_ref[...])
    o_ref[...] = _rms(h6, fn_ref[...])


def _final_layer(h, y2, route, p_all, layer, ple_norm, ple_gate, ple_proj, final_norm):
    t, d = h.shape
    pd = p_all.shape[2]
    tm = ROW_TILE
    row = lambda i: (i, 0)
    return pl.pallas_call(
        _final_kernel,
        grid=(t // tm,),
        in_specs=[pl.BlockSpec((tm, d), row), pl.BlockSpec((tm, y2.shape[1]), row),
                  pl.BlockSpec((tm, LANES), row), _layer_rows_spec(p_all, layer, tm),
                  _const_spec((1, d)), _const_spec((d, d)), _const_spec((pd, d)), _const_spec((1, d))],
        out_specs=pl.BlockSpec((tm, d), row),
        out_shape=jax.ShapeDtypeStruct((t, d), f32),
        compiler_params=_params("parallel"),
        name="final",
    )(h, y2, route, p_all, ple_norm.reshape(1, d), ple_gate.astype(bf16), ple_proj.astype(bf16),
      final_norm.reshape(1, d))


def kernel(x, p, pool_norm, pool_w, pool_scale, kv_norm, w_k, w_v, attn_norm, w_q, w_o, ffn_norm, ffn_w1, ffn_w3, ffn_w2, router, exp_w1, exp_w3, exp_w2, ple_norm, ple_gate, ple_proj, final_norm):
    batch, seq, d = x.shape
    t = batch * seq
    assert seq % ROW_TILE == 0 and ROW_TILE % MOBA_BLOCK == 0 and d % HEAD_DIM == 0
    assert p.shape[0] == 2 and router.shape[2] <= LANES
    h = x.reshape(t, d)
    pf = p.reshape(p.shape[0], t, p.shape[-1])

    h = _pool_layer(h, pool_norm[0], pool_w[0], pool_scale[0], seq)
    h = _swiglu_layer(h, ffn_norm[0], ffn_w1[0], ffn_w3[0], ffn_w2[0])
    h, q, k, v, km = _ple_qkv_layer(h, pf, 0, ple_norm[0], ple_gate[0], ple_proj[0],
                                    kv_norm, w_k, w_v, attn_norm[0], w_q[0], seq)
    attn = _attention(q, k, v, km, batch, seq)
    h, hn, route, route_t = _oproj_router_layer(h, attn, w_o[0], ffn_norm[1], router[0])
    choices = [route_t[k].astype(jnp.int32) for k in range(TOP_K_EXPERTS)]
    y = _experts_layer(hn, choices, exp_w1[0], exp_w3[0], exp_w2[0])
    out = _final_layer(h, y, route, pf, 1, ple_norm[1], ple_gate[1],
                       ple_proj[1], final_norm)
    return out.reshape(batch, seq, d)
```

```python
import functools

import jax
import jax.numpy as jnp
from jax import lax
from jax.experimental import pallas as pl
from jax.experimental.pallas import tpu as pltpu
from jax.experimental.pallas import tpu_sc as plsc

POOL_WINDOWS = (2, 4, 8, 16)
HEAD_DIM = 128
MOBA_BLOCK = 256
MOBA_TOPK = 3
ROPE_THETA = 500000.0
ROPE_DIM = HEAD_DIM // 4
TOP_K_EXPERTS = 2
RMS_EPS = 1e-6
NEG_INF = -1e30
REMOVED = -3e38

LANES = 128
SUBLANES = 8
ROW_TILE = 512
POOL_SUB = 128
ATTN_HEADS_PER_STEP = 4
ATTN_BLOCKS_PER_TRIP = 4
SUM_ROWS = 16
LOG2E = 1.4426950408889634
EXPERT_FF_CHUNK = 1792
EXPERT_CAST_STEPS = 32
SC_ROW = 128
SC_WINDOW = 128
VMEM_LIMIT = 56 * 1024 * 1024

bf16 = jnp.bfloat16
f32 = jnp.float32


def _dot(a, b):
    return jnp.dot(a, b, preferred_element_type=f32)


def _dot_split(a, b):
    a_hi = a.astype(bf16)
    a_lo = (a - a_hi.astype(f32)).astype(bf16)
    b_hi = b.astype(bf16)
    b_lo = (b - b_hi.astype(f32)).astype(bf16)
    return _dot(a_hi, b_hi) + (_dot(a_lo, b_hi) + _dot(a_hi, b_lo))


def _pack_pairs(x):
    half = x.shape[1] // 2
    lo = lax.bitcast_convert_type(x[:, :half].astype(bf16).astype(f32), jnp.uint32)
    hi = lax.bitcast_convert_type(x[:, half:].astype(bf16).astype(f32), jnp.uint32)
    return (hi & jnp.uint32(0xFFFF0000)) | (lo >> 16)


def _unpack_pairs(w):
    lo = lax.bitcast_convert_type(w << 16, f32)
    hi = lax.bitcast_convert_type(w & jnp.uint32(0xFFFF0000), f32)
    return jnp.concatenate([lo, hi], axis=1)


def _cast_kernel(*refs):
    n = len(refs) // 2
    for src, dst in zip(refs[:n], refs[n:]):
        dst[...] = src[...].astype(bf16)


def _cast_bf16(ws, steps):
    flat = [w.reshape(-1, w.shape[-1]) for w in ws]
    for a in flat:
        assert a.shape[0] % steps == 0 and (a.shape[0] // steps) % (2 * SUBLANES) == 0
    specs = [pl.BlockSpec((a.shape[0] // steps, a.shape[1]), lambda s: (s, 0)) for a in flat]
    outs = pl.pallas_call(
        _cast_kernel,
        grid=(steps,),
        in_specs=specs,
        out_specs=specs,
        out_shape=[jax.ShapeDtypeStruct(a.shape, bf16) for a in flat],
        compiler_params=_params("parallel"),
        name="cast_bf16",
    )(*flat)
    return [o.reshape(w.shape) for o, w in zip(outs, ws)]


def _rms(x, g):
    var = jnp.mean(x * x, axis=-1, keepdims=True)
    return x * lax.rsqrt(var + RMS_EPS) * g


def _params(*sem):
    return pltpu.CompilerParams(dimension_semantics=sem, vmem_limit_bytes=VMEM_LIMIT)


def _const_spec(shape):
    nd = len(shape)
    return pl.BlockSpec(shape, lambda *_: (0,) * nd)


def _pool_kernel(x_ref, halo_ref, g_ref, pw_ref, ps_ref, o_ref, pooled_ref, *, ts, seq):
    i = pl.program_id(0)
    g = g_ref[...]
    x = x_ref[...]
    xn = _rms(x, g)
    keep = jnp.where((i * ts) % seq == 0, 0.0, 1.0)
    hnb = (_rms(halo_ref[...], g) * keep).astype(bf16)
    xnb = xn.astype(bf16)
    gd = x.shape[1] // len(POOL_WINDOWS)
    r = lax.broadcasted_iota(jnp.int32, (POOL_SUB, 2 * POOL_SUB), 0)
    c = lax.broadcasted_iota(jnp.int32, (POOL_SUB, 2 * POOL_SUB), 1)
    dist = r + POOL_SUB - c
    bands = [((dist >= 0) & (dist < w)).astype(bf16) for w in POOL_WINDOWS]
    rows = lax.broadcasted_iota(jnp.int32, (POOL_SUB, 1), 0)
    for sb in range(ts // POOL_SUB):
        lo, hi = sb * POOL_SUB, (sb + 1) * POOL_SUB
        prev = hnb if sb == 0 else xnb[lo - POOL_SUB:lo]
        ext = jnp.concatenate([prev, xnb[lo:hi]], axis=0)
        tpos = (i * ts + lo) % seq + rows
        for gi, w in enumerate(POOL_WINDOWS):
            cs = slice(gi * gd, (gi + 1) * gd)
            wsum = _dot(bands[gi], ext[:, cs])
            cnt = jnp.minimum(tpos + 1, w).astype(f32)
            pooled_ref[lo:hi, cs] = (wsum / cnt - xn[lo:hi, cs]).astype(bf16)
    for gi in range(len(POOL_WINDOWS)):
        cs = slice(gi * gd, (gi + 1) * gd)
        mixed = _dot(pooled_ref[:, cs], pw_ref[gi])
        o_ref[:, cs] = x[:, cs] + mixed * ps_ref[:, cs]


def _pool_layer(h, norm, pool_w, pool_scale, seq):
    t, d = h.shape
    ts = ROW_TILE
    ng = len(POOL_WINDOWS)
    gd = d // ng
    per = ts // POOL_SUB
    return pl.pallas_call(
        functools.partial(_pool_kernel, ts=ts, seq=seq),
        grid=(t // ts,),
        in_specs=[
            pl.BlockSpec((ts, d), lambda i: (i, 0)),
            pl.BlockSpec((POOL_SUB, d), lambda i: (jnp.maximum(i * per - 1, 0), 0)),
            _const_spec((1, d)),
            _const_spec((ng, gd, gd)),
            _const_spec((1, d)),
        ],
        out_specs=pl.BlockSpec((ts, d), lambda i: (i, 0)),
        out_shape=jax.ShapeDtypeStruct((t, d), f32),
        scratch_shapes=[pltpu.VMEM((ts, d), bf16)],
        compiler_params=_params("parallel"),
        name="pool",
    )(h, h, norm.reshape(1, d), pool_w.astype(bf16), pool_scale.reshape(1, d))


def _swiglu_kernel(h_ref, g_ref, w1_ref, w3_ref, w2_ref, o_ref):
    x = h_ref[...]
    hn = _rms(x, g_ref[...]).astype(bf16)
    a = _dot(hn, w1_ref[...])
    b = _dot(hn, w3_ref[...])
    o_ref[...] = x + _dot((jax.nn.silu(a) * b).astype(bf16), w2_ref[...])


def _ff_chunk(ff, target):
    units = ff // LANES
    best = 1
    for k in range(1, units + 1):
        if units % k == 0 and k * LANES <= target:
            best = k
    return best * LANES


def _swiglu_layer(h, norm, w1, w3, w2):
    t, d = h.shape
    ff = w1.shape[1]
    tm = ROW_TILE
    w1r, w3r, w2r = w1.astype(bf16), w3.astype(bf16), w2.astype(bf16)
    resident = lambda shape: pl.BlockSpec(shape, lambda i: (0, 0), pipeline_mode=pl.Buffered(1))
    return pl.pallas_call(
        _swiglu_kernel,
        grid=(t // tm,),
        in_specs=[
            pl.BlockSpec((tm, d), lambda i: (i, 0)),
            _const_spec((1, d)),
            resident((d, ff)), resident((d, ff)), resident((ff, d)),
        ],
        out_specs=pl.BlockSpec((tm, d), lambda i: (i, 0)),
        out_shape=jax.ShapeDtypeStruct((t, d), f32),
        compiler_params=_params("parallel"),
        name="swiglu",
    )(h, norm.reshape(1, d), w1r, w3r, w2r)


def _rope_tables(seq):
    half = ROPE_DIM // 2
    inv_freq = jnp.float32(ROPE_THETA) ** (-(jnp.arange(0, ROPE_DIM, 2, dtype=f32) / ROPE_DIM))
    ang = jnp.arange(seq, dtype=f32)[:, None] * inv_freq[None, :]
    cos, sin = jnp.cos(ang), jnp.sin(ang)
    ones = jnp.ones((seq, HEAD_DIM - ROPE_DIM), f32)
    zeros = jnp.zeros((seq, HEAD_DIM - half), f32)
    cos_t = jnp.concatenate([cos, cos, ones], axis=1)
    up_t = jnp.concatenate([-sin, zeros], axis=1)
    dn_t = jnp.concatenate([jnp.zeros((seq, half), f32), sin, jnp.zeros((seq, HEAD_DIM - ROPE_DIM), f32)], axis=1)
    return cos_t, up_t, dn_t


def _rope(xh, cos_t, up_t, dn_t):
    half = ROPE_DIM // 2
    return (xh * cos_t + pltpu.roll(xh, HEAD_DIM - half, 1) * up_t + pltpu.roll(xh, half, 1) * dn_t)


def _ple_qkv_kernel(h_ref, p_ref, pn_ref, wg_ref, wp_ref, kvn_ref, wk_ref, wv_ref, an_ref, wq_ref,
                    cos_ref, up_ref, dn_ref, h_out, q_out, k_out, v_out, km_out):
    h = h_ref[...]
    hn = _rms(h, pn_ref[...]).astype(bf16)
    h3 = h + jax.nn.sigmoid(_dot(hn, wg_ref[...])) * _dot(p_ref[...].astype(bf16), wp_ref[...])
    h_out[...] = h3
    base = h3 * lax.rsqrt(jnp.mean(h3 * h3, axis=-1, keepdims=True) + RMS_EPS)
    kn = (base * kvn_ref[...]).astype(bf16)
    qn = (base * an_ref[...]).astype(bf16)
    v = _dot(kn, wv_ref[...])
    for bi in range(h.shape[0] // MOBA_BLOCK):
        v_out[bi] = v[bi * MOBA_BLOCK:(bi + 1) * MOBA_BLOCK].T.astype(bf16)
    k = _dot(kn, wk_ref[...])
    q = _dot(qn, wq_ref[...])
    cos_t, up_t, dn_t = cos_ref[...], up_ref[...], dn_ref[...]
    tm, d = h.shape
    sub = km_out.shape[0] // (tm // MOBA_BLOCK)
    for hh in range(d // HEAD_DIM):
        cs = slice(hh * HEAD_DIM, (hh + 1) * HEAD_DIM)
        q_out[:, cs] = _rope(q[:, cs], cos_t, up_t, dn_t)
        kr = _rope(k[:, cs], cos_t, up_t, dn_t)
        k_out[:, cs] = kr.astype(bf16)
        for bi in range(tm // MOBA_BLOCK):
            m = jnp.mean(kr[bi * MOBA_BLOCK:(bi + 1) * MOBA_BLOCK], axis=0, keepdims=True)
            km_out[bi * sub:(bi + 1) * sub, cs] = jnp.broadcast_to(m, (sub, HEAD_DIM))


def _layer_rows_spec(p_all, layer, tm):
    return pl.BlockSpec((None, tm, p_all.shape[2]), lambda i: (layer, i, 0))


def _ple_qkv_layer(h, p_all, layer, ple_norm, ple_gate, ple_proj, kv_norm, w_k, w_v, attn_norm, w_q, seq):
    t, d = h.shape
    pd = p_all.shape[2]
    tm = ROW_TILE
    sub = 8
    cos_t, up_t, dn_t = _rope_tables(seq)
    tiles_per_seq = seq // tm
    row = lambda i: (i, 0)
    tab = pl.BlockSpec((tm, HEAD_DIM), lambda i: (i % tiles_per_seq, 0))
    nkm = t // MOBA_BLOCK * sub
    outs = pl.pallas_call(
        _ple_qkv_kernel,
        grid=(t // tm,),
        in_specs=[
            pl.BlockSpec((tm, d), row), _layer_rows_spec(p_all, layer, tm),
            _const_spec((1, d)), _const_spec((d, d)), _const_spec((pd, d)),
            _const_spec((1, d)), _const_spec((d, d)), _const_spec((d, d)),
            _const_spec((1, d)), _const_spec((d, d)),
            tab, tab, tab,
        ],
        out_specs=[
            pl.BlockSpec((tm, d), row), pl.BlockSpec((tm, d), row),
            pl.BlockSpec((tm, d), row),
            pl.BlockSpec((tm // MOBA_BLOCK, d, MOBA_BLOCK), lambda i: (i, 0, 0)),
            pl.BlockSpec((tm // MOBA_BLOCK * sub, d), row),
        ],
        out_shape=[
            jax.ShapeDtypeStruct((t, d), f32), jax.ShapeDtypeStruct((t, d), f32),
            jax.ShapeDtypeStruct((t, d), bf16),
            jax.ShapeDtypeStruct((t // MOBA_BLOCK, d, MOBA_BLOCK), bf16),
            jax.ShapeDtypeStruct((nkm, d), f32),
        ],
        compiler_params=_params("parallel"),
        name="ple_qkv",
    )(h, p_all, ple_norm.reshape(1, d), ple_gate.astype(bf16), ple_proj.astype(bf16),
      kv_norm.reshape(1, d), w_k.astype(bf16), w_v.astype(bf16),
      attn_norm.reshape(1, d), w_q.astype(bf16), cos_t, up_t, dn_t)
    h3, q, k, v, km = outs
    return h3, q, k, v, km.reshape(t // MOBA_BLOCK, sub, d)[:, 0, :]


def _attn_kernel(q_ref, k_ref, vt_ref, km_ref, o_ref, qa_ref, sa_ref, sb_ref, pa_ref, pb_ref, acc_ref, *, hg):
    j = pl.program_id(2)
    nb = km_ref.shape[0]
    bs = q_ref.shape[0]
    hd = HEAD_DIM
    qscale = (hd ** -0.5) * LOG2E
    blk = lax.broadcasted_iota(jnp.int32, (nb, bs), 0).astype(f32)
    jf = j.astype(f32)
    krow = lax.broadcasted_iota(jnp.int32, (bs, bs), 0)
    qcol = lax.broadcasted_iota(jnp.int32, (bs, bs), 1)
    start_j = pl.multiple_of(j * bs, bs)

    ones_rows = jnp.ones((SUM_ROWS, bs), bf16)

    def weighted_values(n, h, pb):
        return _dot(jnp.concatenate([vt_ref[n, h * hd:(h + 1) * hd, :], ones_rows], axis=0), pb)

    init = []
    for h in range(hg):
        cs = slice(h * hd, (h + 1) * hd)
        qft = q_ref[:, cs].T
        gate = _dot_split(km_ref[:, cs], qft)
        cand = jnp.where(blk < jf, gate, NEG_INF)
        sel = blk < 0.0
        for _ in range(min(MOBA_TOPK, nb)):
            mx = jnp.max(cand, axis=0, keepdims=True)
            pick = jnp.min(jnp.where(cand == mx, blk, float(nb)), axis=0, keepdims=True)
            hit = blk == pick
            sel = sel | hit
            cand = jnp.where(hit, REMOVED, cand)
        bias = jnp.where(sel & (blk < jf), 0.0, NEG_INF)
        if nb < hd:
            bias = jnp.concatenate([bias, jnp.zeros((hd - nb, bs), f32)], axis=0)
        qbt = (qft * qscale).astype(bf16)
        qa_ref[h] = jnp.concatenate([qbt, bias.astype(bf16)], axis=0)

        s = _dot(k_ref[pl.ds(start_j, bs), cs], qbt)
        s = jnp.where(krow <= qcol, s, NEG_INF)
        m0 = jnp.max(s, axis=0, keepdims=True)
        acc_ref[h] = weighted_values(j, h, jnp.exp2((s - m0).astype(bf16)))
        init += [m0, jnp.ones_like(m0)]

    lane = lax.broadcasted_iota(jnp.int32, (bs, hd), 1)

    def block_scores(n, h):
        start = pl.multiple_of(n * bs, bs)
        onehot = (lane == n).astype(bf16)
        ka = jnp.concatenate([k_ref[pl.ds(start, bs), h * hd:(h + 1) * hd], onehot], axis=1)
        return _dot(ka, qa_ref[h])

    def stage(n, s_in, s_out, p_prev, p_out, stats):
        nn = jnp.minimum(n + 1, nb - 1)
        prev = jnp.clip(n - 1, 0, nb - 1)
        new = []
        for h in range(hg):
            m, alpha_p = stats[2 * h:2 * h + 2]
            acc_ref[h] = alpha_p * acc_ref[h] + weighted_values(prev, h, p_prev[h])
            s = s_in[h]
            m_new = jnp.maximum(m, jnp.max(s, axis=0, keepdims=True))
            p_out[h] = jnp.exp2((s - m_new).astype(bf16))
            s_out[h] = block_scores(nn, h)
            new += [m_new, jnp.exp2(m - m_new)]
        return new

    per_trip = ATTN_BLOCKS_PER_TRIP

    def body(i, stats):
        for u in range(0, per_trip, 2):
            stats = stage(per_trip * i + u, sa_ref, sb_ref, pb_ref, pa_ref, stats)
            stats = stage(per_trip * i + u + 1, sb_ref, sa_ref, pa_ref, pb_ref, stats)
        return tuple(stats)

    for h in range(hg):
        sa_ref[h] = block_scores(0, h)
        pb_ref[h] = jnp.zeros((bs, bs), bf16)
    trips = (j + per_trip - 1) // per_trip
    res = lax.fori_loop(0, trips, body, tuple(init))
    last = jnp.clip(per_trip * trips - 1, 0, nb - 1)
    for h in range(hg):
        alpha_p = res[2 * h + 1]
        acc = alpha_p * acc_ref[h] + weighted_values(last, h, pb_ref[h])
        o_ref[:, h * hd:(h + 1) * hd] = (acc[:hd] / acc[hd:hd + 1]).T.astype(o_ref.dtype)


def _attention(q, k, vt, km, batch, seq):
    t, d = q.shape
    nh = d // HEAD_DIM
    nb = seq // MOBA_BLOCK
    hg = ATTN_HEADS_PER_STEP
    assert nb <= HEAD_DIM and nh % hg == 0
    w = hg * HEAD_DIM
    return pl.pallas_call(
        functools.partial(_attn_kernel, hg=hg),
        grid=(batch, nh // hg, nb),
        in_specs=[
            pl.BlockSpec((MOBA_BLOCK, w), lambda b, g, j: (b * nb + j, g)),
            pl.BlockSpec((seq, w), lambda b, g, j: (b, g), pipeline_mode=pl.Buffered(1)),
            pl.BlockSpec((nb, w, MOBA_BLOCK), lambda b, g, j: (b, g, 0), pipeline_mode=pl.Buffered(1)),
            pl.BlockSpec((nb, w), lambda b, g, j: (b, g)),
        ],
        out_specs=pl.BlockSpec((MOBA_BLOCK, w), lambda b, g, j: (b * nb + j, g)),
        out_shape=jax.ShapeDtypeStruct((t, d), bf16),
        scratch_shapes=[
            pltpu.VMEM((hg, 2 * HEAD_DIM, MOBA_BLOCK), bf16),
            pltpu.VMEM((hg, MOBA_BLOCK, MOBA_BLOCK), f32), pltpu.VMEM((hg, MOBA_BLOCK, MOBA_BLOCK), f32),
            pltpu.VMEM((hg, MOBA_BLOCK, MOBA_BLOCK), bf16), pltpu.VMEM((hg, MOBA_BLOCK, MOBA_BLOCK), bf16),
            pltpu.VMEM((hg, HEAD_DIM + SUM_ROWS, MOBA_BLOCK), f32),
        ],
        compiler_params=_params("parallel", "parallel", "arbitrary"),
        name="attn",
    )(q, k, vt, km)


def _oproj_router_kernel(h_ref, a_ref, wo_ref, g_ref, r_ref, h_out, hn_out, route_out, route_t_out, *, n_exp):
    h4 = h_ref[...] + _dot(a_ref[...], wo_ref[...])
    h_out[...] = h4
    hn = _rms(h4, g_ref[...])
    hn_out[...] = _pack_pairs(hn)
    logits = _dot_split(hn, r_ref[...])
    lane = lax.broadcasted_iota(jnp.int32, logits.shape, 1).astype(f32)
    cand = jnp.where(lane < n_exp, logits, NEG_INF)
    m1 = jnp.max(cand, axis=1, keepdims=True)
    i1 = jnp.min(jnp.where(cand == m1, lane, float(LANES)), axis=1, keepdims=True)
    cand = jnp.where(lane == i1, REMOVED, cand)
    m2 = jnp.max(cand, axis=1, keepdims=True)
    i2 = jnp.min(jnp.where(cand == m2, lane, float(LANES)), axis=1, keepdims=True)
    e2 = jnp.exp(m2 - m1)
    den = 1.0 + e2
    route = jnp.where(lane == 0, i1, jnp.where(lane == 1, i2, jnp.where(
        lane == 2, 1.0 / den, jnp.where(lane == 3, e2 / den, 0.0))))
    route_out[...] = route
    route_t_out[...] = route.T[:SUBLANES]


def _oproj_router_layer(h, attn, w_o, norm, router):
    t, d = h.shape
    tm = ROW_TILE
    n_exp = router.shape[1]
    r_pad = jnp.zeros((d, LANES), f32).at[:, :n_exp].set(router)
    row = lambda i: (i, 0)
    return pl.pallas_call(
        functools.partial(_oproj_router_kernel, n_exp=n_exp),
        grid=(t // tm,),
        in_specs=[pl.BlockSpec((tm, d), row), pl.BlockSpec((tm, d), row), _const_spec((d, d)),
                  _const_spec((1, d)), _const_spec((d, LANES))],
        out_specs=[pl.BlockSpec((tm, d), row), pl.BlockSpec((tm, d // 2), row), pl.BlockSpec((tm, LANES), row),
                   pl.BlockSpec((SUBLANES, tm), lambda i: (0, i))],
        out_shape=[jax.ShapeDtypeStruct((t, d), f32), jax.ShapeDtypeStruct((t, d // 2), jnp.uint32),
                   jax.ShapeDtypeStruct((t, LANES), f32), jax.ShapeDtypeStruct((SUBLANES, t), f32)],
        compiler_params=_params("parallel"),
        name="oproj_router",
    )(h, attn, w_o.astype(bf16), norm.reshape(1, d), r_pad)


def _sc_gather(x, idx):
    n = idx.shape[0]
    assert x.shape[1] == SC_ROW and n % SC_WINDOW == 0
    mesh = plsc.VectorSubcoreMesh(core_axis_name="core", subcore_axis_name="subcore")

    @pl.kernel(out_type=jax.ShapeDtypeStruct((n, SC_ROW), x.dtype), mesh=mesh, scratch_types=[])
    def gather(x_hbm, i_hbm, o_hbm):
        def body(i_vmem, o_vmem):
            pltpu.sync_copy(x_hbm.at[i_vmem.at[0]], o_vmem)

        pltpu.emit_pipeline(
            body,
            grid=(n // SC_WINDOW,),
            in_specs=[pl.BlockSpec((1, SC_WINDOW), index_map=lambda i: (0, i))],
            out_specs=[pl.BlockSpec((SC_WINDOW, SC_ROW), index_map=lambda i: (i, 0))],
            core_axis_name=("core", "subcore"),
            dimension_semantics=(pltpu.PARALLEL,),
        )(i_hbm, o_hbm)

    return gather(x, idx.reshape(n // LANES, LANES).reshape(1, n))


def _pieces(a):
    n, d = a.shape
    return a.reshape(n // SUBLANES, SUBLANES, d // SC_ROW, SC_ROW).transpose(0, 2, 1, 3).reshape(-1, SC_ROW)


def _unpieces(p, d):
    per = d // SC_ROW
    n = p.shape[0] // per
    return p.reshape(n // SUBLANES, per, SUBLANES, SC_ROW).transpose(0, 2, 1, 3).reshape(n, d)


def _gather_tokens(x, rows):
    n, d = x.shape
    m = rows[0].shape[0]
    per = d // SC_ROW
    assert n % SUBLANES == 0 and m % SUBLANES == 0
    tiled = []
    for r in rows:
        first = (r // SUBLANES * (SUBLANES * per) + r % SUBLANES).reshape(m // SUBLANES, SUBLANES)
        tiled.append(jnp.tile(first, (1, per)))
    chunk = jnp.arange(per * SUBLANES, dtype=jnp.int32) // SUBLANES * SUBLANES
    idx = jnp.concatenate(tiled, axis=1) + jnp.tile(chunk, len(rows))[None, :]
    return _unpieces(_sc_gather(_pieces(x), idx.reshape(-1)), len(rows) * d)


def _expert_kernel(te_ref, tv_ref, x_ref, w1_ref, w3_ref, w2_ref, o_ref, xb_ref, acc_ref, *, nf):
    i = pl.program_id(0)
    f = pl.program_id(1)
    valid = tv_ref[i] > 0

    @pl.when(f == 0)
    def _():
        xb_ref[...] = _unpack_pairs(x_ref[...]).astype(bf16)
        acc_ref[...] = jnp.zeros_like(acc_ref)

    @pl.when(valid)
    def _():
        x = xb_ref[...]
        a = _dot(x, w1_ref[...])
        b = _dot(x, w3_ref[...])
        acc_ref[...] += _dot((jax.nn.silu(a) * b).astype(bf16), w2_ref[...])

    @pl.when(f == nf - 1)
    def _():
        o_ref[...] = _pack_pairs(acc_ref[...])


def _route_tables(choices, n_exp, tm):
    t = choices[0].shape[0]
    nslots = len(choices) * t
    experts = jnp.arange(n_exp, dtype=jnp.int32)[:, None]
    onehots = [(c[None, :] == experts).astype(jnp.int32) for c in choices]
    cums = [jnp.cumsum(oh, axis=1) for oh in onehots]
    totals = [cu[:, -1] for cu in cums]
    counts = sum(totals)
    tiles_e = (counts + tm - 1) // tm
    tile_end = jnp.cumsum(tiles_e)
    tile_start = tile_end - tiles_e
    group_start = jnp.cumsum(counts) - counts
    nt = nslots // tm + n_exp
    total = tile_end[-1]
    ti = jnp.arange(nt, dtype=jnp.int32)
    tv = (ti < total).astype(jnp.int32)
    tc = jnp.minimum(ti, total - 1)
    te = jnp.minimum(jnp.sum((tc[:, None] >= tile_end[None, :]).astype(jnp.int32), axis=1), n_exp - 1)
    order = jnp.argsort(jnp.concatenate(choices), stable=True).astype(jnp.int32)
    rank = ((tc - tile_start[te]) * tm)[:, None] + jnp.arange(tm, dtype=jnp.int32)[None, :]
    real = (rank < counts[te][:, None]) & (tv[:, None] > 0)
    sidx = jnp.clip(group_start[te][:, None] + rank, 0, nslots - 1)
    src = jnp.where(real, order[sidx] % t, 0).reshape(-1)
    pos = []
    earlier = jnp.zeros((n_exp,), jnp.int32)
    for oh, cu, tot in zip(onehots, cums, totals):
        pos.append(jnp.sum(oh * ((tile_start * tm + earlier)[:, None] + cu - oh), axis=0))
        earlier = earlier + tot
    return te, tv, src, pos, nt


def _experts_layer(hn, choices, w1, w3, w2):
    n_exp, d, ff = w1.shape
    tm = ROW_TILE
    fc = _ff_chunk(ff, EXPERT_FF_CHUNK)
    nf = ff // fc
    te, tv, src, pos, nt = _route_tables(choices, n_exp, tm)
    xs = _gather_tokens(hn, [src])
    w1r, w3r, w2r = _cast_bf16([w1, w3, w2], EXPERT_CAST_STEPS)

    def fsel(i, f, tv_r):
        return jnp.where(tv_r[i] > 0, f, nf - 1)

    def wmap_in(i, f, te_r, tv_r):
        return (te_r[i], 0, fsel(i, f, tv_r))

    def wmap_out(i, f, te_r, tv_r):
        return (te_r[i], fsel(i, f, tv_r), 0)

    grid_spec = pltpu.PrefetchScalarGridSpec(
        num_scalar_prefetch=2,
        grid=(nt, nf),
        in_specs=[
            pl.BlockSpec((tm, d // 2), lambda i, f, *_: (i, 0)),
            pl.BlockSpec((None, d, fc), wmap_in),
            pl.BlockSpec((None, d, fc), wmap_in),
            pl.BlockSpec((None, fc, d), wmap_out),
        ],
        out_specs=pl.BlockSpec((tm, d // 2), lambda i, f, *_: (i, 0)),
        scratch_shapes=[pltpu.VMEM((tm, d), bf16), pltpu.VMEM((tm, d), f32)],
    )
    ys = pl.pallas_call(
        functools.partial(_expert_kernel, nf=nf),
        grid_spec=grid_spec,
        out_shape=jax.ShapeDtypeStruct((nt * tm, d // 2), jnp.uint32),
        compiler_params=_params("parallel", "arbitrary"),
        name="experts",
    )(te, tv, xs, w1r, w3r, w2r)
    return _gather_tokens(ys, pos)


def _final_kernel(h_ref, y_ref, route_ref, p_ref, pn_ref, wg_ref, wp_ref, fn_ref, o_ref):
    d = h_ref.shape[1]
    route = route_ref[...]
    y = y_ref[...]
    h5 = h_ref[...] + route[:, 2:3] * _unpack_pairs(y[:, :d // 2]) + route[:, 3:4] * _unpack_pairs(y[:, d // 2:])
    hn = _rms(h5, pn_ref[...]).astype(bf16)
    h6 = h5 + jax.nn.sigmoid(_dot(hn, wg_ref[...])) * _dot(p_ref[...].astype(bf16), wp_ref[...])
    o_ref[...] = _rms(h6, fn_ref[...])


def _final_layer(h, y2, route, p_all, layer, ple_norm, ple_gate, ple_proj, final_norm):
    t, d = h.shape
    pd = p_all.shape[2]
    tm = ROW_TILE
    row = lambda i: (i, 0)
    return pl.pallas_call(
        _final_kernel,
        grid=(t // tm,),
        in_specs=[pl.BlockSpec((tm, d), row), pl.BlockSpec((tm, y2.shape[1]), row),
                  pl.BlockSpec((tm, LANES), row), _layer_rows_spec(p_all, layer, tm),
                  _const_spec((1, d)), _const_spec((d, d)), _const_spec((pd, d)), _const_spec((1, d))],
        out_specs=pl.BlockSpec((tm, d), row),
        out_shape=jax.ShapeDtypeStruct((t, d), f32),
        compiler_params=_params("parallel"),
        name="final",
    )(h, y2, route, p_all, ple_norm.reshape(1, d), ple_gate.astype(bf16), ple_proj.astype(bf16),
      final_norm.reshape(1, d))


def kernel(x, p, pool_norm, pool_w, pool_scale, kv_norm, w_k, w_v, attn_norm, w_q, w_o, ffn_norm, ffn_w1, ffn_w3, ffn_w2, router, exp_w1, exp_w3, exp_w2, ple_norm, ple_gate, ple_proj, final_norm):
    batch, seq, d = x.shape
    t = batch * seq
    assert seq % ROW_TILE == 0 and ROW_TILE % MOBA_BLOCK == 0 and d % HEAD_DIM == 0
    assert p.shape[0] == 2 and router.shape[2] <= LANES
    h = x.reshape(t, d)
    pf = p.reshape(p.shape[0], t, p.shape[-1])

    h = _pool_layer(h, pool_norm[0], pool_w[0], pool_scale[0], seq)
    h = _swiglu_layer(h, ffn_norm[0], ffn_w1[0], ffn_w3[0], ffn_w2[0])
    h, q, k, v, km = _ple_qkv_layer(h, pf, 0, ple_norm[0], ple_gate[0], ple_proj[0],
                                    kv_norm, w_k, w_v, attn_norm[0], w_q[0], seq)
    attn = _attention(q, k, v, km, batch, seq)
    h, hn, route, route_t = _oproj_router_layer(h, attn, w_o[0], ffn_norm[1], router[0])
    choices = [route_t[k].astype(jnp.int32) for k in range(TOP_K_EXPERTS)]
    y = _experts_layer(hn, choices, exp_w1[0], exp_w3[0], exp_w2[0])
    out = _final_layer(h, y, route, pf, 1, ple_norm[1], ple_gate[1],
                       ple_proj[1], final_norm)
    return out.reshape(batch, seq, d)
```

```python
import functools

import jax
import jax.numpy as jnp
from jax import lax
from jax.experimental import pallas as pl
from jax.experimental.pallas import tpu as pltpu
from jax.experimental.pallas import tpu_sc as plsc

POOL_WINDOWS = (2, 4, 8, 16)
HEAD_DIM = 128
MOBA_BLOCK = 256
MOBA_TOPK = 3
ROPE_THETA = 500000.0
ROPE_DIM = HEAD_DIM // 4
TOP_K_EXPERTS = 2
RMS_EPS = 1e-6
NEG_INF = -1e30
REMOVED = -3e38

LANES = 128
SUBLANES = 8
ROW_TILE = 512
POOL_SUB = 128
ATTN_HEADS_PER_STEP = 4
ATTN_BLOCKS_PER_TRIP = 4
SUM_ROWS = 16
LOG2E = 1.4426950408889634
EXPERT_FF_CHUNK = 1792
EXPERT_CAST_STEPS = 32
SC_ROW = 128
SC_WINDOW = 128
VMEM_LIMIT = 56 * 1024 * 1024

bf16 = jnp.bfloat16
f32 = jnp.float32


def _dot(a, b):
    return jnp.dot(a, b, preferred_element_type=f32)


def _dot_split(a, b):
    a_hi = a.astype(bf16)
    a_lo = (a - a_hi.astype(f32)).astype(bf16)
    b_hi = b.astype(bf16)
    b_lo = (b - b_hi.astype(f32)).astype(bf16)
    return _dot(a_hi, b_hi) + (_dot(a_lo, b_hi) + _dot(a_hi, b_lo))


def _pack_pairs(x):
    half = x.shape[1] // 2
    lo = lax.bitcast_convert_type(x[:, :half].astype(bf16).astype(f32), jnp.uint32)
    hi = lax.bitcast_convert_type(x[:, half:].astype(bf16).astype(f32), jnp.uint32)
    return (hi & jnp.uint32(0xFFFF0000)) | (lo >> 16)


def _unpack_pairs(w):
    lo = lax.bitcast_convert_type(w << 16, f32)
    hi = lax.bitcast_convert_type(w & jnp.uint32(0xFFFF0000), f32)
    return jnp.concatenate([lo, hi], axis=1)


def _cast_kernel(*refs):
    n = len(refs) // 2
    for src, dst in zip(refs[:n], refs[n:]):
        dst[...] = src[...].astype(bf16)


def _cast_bf16(ws, steps):
    flat = [w.reshape(-1, w.shape[-1]) for w in ws]
    for a in flat:
        assert a.shape[0] % steps == 0 and (a.shape[0] // steps) % (2 * SUBLANES) == 0
    specs = [pl.BlockSpec((a.shape[0] // steps, a.shape[1]), lambda s: (s, 0)) for a in flat]
    outs = pl.pallas_call(
        _cast_kernel,
        grid=(steps,),
        in_specs=specs,
        out_specs=specs,
        out_shape=[jax.ShapeDtypeStruct(a.shape, bf16) for a in flat],
        compiler_params=_params("parallel"),
        name="cast_bf16",
    )(*flat)
    return [o.reshape(w.shape) for o, w in zip(outs, ws)]


def _rms(x, g):
    var = jnp.mean(x * x, axis=-1, keepdims=True)
    return x * lax.rsqrt(var + RMS_EPS) * g


def _params(*sem):
    return pltpu.CompilerParams(dimension_semantics=sem, vmem_limit_bytes=VMEM_LIMIT)


def _const_spec(shape):
    nd = len(shape)
    return pl.BlockSpec(shape, lambda *_: (0,) * nd)


def _pool_kernel(x_ref, halo_ref, g_ref, pw_ref, ps_ref, after_ref, o_ref, pooled_ref, *, ts, seq):
    del after_ref
    i = pl.program_id(0)
    g = g_ref[...]
    x = x_ref[...]
    xn = _rms(x, g)
    keep = jnp.where((i * ts) % seq == 0, 0.0, 1.0)
    hnb = (_rms(halo_ref[...], g) * keep).astype(bf16)
    xnb = xn.astype(bf16)
    gd = x.shape[1] // len(POOL_WINDOWS)
    r = lax.broadcasted_iota(jnp.int32, (POOL_SUB, 2 * POOL_SUB), 0)
    c = lax.broadcasted_iota(jnp.int32, (POOL_SUB, 2 * POOL_SUB), 1)
    dist = r + POOL_SUB - c
    bands = [((dist >= 0) & (dist < w)).astype(bf16) for w in POOL_WINDOWS]
    rows = lax.broadcasted_iota(jnp.int32, (POOL_SUB, 1), 0)
    for sb in range(ts // POOL_SUB):
        lo, hi = sb * POOL_SUB, (sb + 1) * POOL_SUB
        prev = hnb if sb == 0 else xnb[lo - POOL_SUB:lo]
        ext = jnp.concatenate([prev, xnb[lo:hi]], axis=0)
        tpos = (i * ts + lo) % seq + rows
        for gi, w in enumerate(POOL_WINDOWS):
            cs = slice(gi * gd, (gi + 1) * gd)
            wsum = _dot(bands[gi], ext[:, cs])
            cnt = jnp.minimum(tpos + 1, w).astype(f32)
            pooled_ref[lo:hi, cs] = (wsum / cnt - xn[lo:hi, cs]).astype(bf16)
    for gi in range(len(POOL_WINDOWS)):
        cs = slice(gi * gd, (gi + 1) * gd)
        mixed = _dot(pooled_ref[:, cs], pw_ref[gi])
        o_ref[:, cs] = x[:, cs] + mixed * ps_ref[:, cs]


def _pool_layer(h, norm, pool_w, pool_scale, seq, after):
    t, d = h.shape
    ts = ROW_TILE
    ng = len(POOL_WINDOWS)
    gd = d // ng
    per = ts // POOL_SUB
    return pl.pallas_call(
        functools.partial(_pool_kernel, ts=ts, seq=seq),
        grid=(t // ts,),
        in_specs=[
            pl.BlockSpec((ts, d), lambda i: (i, 0)),
            pl.BlockSpec((POOL_SUB, d), lambda i: (jnp.maximum(i * per - 1, 0), 0)),
            _const_spec((1, d)),
            _const_spec((ng, gd, gd)),
            _const_spec((1, d)),
            _const_spec(after.shape),
        ],
        out_specs=pl.BlockSpec((ts, d), lambda i: (i, 0)),
        out_shape=jax.ShapeDtypeStruct((t, d), f32),
        scratch_shapes=[pltpu.VMEM((ts, d), bf16)],
        compiler_params=_params("parallel"),
        name="pool",
    )(h, h, norm.reshape(1, d), pool_w.astype(bf16), pool_scale.reshape(1, d), after)


def _swiglu_kernel(h_ref, g_ref, w1_ref, w3_ref, w2_ref, o_ref):
    x = h_ref[...]
    hn = _rms(x, g_ref[...]).astype(bf16)
    a = _dot(hn, w1_ref[...])
    b = _dot(hn, w3_ref[...])
    o_ref[...] = x + _dot((jax.nn.silu(a) * b).astype(bf16), w2_ref[...])


def _ff_chunk(ff, target):
    units = ff // LANES
    best = 1
    for k in range(1, units + 1):
        if units % k == 0 and k * LANES <= target:
            best = k
    return best * LANES


def _swiglu_layer(h, norm, w1, w3, w2):
    t, d = h.shape
    ff = w1.shape[1]
    tm = ROW_TILE
    w1r, w3r, w2r = w1.astype(bf16), w3.astype(bf16), w2.astype(bf16)
    resident = lambda shape: pl.BlockSpec(shape, lambda i: (0, 0), pipeline_mode=pl.Buffered(1))
    return pl.pallas_call(
        _swiglu_kernel,
        grid=(t // tm,),
        in_specs=[
            pl.BlockSpec((tm, d), lambda i: (i, 0)),
            _const_spec((1, d)),
            resident((d, ff)), resident((d, ff)), resident((ff, d)),
        ],
        out_specs=pl.BlockSpec((tm, d), lambda i: (i, 0)),
        out_shape=jax.ShapeDtypeStruct((t, d), f32),
        compiler_params=_params("parallel"),
        name="swiglu",
    )(h, norm.reshape(1, d), w1r, w3r, w2r)


def _rope_tables(seq):
    half = ROPE_DIM // 2
    inv_freq = jnp.float32(ROPE_THETA) ** (-(jnp.arange(0, ROPE_DIM, 2, dtype=f32) / ROPE_DIM))
    ang = jnp.arange(seq, dtype=f32)[:, None] * inv_freq[None, :]
    cos, sin = jnp.cos(ang), jnp.sin(ang)
    ones = jnp.ones((seq, HEAD_DIM - ROPE_DIM), f32)
    zeros = jnp.zeros((seq, HEAD_DIM - half), f32)
    cos_t = jnp.concatenate([cos, cos, ones], axis=1)
    up_t = jnp.concatenate([-sin, zeros], axis=1)
    dn_t = jnp.concatenate([jnp.zeros((seq, half), f32), sin, jnp.zeros((seq, HEAD_DIM - ROPE_DIM), f32)], axis=1)
    return cos_t, up_t, dn_t


def _rope(xh, cos_t, up_t, dn_t):
    half = ROPE_DIM // 2
    return (xh * cos_t + pltpu.roll(xh, HEAD_DIM - half, 1) * up_t + pltpu.roll(xh, half, 1) * dn_t)


def _ple_qkv_kernel(h_ref, p_ref, pn_ref, wg_ref, wp_ref, kvn_ref, wk_ref, wv_ref, an_ref, wq_ref,
                    cos_ref, up_ref, dn_ref, h_out, q_out, k_out, v_out, km_out):
    h = h_ref[...]
    hn = _rms(h, pn_ref[...]).astype(bf16)
    h3 = h + jax.nn.sigmoid(_dot(hn, wg_ref[...])) * _dot(p_ref[...].astype(bf16), wp_ref[...])
    h_out[...] = h3
    base = h3 * lax.rsqrt(jnp.mean(h3 * h3, axis=-1, keepdims=True) + RMS_EPS)
    kn = (base * kvn_ref[...]).astype(bf16)
    qn = (base * an_ref[...]).astype(bf16)
    v = _dot(kn, wv_ref[...])
    for bi in range(h.shape[0] // MOBA_BLOCK):
        v_out[bi] = v[bi * MOBA_BLOCK:(bi + 1) * MOBA_BLOCK].T.astype(bf16)
    k = _dot(kn, wk_ref[...])
    q = _dot(qn, wq_ref[...])
    cos_t, up_t, dn_t = cos_ref[...], up_ref[...], dn_ref[...]
    tm, d = h.shape
    sub = km_out.shape[0] // (tm // MOBA_BLOCK)
    for hh in range(d // HEAD_DIM):
        cs = slice(hh * HEAD_DIM, (hh + 1) * HEAD_DIM)
        q_out[:, cs] = _rope(q[:, cs], cos_t, up_t, dn_t)
        kr = _rope(k[:, cs], cos_t, up_t, dn_t)
        k_out[:, cs] = kr.astype(bf16)
        for bi in range(tm // MOBA_BLOCK):
            m = jnp.mean(kr[bi * MOBA_BLOCK:(bi + 1) * MOBA_BLOCK], axis=0, keepdims=True)
            km_out[bi * sub:(bi + 1) * sub, cs] = jnp.broadcast_to(m, (sub, HEAD_DIM))


def _layer_rows_spec(p_all, layer, tm):
    return pl.BlockSpec((None, tm, p_all.shape[2]), lambda i: (layer, i, 0))


def _ple_qkv_layer(h, p_all, layer, ple_norm, ple_gate, ple_proj, kv_norm, w_k, w_v, attn_norm, w_q, seq):
    t, d = h.shape
    pd = p_all.shape[2]
    tm = ROW_TILE
    sub = 8
    cos_t, up_t, dn_t = _rope_tables(seq)
    tiles_per_seq = seq // tm
    row = lambda i: (i, 0)
    tab = pl.BlockSpec((tm, HEAD_DIM), lambda i: (i % tiles_per_seq, 0))
    nkm = t // MOBA_BLOCK * sub
    outs = pl.pallas_call(
        _ple_qkv_kernel,
        grid=(t // tm,),
        in_specs=[
            pl.BlockSpec((tm, d), row), _layer_rows_spec(p_all, layer, tm),
            _const_spec((1, d)), _const_spec((d, d)), _const_spec((pd, d)),
            _const_spec((1, d)), _const_spec((d, d)), _const_spec((d, d)),
            _const_spec((1, d)), _const_spec((d, d)),
            tab, tab, tab,
        ],
        out_specs=[
            pl.BlockSpec((tm, d), row), pl.BlockSpec((tm, d), row),
            pl.BlockSpec((tm, d), row),
            pl.BlockSpec((tm // MOBA_BLOCK, d, MOBA_BLOCK), lambda i: (i, 0, 0)),
            pl.BlockSpec((tm // MOBA_BLOCK * sub, d), row),
        ],
        out_shape=[
            jax.ShapeDtypeStruct((t, d), f32), jax.ShapeDtypeStruct((t, d), f32),
            jax.ShapeDtypeStruct((t, d), bf16),
            jax.ShapeDtypeStruct((t // MOBA_BLOCK, d, MOBA_BLOCK), bf16),
            jax.ShapeDtypeStruct((nkm, d), f32),
        ],
        compiler_params=_params("parallel"),
        name="ple_qkv",
    )(h, p_all, ple_norm.reshape(1, d), ple_gate.astype(bf16), ple_proj.astype(bf16),
      kv_norm.reshape(1, d), w_k.astype(bf16), w_v.astype(bf16),
      attn_norm.reshape(1, d), w_q.astype(bf16), cos_t, up_t, dn_t)
    h3, q, k, v, km = outs
    return h3, q, k, v, km.reshape(t // MOBA_BLOCK, sub, d)[:, 0, :]


def _attn_kernel(q_ref, k_ref, vt_ref, km_ref, o_ref, qa_ref, sa_ref, sb_ref, pa_ref, pb_ref, acc_ref, *, hg):
    j = pl.program_id(2)
    nb = km_ref.shape[0]
    bs = q_ref.shape[0]
    hd = HEAD_DIM
    qscale = (hd ** -0.5) * LOG2E
    blk = lax.broadcasted_iota(jnp.int32, (nb, bs), 0).astype(f32)
    jf = j.astype(f32)
    krow = lax.broadcasted_iota(jnp.int32, (bs, bs), 0)
    qcol = lax.broadcasted_iota(jnp.int32, (bs, bs), 1)
    start_j = pl.multiple_of(j * bs, bs)

    ones_rows = jnp.ones((SUM_ROWS, bs), bf16)

    def weighted_values(n, h, pb):
        return _dot(jnp.concatenate([vt_ref[n, h * hd:(h + 1) * hd, :], ones_rows], axis=0), pb)

    init = []
    for h in range(hg):
        cs = slice(h * hd, (h + 1) * hd)
        qft = q_ref[:, cs].T
        gate = _dot_split(km_ref[:, cs], qft)
        cand = jnp.where(blk < jf, gate, NEG_INF)
        sel = blk < 0.0
        for _ in range(min(MOBA_TOPK, nb)):
            mx = jnp.max(cand, axis=0, keepdims=True)
            pick = jnp.min(jnp.where(cand == mx, blk, float(nb)), axis=0, keepdims=True)
            hit = blk == pick
            sel = sel | hit
            cand = jnp.where(hit, REMOVED, cand)
        bias = jnp.where(sel & (blk < jf), 0.0, NEG_INF)
        if nb < hd:
            bias = jnp.concatenate([bias, jnp.zeros((hd - nb, bs), f32)], axis=0)
        qbt = (qft * qscale).astype(bf16)
        qa_ref[h] = jnp.concatenate([qbt, bias.astype(bf16)], axis=0)

        s = _dot(k_ref[pl.ds(start_j, bs), cs], qbt)
        s = jnp.where(krow <= qcol, s, NEG_INF)
        m0 = jnp.max(s, axis=0, keepdims=True)
        acc_ref[h] = weighted_values(j, h, jnp.exp2((s - m0).astype(bf16)))
        init += [m0, jnp.ones_like(m0)]

    lane = lax.broadcasted_iota(jnp.int32, (bs, hd), 1)

    def block_scores(n, h):
        start = pl.multiple_of(n * bs, bs)
        onehot = (lane == n).astype(bf16)
        ka = jnp.concatenate([k_ref[pl.ds(start, bs), h * hd:(h + 1) * hd], onehot], axis=1)
        return _dot(ka, qa_ref[h])

    def stage(n, s_in, s_out, p_prev, p_out, stats):
        nn = jnp.minimum(n + 1, nb - 1)
        prev = jnp.clip(n - 1, 0, nb - 1)
        new = []
        for h in range(hg):
            m, alpha_p = stats[2 * h:2 * h + 2]
            acc_ref[h] = alpha_p * acc_ref[h] + weighted_values(prev, h, p_prev[h])
            s = s_in[h]
            m_new = jnp.maximum(m, jnp.max(s, axis=0, keepdims=True))
            p_out[h] = jnp.exp2((s - m_new).astype(bf16))
            s_out[h] = block_scores(nn, h)
            new += [m_new, jnp.exp2(m - m_new)]
        return new

    per_trip = ATTN_BLOCKS_PER_TRIP

    def body(i, stats):
        for u in range(0, per_trip, 2):
            stats = stage(per_trip * i + u, sa_ref, sb_ref, pb_ref, pa_ref, stats)
            stats = stage(per_trip * i + u + 1, sb_ref, sa_ref, pa_ref, pb_ref, stats)
        return tuple(stats)

    for h in range(hg):
        sa_ref[h] = block_scores(0, h)
        pb_ref[h] = jnp.zeros((bs, bs), bf16)
    trips = (j + per_trip - 1) // per_trip
    res = lax.fori_loop(0, trips, body, tuple(init))
    last = jnp.clip(per_trip * trips - 1, 0, nb - 1)
    for h in range(hg):
        alpha_p = res[2 * h + 1]
        acc = alpha_p * acc_ref[h] + weighted_values(last, h, pb_ref[h])
        o_ref[:, h * hd:(h + 1) * hd] = (acc[:hd] / acc[hd:hd + 1]).T.astype(o_ref.dtype)


def _attention(q, k, vt, km, batch, seq):
    t, d = q.shape
    nh = d // HEAD_DIM
    nb = seq // MOBA_BLOCK
    hg = ATTN_HEADS_PER_STEP
    assert nb <= HEAD_DIM and nh % hg == 0
    w = hg * HEAD_DIM
    return pl.pallas_call(
        functools.partial(_attn_kernel, hg=hg),
        grid=(batch, nh // hg, nb),
        in_specs=[
            pl.BlockSpec((MOBA_BLOCK, w), lambda b, g, j: (b * nb + j, g)),
            pl.BlockSpec((seq, w), lambda b, g, j: (b, g), pipeline_mode=pl.Buffered(1)),
            pl.BlockSpec((nb, w, MOBA_BLOCK), lambda b, g, j: (b, g, 0), pipeline_mode=pl.Buffered(1)),
            pl.BlockSpec((nb, w), lambda b, g, j: (b, g)),
        ],
        out_specs=pl.BlockSpec((MOBA_BLOCK, w), lambda b, g, j: (b * nb + j, g)),
        out_shape=jax.ShapeDtypeStruct((t, d), bf16),
        scratch_shapes=[
            pltpu.VMEM((hg, 2 * HEAD_DIM, MOBA_BLOCK), bf16),
            pltpu.VMEM((hg, MOBA_BLOCK, MOBA_BLOCK), f32), pltpu.VMEM((hg, MOBA_BLOCK, MOBA_BLOCK), f32),
            pltpu.VMEM((hg, MOBA_BLOCK, MOBA_BLOCK), bf16), pltpu.VMEM((hg, MOBA_BLOCK, MOBA_BLOCK), bf16),
            pltpu.VMEM((hg, HEAD_DIM + SUM_ROWS, MOBA_BLOCK), f32),
        ],
        compiler_params=_params("parallel", "parallel", "arbitrary"),
        name="attn",
    )(q, k, vt, km)


def _oproj_router_kernel(h_ref, a_ref, wo_ref, g_ref, r_ref, h_out, hn_out, route_out, route_t_out, *, n_exp):
    h4 = h_ref[...] + _dot(a_ref[...], wo_ref[...])
    h_out[...] = h4
    hn = _rms(h4, g_ref[...])
    hn_out[...] = _pack_pairs(hn)
    logits = _dot_split(hn, r_ref[...])
    lane = lax.broadcasted_iota(jnp.int32, logits.shape, 1).astype(f32)
    cand = jnp.where(lane < n_exp, logits, NEG_INF)
    m1 = jnp.max(cand, axis=1, keepdims=True)
    i1 = jnp.min(jnp.where(cand == m1, lane, float(LANES)), axis=1, keepdims=True)
    cand = jnp.where(lane == i1, REMOVED, cand)
    m2 = jnp.max(cand, axis=1, keepdims=True)
    i2 = jnp.min(jnp.where(cand == m2, lane, float(LANES)), axis=1, keepdims=True)
    e2 = jnp.exp(m2 - m1)
    den = 1.0 + e2
    route = jnp.where(lane == 0, i1, jnp.where(lane == 1, i2, jnp.where(
        lane == 2, 1.0 / den, jnp.where(lane == 3, e2 / den, 0.0))))
    route_out[...] = route
    route_t_out[...] = route.T[:SUBLANES]


def _oproj_router_layer(h, attn, w_o, norm, router):
    t, d = h.shape
    tm = ROW_TILE
    n_exp = router.shape[1]
    r_pad = jnp.zeros((d, LANES), f32).at[:, :n_exp].set(router)
    row = lambda i: (i, 0)
    return pl.pallas_call(
        functools.partial(_oproj_router_kernel, n_exp=n_exp),
        grid=(t // tm,),
        in_specs=[pl.BlockSpec((tm, d), row), pl.BlockSpec((tm, d), row), _const_spec((d, d)),
                  _const_spec((1, d)), _const_spec((d, LANES))],
        out_specs=[pl.BlockSpec((tm, d), row), pl.BlockSpec((tm, d // 2), row), pl.BlockSpec((tm, LANES), row),
                   pl.BlockSpec((SUBLANES, tm), lambda i: (0, i))],
        out_shape=[jax.ShapeDtypeStruct((t, d), f32), jax.ShapeDtypeStruct((t, d // 2), jnp.uint32),
                   jax.ShapeDtypeStruct((t, LANES), f32), jax.ShapeDtypeStruct((SUBLANES, t), f32)],
        compiler_params=_params("parallel"),
        name="oproj_router",
    )(h, attn, w_o.astype(bf16), norm.reshape(1, d), r_pad)


def _sc_gather(x, idx):
    n = idx.shape[0]
    assert x.shape[1] == SC_ROW and n % SC_WINDOW == 0
    mesh = plsc.VectorSubcoreMesh(core_axis_name="core", subcore_axis_name="subcore")

    @pl.kernel(out_type=jax.ShapeDtypeStruct((n, SC_ROW), x.dtype), mesh=mesh, scratch_types=[])
    def gather(x_hbm, i_hbm, o_hbm):
        def body(i_vmem, o_vmem):
            pltpu.sync_copy(x_hbm.at[i_vmem.at[0]], o_vmem)

        pltpu.emit_pipeline(
            body,
            grid=(n // SC_WINDOW,),
            in_specs=[pl.BlockSpec((1, SC_WINDOW), index_map=lambda i: (0, i))],
            out_specs=[pl.BlockSpec((SC_WINDOW, SC_ROW), index_map=lambda i: (i, 0))],
            core_axis_name=("core", "subcore"),
            dimension_semantics=(pltpu.PARALLEL,),
        )(i_hbm, o_hbm)

    return gather(x, idx.reshape(n // LANES, LANES).reshape(1, n))


def _pieces(a):
    n, d = a.shape
    return a.reshape(n // SUBLANES, SUBLANES, d // SC_ROW, SC_ROW).transpose(0, 2, 1, 3).reshape(-1, SC_ROW)


def _unpieces(p, d):
    per = d // SC_ROW
    n = p.shape[0] // per
    return p.reshape(n // SUBLANES, per, SUBLANES, SC_ROW).transpose(0, 2, 1, 3).reshape(n, d)


def _gather_tokens(x, rows):
    n, d = x.shape
    m = rows[0].shape[0]
    per = d // SC_ROW
    assert n % SUBLANES == 0 and m % SUBLANES == 0
    tiled = []
    for r in rows:
        first = (r // SUBLANES * (SUBLANES * per) + r % SUBLANES).reshape(m // SUBLANES, SUBLANES)
        tiled.append(jnp.tile(first, (1, per)))
    chunk = jnp.arange(per * SUBLANES, dtype=jnp.int32) // SUBLANES * SUBLANES
    idx = jnp.concatenate(tiled, axis=1) + jnp.tile(chunk, len(rows))[None, :]
    return _unpieces(_sc_gather(_pieces(x), idx.reshape(-1)), len(rows) * d)


def _expert_kernel(te_ref, tv_ref, x_ref, w1_ref, w3_ref, w2_ref, o_ref, xb_ref, acc_ref, *, nf):
    i = pl.program_id(0)
    f = pl.program_id(1)
    valid = tv_ref[i] > 0

    @pl.when(f == 0)
    def _():
        xb_ref[...] = _unpack_pairs(x_ref[...]).astype(bf16)
        acc_ref[...] = jnp.zeros_like(acc_ref)

    @pl.when(valid)
    def _():
        x = xb_ref[...]
        a = _dot(x, w1_ref[...])
        b = _dot(x, w3_ref[...])
        acc_ref[...] += _dot((jax.nn.silu(a) * b).astype(bf16), w2_ref[...])

    @pl.when(f == nf - 1)
    def _():
        o_ref[...] = _pack_pairs(acc_ref[...])


def _route_tables(choices, n_exp, tm):
    t = choices[0].shape[0]
    nslots = len(choices) * t
    experts = jnp.arange(n_exp, dtype=jnp.int32)[:, None]
    onehots = [(c[None, :] == experts).astype(jnp.int32) for c in choices]
    cums = [jnp.cumsum(oh, axis=1) for oh in onehots]
    totals = [cu[:, -1] for cu in cums]
    counts = sum(totals)
    tiles_e = (counts + tm - 1) // tm
    tile_end = jnp.cumsum(tiles_e)
    tile_start = tile_end - tiles_e
    group_start = jnp.cumsum(counts) - counts
    nt = nslots // tm + n_exp
    total = tile_end[-1]
    ti = jnp.arange(nt, dtype=jnp.int32)
    tv = (ti < total).astype(jnp.int32)
    tc = jnp.minimum(ti, total - 1)
    te = jnp.minimum(jnp.sum((tc[:, None] >= tile_end[None, :]).astype(jnp.int32), axis=1), n_exp - 1)
    order = jnp.argsort(jnp.concatenate(choices), stable=True).astype(jnp.int32)
    rank = ((tc - tile_start[te]) * tm)[:, None] + jnp.arange(tm, dtype=jnp.int32)[None, :]
    real = (rank < counts[te][:, None]) & (tv[:, None] > 0)
    sidx = jnp.clip(group_start[te][:, None] + rank, 0, nslots - 1)
    src = jnp.where(real, order[sidx] % t, 0).reshape(-1)
    pos = []
    earlier = jnp.zeros((n_exp,), jnp.int32)
    for oh, cu, tot in zip(onehots, cums, totals):
        pos.append(jnp.sum(oh * ((tile_start * tm + earlier)[:, None] + cu - oh), axis=0))
        earlier = earlier + tot
    return te, tv, src, pos, nt


def _experts_layer(hn, choices, w1, w3, w2):
    n_exp, d, ff = w1.shape
    tm = ROW_TILE
    fc = _ff_chunk(ff, EXPERT_FF_CHUNK)
    nf = ff // fc
    te, tv, src, pos, nt = _route_tables(choices, n_exp, tm)
    xs = _gather_tokens(hn, [src])
    w1r, w3r, w2r = w1, w3, w2

    def fsel(i, f, tv_r):
        return jnp.where(tv_r[i] > 0, f, nf - 1)

    def wmap_in(i, f, te_r, tv_r):
        return (te_r[i], 0, fsel(i, f, tv_r))

    def wmap_out(i, f, te_r, tv_r):
        return (te_r[i], fsel(i, f, tv_r), 0)

    grid_spec = pltpu.PrefetchScalarGridSpec(
        num_scalar_prefetch=2,
        grid=(nt, nf),
        in_specs=[
            pl.BlockSpec((tm, d // 2), lambda i, f, *_: (i, 0)),
            pl.BlockSpec((None, d, fc), wmap_in),
            pl.BlockSpec((None, d, fc), wmap_in),
            pl.BlockSpec((None, fc, d), wmap_out),
        ],
        out_specs=pl.BlockSpec((tm, d // 2), lambda i, f, *_: (i, 0)),
        scratch_shapes=[pltpu.VMEM((tm, d), bf16), pltpu.VMEM((tm, d), f32)],
    )
    ys = pl.pallas_call(
        functools.partial(_expert_kernel, nf=nf),
        grid_spec=grid_spec,
        out_shape=jax.ShapeDtypeStruct((nt * tm, d // 2), jnp.uint32),
        compiler_params=_params("parallel", "arbitrary"),
        name="experts",
    )(te, tv, xs, w1r, w3r, w2r)
    return _gather_tokens(ys, pos)


def _final_kernel(h_ref, y_ref, route_ref, p_ref, pn_ref, wg_ref, wp_ref, fn_ref, o_ref):
    d = h_ref.shape[1]
    route = route_ref[...]
    y = y_ref[...]
    h5 = h_ref[...] + route[:, 2:3] * _unpack_pairs(y[:, :d // 2]) + route[:, 3:4] * _unpack_pairs(y[:, d // 2:])
    hn = _rms(h5, pn_ref[...]).astype(bf16)
    h6 = h5 + jax.nn.sigmoid(_dot(hn, wg_ref[...])) * _dot(p_ref[...].astype(bf16), wp_ref[...])
    o_ref[...] = _rms(h6, fn_ref[...])


def _final_layer(h, y2, route, p_all, layer, ple_norm, ple_gate, ple_proj, final_norm):
    t, d = h.shape
    pd = p_all.shape[2]
    tm = ROW_TILE
    row = lambda i: (i, 0)
    return pl.pallas_call(
        _final_kernel,
        grid=(t // tm,),
        in_specs=[pl.BlockSpec((tm, d), row), pl.BlockSpec((tm, y2.shape[1]), row),
                  pl.BlockSpec((tm, LANES), row), _layer_rows_spec(p_all, layer, tm),
                  _const_spec((1, d)), _const_spec((d, d)), _const_spec((pd, d)), _const_spec((1, d))],
        out_specs=pl.BlockSpec((tm, d), row),
        out_shape=jax.ShapeDtypeStruct((t, d), f32),
        compiler_params=_params("parallel"),
        name="final",
    )(h, y2, route, p_all, ple_norm.reshape(1, d), ple_gate.astype(bf16), ple_proj.astype(bf16),
      final_norm.reshape(1, d))


def kernel(x, p, pool_norm, pool_w, pool_scale, kv_norm, w_k, w_v, attn_norm, w_q, w_o, ffn_norm, ffn_w1, ffn_w3, ffn_w2, router, exp_w1, exp_w3, exp_w2, ple_norm, ple_gate, ple_proj, final_norm):
    batch, seq, d = x.shape
    t = batch * seq
    assert seq % ROW_TILE == 0 and ROW_TILE % MOBA_BLOCK == 0 and d % HEAD_DIM == 0
    assert p.shape[0] == 2 and router.shape[2] <= LANES
    h = x.reshape(t, d)
    pf = p.reshape(p.shape[0], t, p.shape[-1])

    ew1, ew3, ew2 = _cast_bf16([exp_w1[0], exp_w3[0], exp_w2[0]], EXPERT_CAST_STEPS)
    h = _pool_layer(h, pool_norm[0], pool_w[0], pool_scale[0], seq, ew2[0, :2 * SUBLANES, :LANES])
    h = _swiglu_layer(h, ffn_norm[0], ffn_w1[0], ffn_w3[0], ffn_w2[0])
    h, q, k, v, km = _ple_qkv_layer(h, pf, 0, ple_norm[0], ple_gate[0], ple_proj[0],
                                    kv_norm, w_k, w_v, attn_norm[0], w_q[0], seq)
    attn = _attention(q, k, v, km, batch, seq)
    h, hn, route, route_t = _oproj_router_layer(h, attn, w_o[0], ffn_norm[1], router[0])
    choices = [route_t[k].astype(jnp.int32) for k in range(TOP_K_EXPERTS)]
    y = _experts_layer(hn, choices, ew1, ew3, ew2)
    out = _final_layer(h, y, route, pf, 1, ple_norm[1], ple_gate[1],
                       ple_proj[1], final_norm)
    return out.reshape(batch, seq, d)
```

```python
import functools

import jax
import jax.numpy as jnp
from jax import lax
from jax.experimental import pallas as pl
from jax.experimental.pallas import tpu as pltpu
from jax.experimental.pallas import tpu_sc as plsc

POOL_WINDOWS = (2, 4, 8, 16)
HEAD_DIM = 128
MOBA_BLOCK = 256
MOBA_TOPK = 3
ROPE_THETA = 500000.0
ROPE_DIM = HEAD_DIM // 4
TOP_K_EXPERTS = 2
RMS_EPS = 1e-6
NEG_INF = -1e30
REMOVED = -3e38

LANES = 128
SUBLANES = 8
ROW_TILE = 512
POOL_SUB = 128
ATTN_HEADS_PER_STEP = 4
ATTN_BLOCKS_PER_TRIP = 4
SUM_ROWS = 16
LOG2E = 1.4426950408889634
EXPERT_FF_CHUNK = 1792
EXPERT_CAST_STEPS = 32
SC_ROW = 128
SC_WINDOW = 128
VMEM_LIMIT = 56 * 1024 * 1024

bf16 = jnp.bfloat16
f32 = jnp.float32


def _dot(a, b):
    return jnp.dot(a, b, preferred_element_type=f32)


def _dot_split(a, b):
    a_hi = a.astype(bf16)
    a_lo = (a - a_hi.astype(f32)).astype(bf16)
    b_hi = b.astype(bf16)
    b_lo = (b - b_hi.astype(f32)).astype(bf16)
    return _dot(a_hi, b_hi) + (_dot(a_lo, b_hi) + _dot(a_hi, b_lo))


def _pack_pairs(x):
    half = x.shape[1] // 2
    lo = lax.bitcast_convert_type(x[:, :half].astype(bf16).astype(f32), jnp.uint32)
    hi = lax.bitcast_convert_type(x[:, half:].astype(bf16).astype(f32), jnp.uint32)
    return (hi & jnp.uint32(0xFFFF0000)) | (lo >> 16)


def _unpack_pairs(w):
    lo = lax.bitcast_convert_type(w << 16, f32)
    hi = lax.bitcast_convert_type(w & jnp.uint32(0xFFFF0000), f32)
    return jnp.concatenate([lo, hi], axis=1)


def _cast_kernel(*refs):
    n = len(refs) // 2
    for src, dst in zip(refs[:n], refs[n:]):
        dst[...] = src[...].astype(bf16)


def _cast_bf16(ws, steps):
    flat = [w.reshape(-1, w.shape[-1]) for w in ws]
    for a in flat:
        assert a.shape[0] % steps == 0 and (a.shape[0] // steps) % (2 * SUBLANES) == 0
    specs = [pl.BlockSpec((a.shape[0] // steps, a.shape[1]), lambda s: (s, 0)) for a in flat]
    outs = pl.pallas_call(
        _cast_kernel,
        grid=(steps,),
        in_specs=specs,
        out_specs=specs,
        out_shape=[jax.ShapeDtypeStruct(a.shape, bf16) for a in flat],
        compiler_params=_params("parallel"),
        name="cast_bf16",
    )(*flat)
    return [o.reshape(w.shape) for o, w in zip(outs, ws)]


def _rms(x, g):
    var = jnp.mean(x * x, axis=-1, keepdims=True)
    return x * lax.rsqrt(var + RMS_EPS) * g


def _params(*sem):
    return pltpu.CompilerParams(dimension_semantics=sem, vmem_limit_bytes=VMEM_LIMIT)


def _const_spec(shape):
    nd = len(shape)
    return pl.BlockSpec(shape, lambda *_: (0,) * nd)


def _pool_kernel(x_ref, halo_ref, g_ref, pw_ref, ps_ref, after_ref, o_ref, pooled_ref, *, ts, seq):
    del after_ref
    i = pl.program_id(0)
    g = g_ref[...]
    x = x_ref[...]
    xn = _rms(x, g)
    keep = jnp.where((i * ts) % seq == 0, 0.0, 1.0)
    hnb = (_rms(halo_ref[...], g) * keep).astype(bf16)
    xnb = xn.astype(bf16)
    gd = x.shape[1] // len(POOL_WINDOWS)
    r = lax.broadcasted_iota(jnp.int32, (POOL_SUB, 2 * POOL_SUB), 0)
    c = lax.broadcasted_iota(jnp.int32, (POOL_SUB, 2 * POOL_SUB), 1)
    dist = r + POOL_SUB - c
    bands = [((dist >= 0) & (dist < w)).astype(bf16) for w in POOL_WINDOWS]
    rows = lax.broadcasted_iota(jnp.int32, (POOL_SUB, 1), 0)
    for sb in range(ts // POOL_SUB):
        lo, hi = sb * POOL_SUB, (sb + 1) * POOL_SUB
        prev = hnb if sb == 0 else xnb[lo - POOL_SUB:lo]
        ext = jnp.concatenate([prev, xnb[lo:hi]], axis=0)
        tpos = (i * ts + lo) % seq + rows
        for gi, w in enumerate(POOL_WINDOWS):
            cs = slice(gi * gd, (gi + 1) * gd)
            wsum = _dot(bands[gi], ext[:, cs])
            inv_cnt = 1.0 / jnp.minimum(tpos + 1, w).astype(f32)
            pooled_ref[lo:hi, cs] = (wsum * inv_cnt - xn[lo:hi, cs]).astype(bf16)
    for gi in range(len(POOL_WINDOWS)):
        cs = slice(gi * gd, (gi + 1) * gd)
        mixed = _dot(pooled_ref[:, cs], pw_ref[gi])
        o_ref[:, cs] = x[:, cs] + mixed * ps_ref[:, cs]


def _pool_layer(h, norm, pool_w, pool_scale, seq, after):
    t, d = h.shape
    ts = ROW_TILE
    ng = len(POOL_WINDOWS)
    gd = d // ng
    per = ts // POOL_SUB
    return pl.pallas_call(
        functools.partial(_pool_kernel, ts=ts, seq=seq),
        grid=(t // ts,),
        in_specs=[
            pl.BlockSpec((ts, d), lambda i: (i, 0)),
            pl.BlockSpec((POOL_SUB, d), lambda i: (jnp.maximum(i * per - 1, 0), 0)),
            _const_spec((1, d)),
            _const_spec((ng, gd, gd)),
            _const_spec((1, d)),
            _const_spec(after.shape),
        ],
        out_specs=pl.BlockSpec((ts, d), lambda i: (i, 0)),
        out_shape=jax.ShapeDtypeStruct((t, d), f32),
        scratch_shapes=[pltpu.VMEM((ts, d), bf16)],
        compiler_params=_params("parallel"),
        name="pool",
    )(h, h, norm.reshape(1, d), pool_w.astype(bf16), pool_scale.reshape(1, d), after)


def _swiglu_kernel(h_ref, g_ref, w1_ref, w3_ref, w2_ref, o_ref):
    x = h_ref[...]
    hn = _rms(x, g_ref[...]).astype(bf16)
    a = _dot(hn, w1_ref[...])
    b = _dot(hn, w3_ref[...])
    o_ref[...] = x + _dot((jax.nn.silu(a) * b).astype(bf16), w2_ref[...])


def _ff_chunk(ff, target):
    units = ff // LANES
    best = 1
    for k in range(1, units + 1):
        if units % k == 0 and k * LANES <= target:
            best = k
    return best * LANES


def _swiglu_layer(h, norm, w1, w3, w2):
    t, d = h.shape
    ff = w1.shape[1]
    tm = ROW_TILE
    w1r, w3r, w2r = w1.astype(bf16), w3.astype(bf16), w2.astype(bf16)
    resident = lambda shape: pl.BlockSpec(shape, lambda i: (0, 0), pipeline_mode=pl.Buffered(1))
    return pl.pallas_call(
        _swiglu_kernel,
        grid=(t // tm,),
        in_specs=[
            pl.BlockSpec((tm, d), lambda i: (i, 0)),
            _const_spec((1, d)),
            resident((d, ff)), resident((d, ff)), resident((ff, d)),
        ],
        out_specs=pl.BlockSpec((tm, d), lambda i: (i, 0)),
        out_shape=jax.ShapeDtypeStruct((t, d), f32),
        compiler_params=_params("parallel"),
        name="swiglu",
    )(h, norm.reshape(1, d), w1r, w3r, w2r)


def _rope_tables(seq):
    half = ROPE_DIM // 2
    inv_freq = jnp.float32(ROPE_THETA) ** (-(jnp.arange(0, ROPE_DIM, 2, dtype=f32) / ROPE_DIM))
    ang = jnp.arange(seq, dtype=f32)[:, None] * inv_freq[None, :]
    cos, sin = jnp.cos(ang), jnp.sin(ang)
    ones = jnp.ones((seq, HEAD_DIM - ROPE_DIM), f32)
    zeros = jnp.zeros((seq, HEAD_DIM - half), f32)
    cos_t = jnp.concatenate([cos, cos, ones], axis=1)
    up_t = jnp.concatenate([-sin, zeros], axis=1)
    dn_t = jnp.concatenate([jnp.zeros((seq, half), f32), sin, jnp.zeros((seq, HEAD_DIM - ROPE_DIM), f32)], axis=1)
    return cos_t, up_t, dn_t


def _rope(xh, cos_t, up_t, dn_t):
    half = ROPE_DIM // 2
    return (xh * cos_t + pltpu.roll(xh, HEAD_DIM - half, 1) * up_t + pltpu.roll(xh, half, 1) * dn_t)


def _ple_qkv_kernel(h_ref, p_ref, pn_ref, wg_ref, wp_ref, kvn_ref, wk_ref, wv_ref, an_ref, wq_ref,
                    cos_ref, up_ref, dn_ref, h_out, q_out, k_out, v_out, km_out):
    h = h_ref[...]
    hn = _rms(h, pn_ref[...]).astype(bf16)
    h3 = h + jax.nn.sigmoid(_dot(hn, wg_ref[...])) * _dot(p_ref[...].astype(bf16), wp_ref[...])
    h_out[...] = h3
    base = h3 * lax.rsqrt(jnp.mean(h3 * h3, axis=-1, keepdims=True) + RMS_EPS)
    kn = (base * kvn_ref[...]).astype(bf16)
    qn = (base * an_ref[...]).astype(bf16)
    v = _dot(kn, wv_ref[...])
    for bi in range(h.shape[0] // MOBA_BLOCK):
        v_out[bi] = v[bi * MOBA_BLOCK:(bi + 1) * MOBA_BLOCK].T.astype(bf16)
    k = _dot(kn, wk_ref[...])
    q = _dot(qn, wq_ref[...])
    cos_t, up_t, dn_t = cos_ref[...], up_ref[...], dn_ref[...]
    tm, d = h.shape
    sub = km_out.shape[0] // (tm // MOBA_BLOCK)
    for hh in range(d // HEAD_DIM):
        cs = slice(hh * HEAD_DIM, (hh + 1) * HEAD_DIM)
        q_out[:, cs] = _rope(q[:, cs], cos_t, up_t, dn_t)
        kr = _rope(k[:, cs], cos_t, up_t, dn_t)
        k_out[:, cs] = kr.astype(bf16)
        for bi in range(tm // MOBA_BLOCK):
            m = jnp.mean(kr[bi * MOBA_BLOCK:(bi + 1) * MOBA_BLOCK], axis=0, keepdims=True)
            km_out[bi * sub:(bi + 1) * sub, cs] = jnp.broadcast_to(m, (sub, HEAD_DIM))


def _layer_rows_spec(p_all, layer, tm):
    return pl.BlockSpec((None, tm, p_all.shape[2]), lambda i: (layer, i, 0))


def _ple_qkv_layer(h, p_all, layer, ple_norm, ple_gate, ple_proj, kv_norm, w_k, w_v, attn_norm, w_q, seq):
    t, d = h.shape
    pd = p_all.shape[2]
    tm = ROW_TILE
    sub = 8
    cos_t, up_t, dn_t = _rope_tables(seq)
    tiles_per_seq = seq // tm
    row = lambda i: (i, 0)
    tab = pl.BlockSpec((tm, HEAD_DIM), lambda i: (i % tiles_per_seq, 0))
    nkm = t // MOBA_BLOCK * sub
    outs = pl.pallas_call(
        _ple_qkv_kernel,
        grid=(t // tm,),
        in_specs=[
            pl.BlockSpec((tm, d), row), _layer_rows_spec(p_all, layer, tm),
            _const_spec((1, d)), _const_spec((d, d)), _const_spec((pd, d)),
            _const_spec((1, d)), _const_spec((d, d)), _const_spec((d, d)),
            _const_spec((1, d)), _const_spec((d, d)),
            tab, tab, tab,
        ],
        out_specs=[
            pl.BlockSpec((tm, d), row), pl.BlockSpec((tm, d), row),
            pl.BlockSpec((tm, d), row),
            pl.BlockSpec((tm // MOBA_BLOCK, d, MOBA_BLOCK), lambda i: (i, 0, 0)),
            pl.BlockSpec((tm // MOBA_BLOCK * sub, d), row),
        ],
        out_shape=[
            jax.ShapeDtypeStruct((t, d), f32), jax.ShapeDtypeStruct((t, d), f32),
            jax.ShapeDtypeStruct((t, d), bf16),
            jax.ShapeDtypeStruct((t // MOBA_BLOCK, d, MOBA_BLOCK), bf16),
            jax.ShapeDtypeStruct((nkm, d), f32),
        ],
        compiler_params=_params("parallel"),
        name="ple_qkv",
    )(h, p_all, ple_norm.reshape(1, d), ple_gate.astype(bf16), ple_proj.astype(bf16),
      kv_norm.reshape(1, d), w_k.astype(bf16), w_v.astype(bf16),
      attn_norm.reshape(1, d), w_q.astype(bf16), cos_t, up_t, dn_t)
    h3, q, k, v, km = outs
    return h3, q, k, v, km.reshape(t // MOBA_BLOCK, sub, d)[:, 0, :]


def _attn_kernel(q_ref, k_ref, vt_ref, km_ref, o_ref, qa_ref, sa_ref, sb_ref, pa_ref, pb_ref, acc_ref, *, hg):
    j = pl.program_id(2)
    nb = km_ref.shape[0]
    bs = q_ref.shape[0]
    hd = HEAD_DIM
    qscale = (hd ** -0.5) * LOG2E
    blk = lax.broadcasted_iota(jnp.int32, (nb, bs), 0).astype(f32)
    jf = j.astype(f32)
    krow = lax.broadcasted_iota(jnp.int32, (bs, bs), 0)
    qcol = lax.broadcasted_iota(jnp.int32, (bs, bs), 1)
    start_j = pl.multiple_of(j * bs, bs)

    ones_rows = jnp.ones((SUM_ROWS, bs), bf16)

    def weighted_values(n, h, pb):
        return _dot(jnp.concatenate([vt_ref[n, h * hd:(h + 1) * hd, :], ones_rows], axis=0), pb)

    init = []
    for h in range(hg):
        cs = slice(h * hd, (h + 1) * hd)
        qft = q_ref[:, cs].T
        gate = _dot_split(km_ref[:, cs], qft)
        cand = jnp.where(blk < jf, gate, NEG_INF)
        sel = blk < 0.0
        for _ in range(min(MOBA_TOPK, nb)):
            mx = jnp.max(cand, axis=0, keepdims=True)
            pick = jnp.min(jnp.where(cand == mx, blk, float(nb)), axis=0, keepdims=True)
            hit = blk == pick
            sel = sel | hit
            cand = jnp.where(hit, REMOVED, cand)
        bias = jnp.where(sel & (blk < jf), 0.0, NEG_INF)
        if nb < hd:
            bias = jnp.concatenate([bias, jnp.zeros((hd - nb, bs), f32)], axis=0)
        qbt = (qft * qscale).astype(bf16)
        qa_ref[h] = jnp.concatenate([qbt, bias.astype(bf16)], axis=0)

        s = _dot(k_ref[pl.ds(start_j, bs), cs], qbt)
        s = jnp.where(krow <= qcol, s, NEG_INF)
        m0 = jnp.max(s, axis=0, keepdims=True)
        acc_ref[h] = weighted_values(j, h, jnp.exp2((s - m0).astype(bf16)))
        init += [m0, jnp.ones_like(m0)]

    lane = lax.broadcasted_iota(jnp.int32, (bs, hd), 1)

    def block_scores(n, h):
        start = pl.multiple_of(n * bs, bs)
        onehot = (lane == n).astype(bf16)
        ka = jnp.concatenate([k_ref[pl.ds(start, bs), h * hd:(h + 1) * hd], onehot], axis=1)
        return _dot(ka, qa_ref[h])

    def stage(n, s_in, s_out, p_prev, p_out, stats):
        nn = jnp.minimum(n + 1, nb - 1)
        prev = jnp.clip(n - 1, 0, nb - 1)
        new = []
        for h in range(hg):
            m, alpha_p = stats[2 * h:2 * h + 2]
            acc_ref[h] = alpha_p * acc_ref[h] + weighted_values(prev, h, p_prev[h])
            s = s_in[h]
            m_new = jnp.maximum(m, jnp.max(s, axis=0, keepdims=True))
            p_out[h] = jnp.exp2((s - m_new).astype(bf16))
            s_out[h] = block_scores(nn, h)
            new += [m_new, jnp.exp2(m - m_new)]
        return new

    per_trip = ATTN_BLOCKS_PER_TRIP

    def body(i, stats):
        for u in range(0, per_trip, 2):
            stats = stage(per_trip * i + u, sa_ref, sb_ref, pb_ref, pa_ref, stats)
            stats = stage(per_trip * i + u + 1, sb_ref, sa_ref, pa_ref, pb_ref, stats)
        return tuple(stats)

    for h in range(hg):
        sa_ref[h] = block_scores(0, h)
        pb_ref[h] = jnp.zeros((bs, bs), bf16)
    trips = (j + per_trip - 1) // per_trip
    res = lax.fori_loop(0, trips, body, tuple(init))
    last = jnp.clip(per_trip * trips - 1, 0, nb - 1)
    for h in range(hg):
        alpha_p = res[2 * h + 1]
        acc = alpha_p * acc_ref[h] + weighted_values(last, h, pb_ref[h])
        o_ref[:, h * hd:(h + 1) * hd] = (acc[:hd] * (1.0 / acc[hd:hd + 1])).T.astype(o_ref.dtype)


def _attention(q, k, vt, km, batch, seq):
    t, d = q.shape
    nh = d // HEAD_DIM
    nb = seq // MOBA_BLOCK
    hg = ATTN_HEADS_PER_STEP
    assert nb <= HEAD_DIM and nh % hg == 0
    w = hg * HEAD_DIM
    return pl.pallas_call(
        functools.partial(_attn_kernel, hg=hg),
        grid=(batch, nh // hg, nb),
        in_specs=[
            pl.BlockSpec((MOBA_BLOCK, w), lambda b, g, j: (b * nb + j, g)),
            pl.BlockSpec((seq, w), lambda b, g, j: (b, g), pipeline_mode=pl.Buffered(1)),
            pl.BlockSpec((nb, w, MOBA_BLOCK), lambda b, g, j: (b, g, 0), pipeline_mode=pl.Buffered(1)),
            pl.BlockSpec((nb, w), lambda b, g, j: (b, g)),
        ],
        out_specs=pl.BlockSpec((MOBA_BLOCK, w), lambda b, g, j: (b * nb + j, g)),
        out_shape=jax.ShapeDtypeStruct((t, d), bf16),
        scratch_shapes=[
            pltpu.VMEM((hg, 2 * HEAD_DIM, MOBA_BLOCK), bf16),
            pltpu.VMEM((hg, MOBA_BLOCK, MOBA_BLOCK), f32), pltpu.VMEM((hg, MOBA_BLOCK, MOBA_BLOCK), f32),
            pltpu.VMEM((hg, MOBA_BLOCK, MOBA_BLOCK), bf16), pltpu.VMEM((hg, MOBA_BLOCK, MOBA_BLOCK), bf16),
            pltpu.VMEM((hg, HEAD_DIM + SUM_ROWS, MOBA_BLOCK), f32),
        ],
        compiler_params=_params("parallel", "parallel", "arbitrary"),
        name="attn",
    )(q, k, vt, km)


def _oproj_router_kernel(h_ref, a_ref, wo_ref, g_ref, r_ref, h_out, hn_out, route_out, route_t_out, *, n_exp):
    h4 = h_ref[...] + _dot(a_ref[...], wo_ref[...])
    h_out[...] = h4
    hn = _rms(h4, g_ref[...])
    hn_out[...] = _pack_pairs(hn)
    logits = _dot_split(hn, r_ref[...])
    lane = lax.broadcasted_iota(jnp.int32, logits.shape, 1).astype(f32)
    cand = jnp.where(lane < n_exp, logits, NEG_INF)
    m1 = jnp.max(cand, axis=1, keepdims=True)
    i1 = jnp.min(jnp.where(cand == m1, lane, float(LANES)), axis=1, keepdims=True)
    cand = jnp.where(lane == i1, REMOVED, cand)
    m2 = jnp.max(cand, axis=1, keepdims=True)
    i2 = jnp.min(jnp.where(cand == m2, lane, float(LANES)), axis=1, keepdims=True)
    e2 = jnp.exp(m2 - m1)
    den = 1.0 + e2
    route = jnp.where(lane == 0, i1, jnp.where(lane == 1, i2, jnp.where(
        lane == 2, 1.0 / den, jnp.where(lane == 3, e2 / den, 0.0))))
    route_out[...] = route
    route_t_out[...] = route.T[:SUBLANES]


def _oproj_router_layer(h, attn, w_o, norm, router):
    t, d = h.shape
    tm = ROW_TILE
    n_exp = router.shape[1]
    r_pad = jnp.zeros((d, LANES), f32).at[:, :n_exp].set(router)
    row = lambda i: (i, 0)
    return pl.pallas_call(
        functools.partial(_oproj_router_kernel, n_exp=n_exp),
        grid=(t // tm,),
        in_specs=[pl.BlockSpec((tm, d), row), pl.BlockSpec((tm, d), row), _const_spec((d, d)),
                  _const_spec((1, d)), _const_spec((d, LANES))],
        out_specs=[pl.BlockSpec((tm, d), row), pl.BlockSpec((tm, d // 2), row), pl.BlockSpec((tm, LANES), row),
                   pl.BlockSpec((SUBLANES, tm), lambda i: (0, i))],
        out_shape=[jax.ShapeDtypeStruct((t, d), f32), jax.ShapeDtypeStruct((t, d // 2), jnp.uint32),
                   jax.ShapeDtypeStruct((t, LANES), f32), jax.ShapeDtypeStruct((SUBLANES, t), f32)],
        compiler_params=_params("parallel"),
        name="oproj_router",
    )(h, attn, w_o.astype(bf16), norm.reshape(1, d), r_pad)


def _sc_gather(x, idx):
    n = idx.shape[0]
    assert x.shape[1] == SC_ROW and n % SC_WINDOW == 0
    mesh = plsc.VectorSubcoreMesh(core_axis_name="core", subcore_axis_name="subcore")

    @pl.kernel(out_type=jax.ShapeDtypeStruct((n, SC_ROW), x.dtype), mesh=mesh, scratch_types=[])
    def gather(x_hbm, i_hbm, o_hbm):
        def body(i_vmem, o_vmem):
            pltpu.sync_copy(x_hbm.at[i_vmem.at[0]], o_vmem)

        pltpu.emit_pipeline(
            body,
            grid=(n // SC_WINDOW,),
            in_specs=[pl.BlockSpec((1, SC_WINDOW), index_map=lambda i: (0, i))],
            out_specs=[pl.BlockSpec((SC_WINDOW, SC_ROW), index_map=lambda i: (i, 0))],
            core_axis_name=("core", "subcore"),
            dimension_semantics=(pltpu.PARALLEL,),
        )(i_hbm, o_hbm)

    return gather(x, idx.reshape(n // LANES, LANES).reshape(1, n))


def _pieces(a):
    n, d = a.shape
    return a.reshape(n // SUBLANES, SUBLANES, d // SC_ROW, SC_ROW).transpose(0, 2, 1, 3).reshape(-1, SC_ROW)


def _unpieces(p, d):
    per = d // SC_ROW
    n = p.shape[0] // per
    return p.reshape(n // SUBLANES, per, SUBLANES, SC_ROW).transpose(0, 2, 1, 3).reshape(n, d)


def _gather_tokens(x, rows):
    n, d = x.shape
    m = rows[0].shape[0]
    per = d // SC_ROW
    assert n % SUBLANES == 0 and m % SUBLANES == 0
    tiled = []
    for r in rows:
        first = (r // SUBLANES * (SUBLANES * per) + r % SUBLANES).reshape(m // SUBLANES, SUBLANES)
        tiled.append(jnp.tile(first, (1, per)))
    chunk = jnp.arange(per * SUBLANES, dtype=jnp.int32) // SUBLANES * SUBLANES
    idx = jnp.concatenate(tiled, axis=1) + jnp.tile(chunk, len(rows))[None, :]
    return _unpieces(_sc_gather(_pieces(x), idx.reshape(-1)), len(rows) * d)


def _expert_kernel(te_ref, tv_ref, x_ref, w1_ref, w3_ref, w2_ref, o_ref, xb_ref, acc_ref, *, nf):
    i = pl.program_id(0)
    f = pl.program_id(1)
    valid = tv_ref[i] > 0

    @pl.when(f == 0)
    def _():
        xb_ref[...] = _unpack_pairs(x_ref[...]).astype(bf16)
        acc_ref[...] = jnp.zeros_like(acc_ref)

    @pl.when(valid)
    def _():
        x = xb_ref[...]
        a = _dot(x, w1_ref[...])
        b = _dot(x, w3_ref[...])
        acc_ref[...] += _dot((jax.nn.silu(a) * b).astype(bf16), w2_ref[...])

    @pl.when(f == nf - 1)
    def _():
        o_ref[...] = _pack_pairs(acc_ref[...])


def _route_tables(choices, n_exp, tm):
    t = choices[0].shape[0]
    nslots = len(choices) * t
    experts = jnp.arange(n_exp, dtype=jnp.int32)[:, None]
    onehots = [(c[None, :] == experts).astype(jnp.int32) for c in choices]
    cums = [jnp.cumsum(oh, axis=1) for oh in onehots]
    totals = [cu[:, -1] for cu in cums]
    counts = sum(totals)
    tiles_e = (counts + tm - 1) // tm
    tile_end = jnp.cumsum(tiles_e)
    tile_start = tile_end - tiles_e
    group_start = jnp.cumsum(counts) - counts
    nt = nslots // tm + n_exp
    total = tile_end[-1]
    ti = jnp.arange(nt, dtype=jnp.int32)
    tv = (ti < total).astype(jnp.int32)
    tc = jnp.minimum(ti, total - 1)
    te = jnp.minimum(jnp.sum((tc[:, None] >= tile_end[None, :]).astype(jnp.int32), axis=1), n_exp - 1)
    order = jnp.argsort(jnp.concatenate(choices), stable=True).astype(jnp.int32)
    lane = jnp.arange(tm, dtype=jnp.int32)[None, :]
    rank = ((tc - tile_start[te]) * tm)[:, None] + lane
    real = (rank < counts[te][:, None]) & (tv[:, None] > 0)
    sidx = jnp.clip(group_start[te][:, None] + rank, 0, nslots - 1)
    src = jnp.where(real, order[sidx] % t, (ti[:, None] * tm + lane) % t).reshape(-1)
    pos = []
    earlier = jnp.zeros((n_exp,), jnp.int32)
    for oh, cu, tot in zip(onehots, cums, totals):
        pos.append(jnp.sum(oh * ((tile_start * tm + earlier)[:, None] + cu - oh), axis=0))
        earlier = earlier + tot
    return te, tv, src, pos, nt


def _experts_layer(hn, choices, w1, w3, w2):
    n_exp, d, ff = w1.shape
    tm = ROW_TILE
    fc = _ff_chunk(ff, EXPERT_FF_CHUNK)
    nf = ff // fc
    te, tv, src, pos, nt = _route_tables(choices, n_exp, tm)
    xs = _gather_tokens(hn, [src])
    w1r, w3r, w2r = w1, w3, w2

    def fsel(i, f, tv_r):
        return jnp.where(tv_r[i] > 0, f, nf - 1)

    def wmap_in(i, f, te_r, tv_r):
        return (te_r[i], 0, fsel(i, f, tv_r))

    def wmap_out(i, f, te_r, tv_r):
        return (te_r[i], fsel(i, f, tv_r), 0)

    grid_spec = pltpu.PrefetchScalarGridSpec(
        num_scalar_prefetch=2,
        grid=(nt, nf),
        in_specs=[
            pl.BlockSpec((tm, d // 2), lambda i, f, *_: (i, 0)),
            pl.BlockSpec((None, d, fc), wmap_in),
            pl.BlockSpec((None, d, fc), wmap_in),
            pl.BlockSpec((None, fc, d), wmap_out),
        ],
        out_specs=pl.BlockSpec((tm, d // 2), lambda i, f, *_: (i, 0)),
        scratch_shapes=[pltpu.VMEM((tm, d), bf16), pltpu.VMEM((tm, d), f32)],
    )
    ys = pl.pallas_call(
        functools.partial(_expert_kernel, nf=nf),
        grid_spec=grid_spec,
        out_shape=jax.ShapeDtypeStruct((nt * tm, d // 2), jnp.uint32),
        compiler_params=_params("parallel", "arbitrary"),
        name="experts",
    )(te, tv, xs, w1r, w3r, w2r)
    return _gather_tokens(ys, pos)


def _final_kernel(h_ref, y_ref, route_ref, p_ref, pn_ref, wg_ref, wp_ref, fn_ref, o_ref):
    d = h_ref.shape[1]
    route = route_ref[...]
    y = y_ref[...]
    h5 = h_ref[...] + route[:, 2:3] * _unpack_pairs(y[:, :d // 2]) + route[:, 3:4] * _unpack_pairs(y[:, d // 2:])
    hn = _rms(h5, pn_ref[...]).astype(bf16)
    h6 = h5 + jax.nn.sigmoid(_dot(hn, wg_ref[...])) * _dot(p_ref[...].astype(bf16), wp_ref[...])
    o_ref[...] = _rms(h6, fn_ref[...])


def _final_layer(h, y2, route, p_all, layer, ple_norm, ple_gate, ple_proj, final_norm):
    t, d = h.shape
    pd = p_all.shape[2]
    tm = ROW_TILE
    row = lambda i: (i, 0)
    return pl.pallas_call(
        _final_kernel,
        grid=(t // tm,),
        in_specs=[pl.BlockSpec((tm, d), row), pl.BlockSpec((tm, y2.shape[1]), row),
                  pl.BlockSpec((tm, LANES), row), _layer_rows_spec(p_all, layer, tm),
                  _const_spec((1, d)), _const_spec((d, d)), _const_spec((pd, d)), _const_spec((1, d))],
        out_specs=pl.BlockSpec((tm, d), row),
        out_shape=jax.ShapeDtypeStruct((t, d), f32),
        compiler_params=_params("parallel"),
        name="final",
    )(h, y2, route, p_all, ple_norm.reshape(1, d), ple_gate.astype(bf16), ple_proj.astype(bf16),
      final_norm.reshape(1, d))


def kernel(x, p, pool_norm, pool_w, pool_scale, kv_norm, w_k, w_v, attn_norm, w_q, w_o, ffn_norm, ffn_w1, ffn_w3, ffn_w2, router, exp_w1, exp_w3, exp_w2, ple_norm, ple_gate, ple_proj, final_norm):
    batch, seq, d = x.shape
    t = batch * seq
    assert seq % ROW_TILE == 0 and ROW_TILE % MOBA_BLOCK == 0 and d % HEAD_DIM == 0
    assert p.shape[0] == 2 and router.shape[2] <= LANES
    h = x.reshape(t, d)
    pf = p.reshape(p.shape[0], t, p.shape[-1])

    ew1, ew3, ew2 = _cast_bf16([exp_w1[0], exp_w3[0], exp_w2[0]], EXPERT_CAST_STEPS)
    h = _pool_layer(h, pool_norm[0], pool_w[0], pool_scale[0], seq, ew2[0, :2 * SUBLANES, :LANES])
    h = _swiglu_layer(h, ffn_norm[0], ffn_w1[0], ffn_w3[0], ffn_w2[0])
    h, q, k, v, km = _ple_qkv_layer(h, pf, 0, ple_norm[0], ple_gate[0], ple_proj[0],
                                    kv_norm, w_k, w_v, attn_norm[0], w_q[0], seq)
    attn = _attention(q, k, v, km, batch, seq)
    h, hn, route, route_t = _oproj_router_layer(h, attn, w_o[0], ffn_norm[1], router[0])
    choices = [route_t[k].astype(jnp.int32) for k in range(TOP_K_EXPERTS)]
    y = _experts_layer(hn, choices, ew1, ew3, ew2)
    out = _final_layer(h, y, route, pf, 1, ple_norm[1], ple_gate[1],
                       ple_proj[1], final_norm)
    return out.reshape(batch, seq, d)
```

```python
import functools

import jax
import jax.numpy as jnp
from jax import lax
from jax.experimental import pallas as pl
from jax.experimental.pallas import tpu as pltpu
from jax.experimental.pallas import tpu_sc as plsc

POOL_WINDOWS = (2, 4, 8, 16)
HEAD_DIM = 128
MOBA_BLOCK = 256
MOBA_TOPK = 3
ROPE_THETA = 500000.0
ROPE_DIM = HEAD_DIM // 4
TOP_K_EXPERTS = 2
RMS_EPS = 1e-6
NEG_INF = -1e30
REMOVED = -3e38

LANES = 128
SUBLANES = 8
ROW_TILE = 512
POOL_SUB = 128
ATTN_HEADS_PER_STEP = 4
ATTN_BLOCKS_PER_TRIP = 4
SUM_ROWS = 16
LOG2E = 1.4426950408889634
EXPERT_FF_CHUNK = 1792
EXPERT_CAST_STEPS = 32
SC_OVERLAP_PARTS = 2
SC_ROW = 128
SC_WINDOW = 256
VMEM_LIMIT = 56 * 1024 * 1024

bf16 = jnp.bfloat16
f32 = jnp.float32


def _dot(a, b):
    return jnp.dot(a, b, preferred_element_type=f32)


def _dot_split(a, b):
    a_hi = a.astype(bf16)
    a_lo = (a - a_hi.astype(f32)).astype(bf16)
    b_hi = b.astype(bf16)
    b_lo = (b - b_hi.astype(f32)).astype(bf16)
    return _dot(a_hi, b_hi) + (_dot(a_lo, b_hi) + _dot(a_hi, b_lo))


def _pack_pairs(x):
    half = x.shape[1] // 2
    lo = lax.bitcast_convert_type(x[:, :half].astype(bf16).astype(f32), jnp.uint32)
    hi = lax.bitcast_convert_type(x[:, half:].astype(bf16).astype(f32), jnp.uint32)
    return (hi & jnp.uint32(0xFFFF0000)) | (lo >> 16)


def _unpack_pairs(w):
    lo = lax.bitcast_convert_type(w << 16, f32)
    hi = lax.bitcast_convert_type(w & jnp.uint32(0xFFFF0000), f32)
    return jnp.concatenate([lo, hi], axis=1)


def _cast_kernel(*refs):
    n = len(refs) // 2
    for src, dst in zip(refs[:n], refs[n:]):
        dst[...] = src[...].astype(bf16)


def _cast_bf16(ws, steps):
    flat = [w.reshape(-1, w.shape[-1]) for w in ws]
    for a in flat:
        assert a.shape[0] % steps == 0 and (a.shape[0] // steps) % (2 * SUBLANES) == 0
    specs = [pl.BlockSpec((a.shape[0] // steps, a.shape[1]), lambda s: (s, 0)) for a in flat]
    outs = pl.pallas_call(
        _cast_kernel,
        grid=(steps,),
        in_specs=specs,
        out_specs=specs,
        out_shape=[jax.ShapeDtypeStruct(a.shape, bf16) for a in flat],
        compiler_params=_params("parallel"),
        name="cast_bf16",
    )(*flat)
    return [o.reshape(w.shape) for o, w in zip(outs, ws)]


def _rms(x, g):
    var = jnp.mean(x * x, axis=-1, keepdims=True)
    return x * lax.rsqrt(var + RMS_EPS) * g


def _params(*sem):
    return pltpu.CompilerParams(dimension_semantics=sem, vmem_limit_bytes=VMEM_LIMIT)


def _const_spec(shape):
    nd = len(shape)
    return pl.BlockSpec(shape, lambda *_: (0,) * nd)


def _pool_kernel(x_ref, halo_ref, g_ref, pw_ref, ps_ref, after_ref, o_ref, pooled_ref, *, ts, seq):
    del after_ref
    i = pl.program_id(0)
    g = g_ref[...]
    x = x_ref[...]
    xn = _rms(x, g)
    keep = jnp.where((i * ts) % seq == 0, 0.0, 1.0)
    hnb = (_rms(halo_ref[...], g) * keep).astype(bf16)
    xnb = xn.astype(bf16)
    gd = x.shape[1] // len(POOL_WINDOWS)
    r = lax.broadcasted_iota(jnp.int32, (POOL_SUB, 2 * POOL_SUB), 0)
    c = lax.broadcasted_iota(jnp.int32, (POOL_SUB, 2 * POOL_SUB), 1)
    dist = r + POOL_SUB - c
    bands = [((dist >= 0) & (dist < w)).astype(bf16) for w in POOL_WINDOWS]
    rows = lax.broadcasted_iota(jnp.int32, (POOL_SUB, 1), 0)
    for sb in range(ts // POOL_SUB):
        lo, hi = sb * POOL_SUB, (sb + 1) * POOL_SUB
        prev = hnb if sb == 0 else xnb[lo - POOL_SUB:lo]
        ext = jnp.concatenate([prev, xnb[lo:hi]], axis=0)
        tpos = (i * ts + lo) % seq + rows
        for gi, w in enumerate(POOL_WINDOWS):
            cs = slice(gi * gd, (gi + 1) * gd)
            wsum = _dot(bands[gi], ext[:, cs])
            inv_cnt = 1.0 / jnp.minimum(tpos + 1, w).astype(f32)
            pooled_ref[lo:hi, cs] = (wsum * inv_cnt - xn[lo:hi, cs]).astype(bf16)
    for gi in range(len(POOL_WINDOWS)):
        cs = slice(gi * gd, (gi + 1) * gd)
        mixed = _dot(pooled_ref[:, cs], pw_ref[gi])
        o_ref[:, cs] = x[:, cs] + mixed * ps_ref[:, cs]


def _pool_layer(h, norm, pool_w, pool_scale, seq, after):
    t, d = h.shape
    ts = ROW_TILE
    ng = len(POOL_WINDOWS)
    gd = d // ng
    per = ts // POOL_SUB
    return pl.pallas_call(
        functools.partial(_pool_kernel, ts=ts, seq=seq),
        grid=(t // ts,),
        in_specs=[
            pl.BlockSpec((ts, d), lambda i: (i, 0)),
            pl.BlockSpec((POOL_SUB, d), lambda i: (jnp.maximum(i * per - 1, 0), 0)),
            _const_spec((1, d)),
            _const_spec((ng, gd, gd)),
            _const_spec((1, d)),
            _const_spec(after.shape),
        ],
        out_specs=pl.BlockSpec((ts, d), lambda i: (i, 0)),
        out_shape=jax.ShapeDtypeStruct((t, d), f32),
        scratch_shapes=[pltpu.VMEM((ts, d), bf16)],
        compiler_params=_params("parallel"),
        name="pool",
    )(h, h, norm.reshape(1, d), pool_w.astype(bf16), pool_scale.reshape(1, d), after)


def _swiglu_kernel(h_ref, g_ref, w1_ref, w3_ref, w2_ref, o_ref):
    x = h_ref[...]
    hn = _rms(x, g_ref[...]).astype(bf16)
    a = _dot(hn, w1_ref[...])
    b = _dot(hn, w3_ref[...])
    o_ref[...] = x + _dot((jax.nn.silu(a) * b).astype(bf16), w2_ref[...])


def _ff_chunk(ff, target):
    units = ff // LANES
    best = 1
    for k in range(1, units + 1):
        if units % k == 0 and k * LANES <= target:
            best = k
    return best * LANES


def _swiglu_layer(h, norm, w1, w3, w2):
    t, d = h.shape
    ff = w1.shape[1]
    tm = ROW_TILE
    w1r, w3r, w2r = w1.astype(bf16), w3.astype(bf16), w2.astype(bf16)
    resident = lambda shape: pl.BlockSpec(shape, lambda i: (0, 0), pipeline_mode=pl.Buffered(1))
    return pl.pallas_call(
        _swiglu_kernel,
        grid=(t // tm,),
        in_specs=[
            pl.BlockSpec((tm, d), lambda i: (i, 0)),
            _const_spec((1, d)),
            resident((d, ff)), resident((d, ff)), resident((ff, d)),
        ],
        out_specs=pl.BlockSpec((tm, d), lambda i: (i, 0)),
        out_shape=jax.ShapeDtypeStruct((t, d), f32),
        compiler_params=_params("parallel"),
        name="swiglu",
    )(h, norm.reshape(1, d), w1r, w3r, w2r)


def _rope_tables(seq):
    half = ROPE_DIM // 2
    inv_freq = jnp.float32(ROPE_THETA) ** (-(jnp.arange(0, ROPE_DIM, 2, dtype=f32) / ROPE_DIM))
    ang = jnp.arange(seq, dtype=f32)[:, None] * inv_freq[None, :]
    cos, sin = jnp.cos(ang), jnp.sin(ang)
    ones = jnp.ones((seq, HEAD_DIM - ROPE_DIM), f32)
    zeros = jnp.zeros((seq, HEAD_DIM - half), f32)
    cos_t = jnp.concatenate([cos, cos, ones], axis=1)
    up_t = jnp.concatenate([-sin, zeros], axis=1)
    dn_t = jnp.concatenate([jnp.zeros((seq, half), f32), sin, jnp.zeros((seq, HEAD_DIM - ROPE_DIM), f32)], axis=1)
    return cos_t, up_t, dn_t


def _rope(xh, cos_t, up_t, dn_t):
    half = ROPE_DIM // 2
    return (xh * cos_t + pltpu.roll(xh, HEAD_DIM - half, 1) * up_t + pltpu.roll(xh, half, 1) * dn_t)


def _ple_qkv_kernel(h_ref, p_ref, pn_ref, wg_ref, wp_ref, kvn_ref, wk_ref, wv_ref, an_ref, wq_ref,
                    cos_ref, up_ref, dn_ref, h_out, q_out, k_out, v_out, km_out):
    h = h_ref[...]
    hn = _rms(h, pn_ref[...]).astype(bf16)
    h3 = h + jax.nn.sigmoid(_dot(hn, wg_ref[...])) * _dot(p_ref[...].astype(bf16), wp_ref[...])
    h_out[...] = h3
    base = h3 * lax.rsqrt(jnp.mean(h3 * h3, axis=-1, keepdims=True) + RMS_EPS)
    kn = (base * kvn_ref[...]).astype(bf16)
    qn = (base * an_ref[...]).astype(bf16)
    v = _dot(kn, wv_ref[...])
    for bi in range(h.shape[0] // MOBA_BLOCK):
        v_out[bi] = v[bi * MOBA_BLOCK:(bi + 1) * MOBA_BLOCK].T.astype(bf16)
    k = _dot(kn, wk_ref[...])
    q = _dot(qn, wq_ref[...])
    cos_t, up_t, dn_t = cos_ref[...], up_ref[...], dn_ref[...]
    tm, d = h.shape
    sub = km_out.shape[0] // (tm // MOBA_BLOCK)
    for hh in range(d // HEAD_DIM):
        cs = slice(hh * HEAD_DIM, (hh + 1) * HEAD_DIM)
        q_out[:, cs] = _rope(q[:, cs], cos_t, up_t, dn_t)
        kr = _rope(k[:, cs], cos_t, up_t, dn_t)
        k_out[:, cs] = kr.astype(bf16)
        for bi in range(tm // MOBA_BLOCK):
            m = jnp.mean(kr[bi * MOBA_BLOCK:(bi + 1) * MOBA_BLOCK], axis=0, keepdims=True)
            km_out[bi * sub:(bi + 1) * sub, cs] = jnp.broadcast_to(m, (sub, HEAD_DIM))


def _layer_rows_spec(p_all, layer, tm):
    return pl.BlockSpec((None, tm, p_all.shape[2]), lambda i: (layer, i, 0))


def _ple_qkv_layer(h, p_all, layer, ple_norm, ple_gate, ple_proj, kv_norm, w_k, w_v, attn_norm, w_q, seq):
    t, d = h.shape
    pd = p_all.shape[2]
    tm = ROW_TILE
    sub = 8
    cos_t, up_t, dn_t = _rope_tables(seq)
    tiles_per_seq = seq // tm
    row = lambda i: (i, 0)
    tab = pl.BlockSpec((tm, HEAD_DIM), lambda i: (i % tiles_per_seq, 0))
    nkm = t // MOBA_BLOCK * sub
    outs = pl.pallas_call(
        _ple_qkv_kernel,
        grid=(t // tm,),
        in_specs=[
            pl.BlockSpec((tm, d), row), _layer_rows_spec(p_all, layer, tm),
            _const_spec((1, d)), _const_spec((d, d)), _const_spec((pd, d)),
            _const_spec((1, d)), _const_spec((d, d)), _const_spec((d, d)),
            _const_spec((1, d)), _const_spec((d, d)),
            tab, tab, tab,
        ],
        out_specs=[
            pl.BlockSpec((tm, d), row), pl.BlockSpec((tm, d), row),
            pl.BlockSpec((tm, d), row),
            pl.BlockSpec((tm // MOBA_BLOCK, d, MOBA_BLOCK), lambda i: (i, 0, 0)),
            pl.BlockSpec((tm // MOBA_BLOCK * sub, d), row),
        ],
        out_shape=[
            jax.ShapeDtypeStruct((t, d), f32), jax.ShapeDtypeStruct((t, d), f32),
            jax.ShapeDtypeStruct((t, d), bf16),
            jax.ShapeDtypeStruct((t // MOBA_BLOCK, d, MOBA_BLOCK), bf16),
            jax.ShapeDtypeStruct((nkm, d), f32),
        ],
        compiler_params=_params("parallel"),
        name="ple_qkv",
    )(h, p_all, ple_norm.reshape(1, d), ple_gate.astype(bf16), ple_proj.astype(bf16),
      kv_norm.reshape(1, d), w_k.astype(bf16), w_v.astype(bf16),
      attn_norm.reshape(1, d), w_q.astype(bf16), cos_t, up_t, dn_t)
    h3, q, k, v, km = outs
    return h3, q, k, v, km.reshape(t // MOBA_BLOCK, sub, d)[:, 0, :]


def _attn_kernel(q_ref, k_ref, vt_ref, km_ref, o_ref, qa_ref, sa_ref, sb_ref, pa_ref, pb_ref, acc_ref, *, hg):
    j = pl.program_id(2)
    nb = km_ref.shape[0]
    bs = q_ref.shape[0]
    hd = HEAD_DIM
    qscale = (hd ** -0.5) * LOG2E
    blk = lax.broadcasted_iota(jnp.int32, (nb, bs), 0).astype(f32)
    jf = j.astype(f32)
    krow = lax.broadcasted_iota(jnp.int32, (bs, bs), 0)
    qcol = lax.broadcasted_iota(jnp.int32, (bs, bs), 1)
    start_j = pl.multiple_of(j * bs, bs)

    ones_rows = jnp.ones((SUM_ROWS, bs), bf16)

    def weighted_values(n, h, pb):
        return _dot(jnp.concatenate([vt_ref[n, h * hd:(h + 1) * hd, :], ones_rows], axis=0), pb)

    init = []
    for h in range(hg):
        cs = slice(h * hd, (h + 1) * hd)
        qft = q_ref[:, cs].T
        gate = _dot_split(km_ref[:, cs], qft)
        cand = jnp.where(blk < jf, gate, NEG_INF)
        sel = blk < 0.0
        for _ in range(min(MOBA_TOPK, nb)):
            mx = jnp.max(cand, axis=0, keepdims=True)
            pick = jnp.min(jnp.where(cand == mx, blk, float(nb)), axis=0, keepdims=True)
            hit = blk == pick
            sel = sel | hit
            cand = jnp.where(hit, REMOVED, cand)
        bias = jnp.where(sel & (blk < jf), 0.0, NEG_INF)
        if nb < hd:
            bias = jnp.concatenate([bias, jnp.zeros((hd - nb, bs), f32)], axis=0)
        qbt = (qft * qscale).astype(bf16)
        qa_ref[h] = jnp.concatenate([qbt, bias.astype(bf16)], axis=0)

        s = _dot(k_ref[pl.ds(start_j, bs), cs], qbt)
        s = jnp.where(krow <= qcol, s, NEG_INF)
        m0 = jnp.max(s, axis=0, keepdims=True)
        acc_ref[h] = weighted_values(j, h, jnp.exp2((s - m0).astype(bf16)))
        init += [m0, jnp.ones_like(m0)]

    lane = lax.broadcasted_iota(jnp.int32, (bs, hd), 1)

    def block_scores(n, h):
        start = pl.multiple_of(n * bs, bs)
        onehot = (lane == n).astype(bf16)
        ka = jnp.concatenate([k_ref[pl.ds(start, bs), h * hd:(h + 1) * hd], onehot], axis=1)
        return _dot(ka, qa_ref[h])

    def stage(n, s_in, s_out, p_prev, p_out, stats):
        nn = jnp.minimum(n + 1, nb - 1)
        prev = jnp.clip(n - 1, 0, nb - 1)
        new = []
        for h in range(hg):
            m, alpha_p = stats[2 * h:2 * h + 2]
            acc_ref[h] = alpha_p * acc_ref[h] + weighted_values(prev, h, p_prev[h])
            s = s_in[h]
            m_new = jnp.maximum(m, jnp.max(s, axis=0, keepdims=True))
            p_out[h] = jnp.exp2((s - m_new).astype(bf16))
            s_out[h] = block_scores(nn, h)
            new += [m_new, jnp.exp2(m - m_new)]
        return new

    per_trip = ATTN_BLOCKS_PER_TRIP

    def body(i, stats):
        for u in range(0, per_trip, 2):
            stats = stage(per_trip * i + u, sa_ref, sb_ref, pb_ref, pa_ref, stats)
            stats = stage(per_trip * i + u + 1, sb_ref, sa_ref, pa_ref, pb_ref, stats)
        return tuple(stats)

    for h in range(hg):
        sa_ref[h] = block_scores(0, h)
        pb_ref[h] = jnp.zeros((bs, bs), bf16)
    trips = (j + per_trip - 1) // per_trip
    res = lax.fori_loop(0, trips, body, tuple(init))
    last = jnp.clip(per_trip * trips - 1, 0, nb - 1)
    for h in range(hg):
        alpha_p = res[2 * h + 1]
        acc = alpha_p * acc_ref[h] + weighted_values(last, h, pb_ref[h])
        o_ref[:, h * hd:(h + 1) * hd] = (acc[:hd] * (1.0 / acc[hd:hd + 1])).T.astype(o_ref.dtype)


def _attention(q, k, vt, km, batch, seq):
    t, d = q.shape
    nh = d // HEAD_DIM
    nb = seq // MOBA_BLOCK
    hg = ATTN_HEADS_PER_STEP
    assert nb <= HEAD_DIM and nh % hg == 0
    w = hg * HEAD_DIM
    return pl.pallas_call(
        functools.partial(_attn_kernel, hg=hg),
        grid=(batch, nh // hg, nb),
        in_specs=[
            pl.BlockSpec((MOBA_BLOCK, w), lambda b, g, j: (b * nb + j, g)),
            pl.BlockSpec((seq, w), lambda b, g, j: (b, g), pipeline_mode=pl.Buffered(1)),
            pl.BlockSpec((nb, w, MOBA_BLOCK), lambda b, g, j: (b, g, 0), pipeline_mode=pl.Buffered(1)),
            pl.BlockSpec((nb, w), lambda b, g, j: (b, g)),
        ],
        out_specs=pl.BlockSpec((MOBA_BLOCK, w), lambda b, g, j: (b * nb + j, g)),
        out_shape=jax.ShapeDtypeStruct((t, d), bf16),
        scratch_shapes=[
            pltpu.VMEM((hg, 2 * HEAD_DIM, MOBA_BLOCK), bf16),
            pltpu.VMEM((hg, MOBA_BLOCK, MOBA_BLOCK), f32), pltpu.VMEM((hg, MOBA_BLOCK, MOBA_BLOCK), f32),
            pltpu.VMEM((hg, MOBA_BLOCK, MOBA_BLOCK), bf16), pltpu.VMEM((hg, MOBA_BLOCK, MOBA_BLOCK), bf16),
            pltpu.VMEM((hg, HEAD_DIM + SUM_ROWS, MOBA_BLOCK), f32),
        ],
        compiler_params=_params("parallel", "parallel", "arbitrary"),
        name="attn",
    )(q, k, vt, km)


def _oproj_router_kernel(h_ref, a_ref, wo_ref, g_ref, r_ref, h_out, hn_out, route_out, route_t_out, *, n_exp):
    h4 = h_ref[...] + _dot(a_ref[...], wo_ref[...])
    h_out[...] = h4
    hn = _rms(h4, g_ref[...])
    hn_out[...] = _pack_pairs(hn)
    logits = _dot_split(hn, r_ref[...])
    lane = lax.broadcasted_iota(jnp.int32, logits.shape, 1).astype(f32)
    cand = jnp.where(lane < n_exp, logits, NEG_INF)
    m1 = jnp.max(cand, axis=1, keepdims=True)
    i1 = jnp.min(jnp.where(cand == m1, lane, float(LANES)), axis=1, keepdims=True)
    cand = jnp.where(lane == i1, REMOVED, cand)
    m2 = jnp.max(cand, axis=1, keepdims=True)
    i2 = jnp.min(jnp.where(cand == m2, lane, float(LANES)), axis=1, keepdims=True)
    e2 = jnp.exp(m2 - m1)
    den = 1.0 + e2
    route = jnp.where(lane == 0, i1, jnp.where(lane == 1, i2, jnp.where(
        lane == 2, 1.0 / den, jnp.where(lane == 3, e2 / den, 0.0))))
    route_out[...] = route
    route_t_out[...] = route.T[:SUBLANES]


def _oproj_router_layer(h, attn, w_o, norm, router):
    t, d = h.shape
    tm = ROW_TILE
    n_exp = router.shape[1]
    r_pad = jnp.zeros((d, LANES), f32).at[:, :n_exp].set(router)
    row = lambda i: (i, 0)
    return pl.pallas_call(
        functools.partial(_oproj_router_kernel, n_exp=n_exp),
        grid=(t // tm,),
        in_specs=[pl.BlockSpec((tm, d), row), pl.BlockSpec((tm, d), row), _const_spec((d, d)),
                  _const_spec((1, d)), _const_spec((d, LANES))],
        out_specs=[pl.BlockSpec((tm, d), row), pl.BlockSpec((tm, d // 2), row), pl.BlockSpec((tm, LANES), row),
                   pl.BlockSpec((SUBLANES, tm), lambda i: (0, i))],
        out_shape=[jax.ShapeDtypeStruct((t, d), f32), jax.ShapeDtypeStruct((t, d // 2), jnp.uint32),
                   jax.ShapeDtypeStruct((t, LANES), f32), jax.ShapeDtypeStruct((SUBLANES, t), f32)],
        compiler_params=_params("parallel"),
        name="oproj_router",
    )(h, attn, w_o.astype(bf16), norm.reshape(1, d), r_pad)


def _sc_gather(x, idx):
    n = idx.shape[0]
    assert x.shape[1] == SC_ROW and n % SC_WINDOW == 0
    mesh = plsc.VectorSubcoreMesh(core_axis_name="core", subcore_axis_name="subcore")

    @pl.kernel(out_type=jax.ShapeDtypeStruct((n, SC_ROW), x.dtype), mesh=mesh, scratch_types=[])
    def gather(x_hbm, i_hbm, o_hbm):
        def body(i_vmem, o_vmem):
            pltpu.sync_copy(x_hbm.at[i_vmem.at[0]], o_vmem)

        pltpu.emit_pipeline(
            body,
            grid=(n // SC_WINDOW,),
            in_specs=[pl.BlockSpec((1, SC_WINDOW), index_map=lambda i: (0, i))],
            out_specs=[pl.BlockSpec((SC_WINDOW, SC_ROW), index_map=lambda i: (i, 0))],
            core_axis_name=("core", "subcore"),
            dimension_semantics=(pltpu.PARALLEL,),
        )(i_hbm, o_hbm)

    return gather(x, idx.reshape(n // LANES, LANES).reshape(1, n))


def _pieces(a):
    n, d = a.shape
    return a.reshape(n // SUBLANES, SUBLANES, d // SC_ROW, SC_ROW).transpose(0, 2, 1, 3).reshape(-1, SC_ROW)


def _unpieces(p, d):
    per = d // SC_ROW
    n = p.shape[0] // per
    return p.reshape(n // SUBLANES, per, SUBLANES, SC_ROW).transpose(0, 2, 1, 3).reshape(n, d)


def _gather_tokens(x, rows):
    n, d = x.shape
    m = rows[0].shape[0]
    per = d // SC_ROW
    assert n % SUBLANES == 0 and m % SUBLANES == 0
    tiled = []
    for r in rows:
        first = (r // SUBLANES * (SUBLANES * per) + r % SUBLANES).reshape(m // SUBLANES, SUBLANES)
        tiled.append(jnp.tile(first, (1, per)))
    chunk = jnp.arange(per * SUBLANES, dtype=jnp.int32) // SUBLANES * SUBLANES
    idx = jnp.concatenate(tiled, axis=1) + jnp.tile(chunk, len(rows))[None, :]
    return _unpieces(_sc_gather(_pieces(x), idx.reshape(-1)), len(rows) * d)


def _expert_kernel(te_ref, tv_ref, x_ref, w1_ref, w3_ref, w2_ref, *rest, nf, first_tile):
    o_ref, xb_ref, acc_ref = rest[-3:]
    i = pl.program_id(0) + first_tile
    f = pl.program_id(1)
    valid = tv_ref[i] > 0

    @pl.when(f == 0)
    def _():
        xb_ref[...] = _unpack_pairs(x_ref[...]).astype(bf16)
        acc_ref[...] = jnp.zeros_like(acc_ref)

    @pl.when(valid)
    def _():
        x = xb_ref[...]
        a = _dot(x, w1_ref[...])
        b = _dot(x, w3_ref[...])
        acc_ref[...] += _dot((jax.nn.silu(a) * b).astype(bf16), w2_ref[...])

    @pl.when(f == nf - 1)
    def _():
        o_ref[...] = _pack_pairs(acc_ref[...])


def _route_tables(choices, n_exp, tm):
    t = choices[0].shape[0]
    nslots = len(choices) * t
    experts = jnp.arange(n_exp, dtype=jnp.int32)[:, None]
    onehots = [(c[None, :] == experts).astype(jnp.int32) for c in choices]
    cums = [jnp.cumsum(oh, axis=1) for oh in onehots]
    totals = [cu[:, -1] for cu in cums]
    counts = sum(totals)
    tiles_e = (counts + tm - 1) // tm
    tile_end = jnp.cumsum(tiles_e)
    tile_start = tile_end - tiles_e
    group_start = jnp.cumsum(counts) - counts
    nt = nslots // tm + n_exp
    total = tile_end[-1]
    ti = jnp.arange(nt, dtype=jnp.int32)
    tv = (ti < total).astype(jnp.int32)
    tc = jnp.minimum(ti, total - 1)
    te = jnp.minimum(jnp.sum((tc[:, None] >= tile_end[None, :]).astype(jnp.int32), axis=1), n_exp - 1)
    order = jnp.argsort(jnp.concatenate(choices), stable=True).astype(jnp.int32)
    lane = jnp.arange(tm, dtype=jnp.int32)[None, :]
    rank = ((tc - tile_start[te]) * tm)[:, None] + lane
    real = (rank < counts[te][:, None]) & (tv[:, None] > 0)
    sidx = jnp.clip(group_start[te][:, None] + rank, 0, nslots - 1)
    src = jnp.where(real, order[sidx] % t, (ti[:, None] * tm + lane) % t).reshape(-1)
    pos = []
    earlier = jnp.zeros((n_exp,), jnp.int32)
    for oh, cu, tot in zip(onehots, cums, totals):
        pos.append(jnp.sum(oh * ((tile_start * tm + earlier)[:, None] + cu - oh), axis=0))
        earlier = earlier + tot
    return te, tv, src, pos, nt


def _experts_layer(hn, choices, w1, w3, w2):
    n_exp, d, ff = w1.shape
    tm = ROW_TILE
    fc = _ff_chunk(ff, EXPERT_FF_CHUNK)
    nf = ff // fc
    te, tv, src, pos, nt = _route_tables(choices, n_exp, tm)
    assert nt % SC_OVERLAP_PARTS == 0
    ntp = nt // SC_OVERLAP_PARTS

    ys = None
    for part in range(SC_OVERLAP_PARTS):
        t0 = part * ntp
        xs = _gather_tokens(hn, [src[t0 * tm:(t0 + ntp) * tm]])

        def fsel(i, f, tv_r, t0=t0):
            return jnp.where(tv_r[i + t0] > 0, f, nf - 1)

        def wmap_in(i, f, te_r, tv_r, t0=t0, fsel=fsel):
            return (te_r[i + t0], 0, fsel(i, f, tv_r))

        def wmap_out(i, f, te_r, tv_r, t0=t0, fsel=fsel):
            return (te_r[i + t0], fsel(i, f, tv_r), 0)

        in_specs = [
            pl.BlockSpec((tm, d // 2), lambda i, f, *_: (i, 0)),
            pl.BlockSpec((None, d, fc), wmap_in),
            pl.BlockSpec((None, d, fc), wmap_in),
            pl.BlockSpec((None, fc, d), wmap_out),
        ]
        args = [te, tv, xs, w1, w3, w2]
        aliases = {}
        if ys is not None:
            in_specs.append(pl.BlockSpec(memory_space=pl.ANY))
            args.append(ys)
            aliases = {len(args) - 1: 0}
        ys = pl.pallas_call(
            functools.partial(_expert_kernel, nf=nf, first_tile=t0),
            grid_spec=pltpu.PrefetchScalarGridSpec(
                num_scalar_prefetch=2,
                grid=(ntp, nf),
                in_specs=in_specs,
                out_specs=pl.BlockSpec((tm, d // 2), lambda i, f, *_, t0=t0: (i + t0, 0)),
                scratch_shapes=[pltpu.VMEM((tm, d), bf16), pltpu.VMEM((tm, d), f32)],
            ),
            out_shape=jax.ShapeDtypeStruct((nt * tm, d // 2), jnp.uint32),
            input_output_aliases=aliases,
            compiler_params=_params("parallel", "arbitrary"),
            name="experts",
        )(*args)
    return ys, pos


def _final_kernel(h_ref, y_ref, route_ref, p_ref, pn_ref, wg_ref, wp_ref, fn_ref, *rest):
    o_ref = rest[-1]
    d = h_ref.shape[1]
    route = route_ref[...]
    y = y_ref[...]
    h5 = h_ref[...] + route[:, 2:3] * _unpack_pairs(y[:, :d // 2]) + route[:, 3:4] * _unpack_pairs(y[:, d // 2:])
    hn = _rms(h5, pn_ref[...]).astype(bf16)
    h6 = h5 + jax.nn.sigmoid(_dot(hn, wg_ref[...])) * _dot(p_ref[...].astype(bf16), wp_ref[...])
    o_ref[...] = _rms(h6, fn_ref[...])


def _final_layer(h, ys, pos, route, p_all, layer, ple_norm, ple_gate, ple_proj, final_norm):
    t, d = h.shape
    pd = p_all.shape[2]
    tm = ROW_TILE
    assert t % (SC_OVERLAP_PARTS * tm) == 0
    tp = t // SC_OVERLAP_PARTS
    out = None
    for part in range(SC_OVERLAP_PARTS):
        r0 = part * tp // tm
        y2 = _gather_tokens(ys, [p[part * tp:(part + 1) * tp] for p in pos])
        row = lambda i, r0=r0: (i + r0, 0)
        in_specs = [pl.BlockSpec((tm, d), row), pl.BlockSpec((tm, y2.shape[1]), lambda i: (i, 0)),
                    pl.BlockSpec((tm, LANES), row),
                    pl.BlockSpec((None, tm, pd), lambda i, r0=r0: (layer, i + r0, 0)),
                    _const_spec((1, d)), _const_spec((d, d)), _const_spec((pd, d)), _const_spec((1, d))]
        args = [h, y2, route, p_all, ple_norm.reshape(1, d), ple_gate.astype(bf16), ple_proj.astype(bf16),
                final_norm.reshape(1, d)]
        aliases = {}
        if out is not None:
            in_specs.append(pl.BlockSpec(memory_space=pl.ANY))
            args.append(out)
            aliases = {len(args) - 1: 0}
        out = pl.pallas_call(
            _final_kernel,
            grid=(tp // tm,),
            in_specs=in_specs,
            out_specs=pl.BlockSpec((tm, d), row),
            out_shape=jax.ShapeDtypeStruct((t, d), f32),
            input_output_aliases=aliases,
            compiler_params=_params("parallel"),
            name="final",
        )(*args)
    return out


def kernel(x, p, pool_norm, pool_w, pool_scale, kv_norm, w_k, w_v, attn_norm, w_q, w_o, ffn_norm, ffn_w1, ffn_w3, ffn_w2, router, exp_w1, exp_w3, exp_w2, ple_norm, ple_gate, ple_proj, final_norm):
    batch, seq, d = x.shape
    t = batch * seq
    assert seq % ROW_TILE == 0 and ROW_TILE % MOBA_BLOCK == 0 and d % HEAD_DIM == 0
    assert p.shape[0] == 2 and router.shape[2] <= LANES
    h = x.reshape(t, d)
    pf = p.reshape(p.shape[0], t, p.shape[-1])

    ew1, ew3, ew2 = _cast_bf16([exp_w1[0], exp_w3[0], exp_w2[0]], EXPERT_CAST_STEPS)
    h = _pool_layer(h, pool_norm[0], pool_w[0], pool_scale[0], seq, ew2[0, :2 * SUBLANES, :LANES])
    h = _swiglu_layer(h, ffn_norm[0], ffn_w1[0], ffn_w3[0], ffn_w2[0])
    h, q, k, v, km = _ple_qkv_layer(h, pf, 0, ple_norm[0], ple_gate[0], ple_proj[0],
                                    kv_norm, w_k, w_v, attn_norm[0], w_q[0], seq)
    attn = _attention(q, k, v, km, batch, seq)
    h, hn, route, route_t = _oproj_router_layer(h, attn, w_o[0], ffn_norm[1], router[0])
    choices = [route_t[k].astype(jnp.int32) for k in range(TOP_K_EXPERTS)]
    ys, pos = _experts_layer(hn, choices, ew1, ew3, ew2)
    out = _final_layer(h, ys, pos, route, pf, 1, ple_norm[1], ple_gate[1],
                       ple_proj[1], final_norm)
    return out.reshape(batch, seq, d)
```

```python
import functools

import jax
import jax.numpy as jnp
from jax import lax
from jax.experimental import pallas as pl
from jax.experimental.pallas import tpu as pltpu
from jax.experimental.pallas import tpu_sc as plsc

POOL_WINDOWS = (2, 4, 8, 16)
HEAD_DIM = 128
MOBA_BLOCK = 256
MOBA_TOPK = 3
ROPE_THETA = 500000.0
ROPE_DIM = HEAD_DIM // 4
TOP_K_EXPERTS = 2
RMS_EPS = 1e-6
NEG_INF = -1e30
REMOVED = -3e38

LANES = 128
SUBLANES = 8
ROW_TILE = 512
POOL_SUB = 128
ATTN_HEADS_PER_STEP = 4
ATTN_BLOCKS_PER_TRIP = 4
SUM_ROWS = 16
LOG2E = 1.4426950408889634
EXPERT_FF_CHUNK = 1792
EXPERT_CAST_STEPS = 32
SC_OVERLAP_PARTS = 2
SC_ROW = 128
SC_WINDOW = 256
VMEM_LIMIT = 56 * 1024 * 1024

bf16 = jnp.bfloat16
f32 = jnp.float32


def _dot(a, b):
    return jnp.dot(a, b, preferred_element_type=f32)


def _dot_split(a, b):
    a_hi = a.astype(bf16)
    a_lo = (a - a_hi.astype(f32)).astype(bf16)
    b_hi = b.astype(bf16)
    b_lo = (b - b_hi.astype(f32)).astype(bf16)
    return _dot(a_hi, b_hi) + (_dot(a_lo, b_hi) + _dot(a_hi, b_lo))


def _pack_pairs(x):
    half = x.shape[1] // 2
    lo = lax.bitcast_convert_type(x[:, :half].astype(bf16).astype(f32), jnp.uint32)
    hi = lax.bitcast_convert_type(x[:, half:].astype(bf16).astype(f32), jnp.uint32)
    return (hi & jnp.uint32(0xFFFF0000)) | (lo >> 16)


def _unpack_pairs(w):
    lo = lax.bitcast_convert_type(w << 16, f32)
    hi = lax.bitcast_convert_type(w & jnp.uint32(0xFFFF0000), f32)
    return jnp.concatenate([lo, hi], axis=1)


def _cast_kernel(*refs):
    n = len(refs) // 2
    for src, dst in zip(refs[:n], refs[n:]):
        dst[...] = src[...].astype(bf16)


def _cast_bf16(ws, steps):
    flat = [w.reshape(-1, w.shape[-1]) for w in ws]
    for a in flat:
        assert a.shape[0] % steps == 0 and (a.shape[0] // steps) % (2 * SUBLANES) == 0
    specs = [pl.BlockSpec((a.shape[0] // steps, a.shape[1]), lambda s: (s, 0)) for a in flat]
    outs = pl.pallas_call(
        _cast_kernel,
        grid=(steps,),
        in_specs=specs,
        out_specs=specs,
        out_shape=[jax.ShapeDtypeStruct(a.shape, bf16) for a in flat],
        compiler_params=_params("parallel"),
        name="cast_bf16",
    )(*flat)
    return [o.reshape(w.shape) for o, w in zip(outs, ws)]


def _rms(x, g):
    var = jnp.mean(x * x, axis=-1, keepdims=True)
    return x * lax.rsqrt(var + RMS_EPS) * g


def _params(*sem):
    return pltpu.CompilerParams(dimension_semantics=sem, vmem_limit_bytes=VMEM_LIMIT)


def _const_spec(shape):
    nd = len(shape)
    return pl.BlockSpec(shape, lambda *_: (0,) * nd)


def _pool_kernel(x_ref, halo_ref, g_ref, pw_ref, ps_ref, after_ref, o_ref, pooled_ref, *, ts, seq):
    del after_ref
    i = pl.program_id(0)
    g = g_ref[...]
    x = x_ref[...]
    xn = _rms(x, g)
    keep = jnp.where((i * ts) % seq == 0, 0.0, 1.0)
    hnb = (_rms(halo_ref[...], g) * keep).astype(bf16)
    xnb = xn.astype(bf16)
    gd = x.shape[1] // len(POOL_WINDOWS)
    r = lax.broadcasted_iota(jnp.int32, (POOL_SUB, 2 * POOL_SUB), 0)
    c = lax.broadcasted_iota(jnp.int32, (POOL_SUB, 2 * POOL_SUB), 1)
    dist = r + POOL_SUB - c
    bands = [((dist >= 0) & (dist < w)).astype(bf16) for w in POOL_WINDOWS]
    rows = lax.broadcasted_iota(jnp.int32, (POOL_SUB, 1), 0)
    for sb in range(ts // POOL_SUB):
        lo, hi = sb * POOL_SUB, (sb + 1) * POOL_SUB
        prev = hnb if sb == 0 else xnb[lo - POOL_SUB:lo]
        ext = jnp.concatenate([prev, xnb[lo:hi]], axis=0)
        tpos = (i * ts + lo) % seq + rows
        for gi, w in enumerate(POOL_WINDOWS):
            cs = slice(gi * gd, (gi + 1) * gd)
            wsum = _dot(bands[gi], ext[:, cs])
            inv_cnt = 1.0 / jnp.minimum(tpos + 1, w).astype(f32)
            pooled_ref[lo:hi, cs] = (wsum * inv_cnt - xn[lo:hi, cs]).astype(bf16)
    for gi in range(len(POOL_WINDOWS)):
        cs = slice(gi * gd, (gi + 1) * gd)
        mixed = _dot(pooled_ref[:, cs], pw_ref[gi])
        o_ref[:, cs] = x[:, cs] + mixed * ps_ref[:, cs]


def _pool_layer(h, norm, pool_w, pool_scale, seq, after):
    t, d = h.shape
    ts = ROW_TILE
    ng = len(POOL_WINDOWS)
    gd = d // ng
    per = ts // POOL_SUB
    return pl.pallas_call(
        functools.partial(_pool_kernel, ts=ts, seq=seq),
        grid=(t // ts,),
        in_specs=[
            pl.BlockSpec((ts, d), lambda i: (i, 0)),
            pl.BlockSpec((POOL_SUB, d), lambda i: (jnp.maximum(i * per - 1, 0), 0)),
            _const_spec((1, d)),
            _const_spec((ng, gd, gd)),
            _const_spec((1, d)),
            _const_spec(after.shape),
        ],
        out_specs=pl.BlockSpec((ts, d), lambda i: (i, 0)),
        out_shape=jax.ShapeDtypeStruct((t, d), f32),
        scratch_shapes=[pltpu.VMEM((ts, d), bf16)],
        compiler_params=_params("parallel"),
        name="pool",
    )(h, h, norm.reshape(1, d), pool_w.astype(bf16), pool_scale.reshape(1, d), after)


def _swiglu_kernel(h_ref, g_ref, w1_ref, w3_ref, w2_ref, o_ref):
    x = h_ref[...]
    hn = _rms(x, g_ref[...]).astype(bf16)
    a = _dot(hn, w1_ref[...])
    b = _dot(hn, w3_ref[...])
    o_ref[...] = x + _dot((jax.nn.silu(a) * b).astype(bf16), w2_ref[...])


def _ff_chunk(ff, target):
    units = ff // LANES
    best = 1
    for k in range(1, units + 1):
        if units % k == 0 and k * LANES <= target:
            best = k
    return best * LANES


def _swiglu_layer(h, norm, w1, w3, w2):
    t, d = h.shape
    ff = w1.shape[1]
    tm = ROW_TILE
    w1r, w3r, w2r = w1.astype(bf16), w3.astype(bf16), w2.astype(bf16)
    resident = lambda shape: pl.BlockSpec(shape, lambda i: (0, 0), pipeline_mode=pl.Buffered(1))
    return pl.pallas_call(
        _swiglu_kernel,
        grid=(t // tm,),
        in_specs=[
            pl.BlockSpec((tm, d), lambda i: (i, 0)),
            _const_spec((1, d)),
            resident((d, ff)), resident((d, ff)), resident((ff, d)),
        ],
        out_specs=pl.BlockSpec((tm, d), lambda i: (i, 0)),
        out_shape=jax.ShapeDtypeStruct((t, d), f32),
        compiler_params=_params("parallel"),
        name="swiglu",
    )(h, norm.reshape(1, d), w1r, w3r, w2r)


def _rope_tables(seq):
    half = ROPE_DIM // 2
    inv_freq = jnp.float32(ROPE_THETA) ** (-(jnp.arange(0, ROPE_DIM, 2, dtype=f32) / ROPE_DIM))
    ang = jnp.arange(seq, dtype=f32)[:, None] * inv_freq[None, :]
    cos, sin = jnp.cos(ang), jnp.sin(ang)
    ones = jnp.ones((seq, HEAD_DIM - ROPE_DIM), f32)
    zeros = jnp.zeros((seq, HEAD_DIM - half), f32)
    cos_t = jnp.concatenate([cos, cos, ones], axis=1)
    up_t = jnp.concatenate([-sin, zeros], axis=1)
    dn_t = jnp.concatenate([jnp.zeros((seq, half), f32), sin, jnp.zeros((seq, HEAD_DIM - ROPE_DIM), f32)], axis=1)
    return cos_t, up_t, dn_t


def _rope(xh, cos_t, up_t, dn_t):
    half = ROPE_DIM // 2
    return (xh * cos_t + pltpu.roll(xh, HEAD_DIM - half, 1) * up_t + pltpu.roll(xh, half, 1) * dn_t)


def _ple_qkv_kernel(h_ref, p_ref, pn_ref, wg_ref, wp_ref, kvn_ref, wk_ref, wv_ref, an_ref, wq_ref,
                    cos_ref, up_ref, dn_ref, h_out, q_out, k_out, v_out, km_out):
    h = h_ref[...]
    hn = _rms(h, pn_ref[...]).astype(bf16)
    h3 = h + jax.nn.sigmoid(_dot(hn, wg_ref[...])) * _dot(p_ref[...].astype(bf16), wp_ref[...])
    h_out[...] = h3
    base = h3 * lax.rsqrt(jnp.mean(h3 * h3, axis=-1, keepdims=True) + RMS_EPS)
    kn = (base * kvn_ref[...]).astype(bf16)
    qn = (base * an_ref[...]).astype(bf16)
    v = _dot(kn, wv_ref[...])
    for bi in range(h.shape[0] // MOBA_BLOCK):
        v_out[bi] = v[bi * MOBA_BLOCK:(bi + 1) * MOBA_BLOCK].T.astype(bf16)
    k = _dot(kn, wk_ref[...])
    q = _dot(qn, wq_ref[...])
    cos_t, up_t, dn_t = cos_ref[...], up_ref[...], dn_ref[...]
    tm, d = h.shape
    sub = km_out.shape[0] // (tm // MOBA_BLOCK)
    for hh in range(d // HEAD_DIM):
        cs = slice(hh * HEAD_DIM, (hh + 1) * HEAD_DIM)
        q_out[:, cs] = _rope(q[:, cs], cos_t, up_t, dn_t)
        kr = _rope(k[:, cs], cos_t, up_t, dn_t)
        k_out[:, cs] = kr.astype(bf16)
        for bi in range(tm // MOBA_BLOCK):
            m = jnp.mean(kr[bi * MOBA_BLOCK:(bi + 1) * MOBA_BLOCK], axis=0, keepdims=True)
            km_out[bi * sub:(bi + 1) * sub, cs] = jnp.broadcast_to(m, (sub, HEAD_DIM))


def _layer_rows_spec(p_all, layer, tm):
    return pl.BlockSpec((None, tm, p_all.shape[2]), lambda i: (layer, i, 0))


def _ple_qkv_layer(h, p_all, layer, ple_norm, ple_gate, ple_proj, kv_norm, w_k, w_v, attn_norm, w_q, seq):
    t, d = h.shape
    pd = p_all.shape[2]
    tm = ROW_TILE
    sub = 8
    cos_t, up_t, dn_t = _rope_tables(seq)
    tiles_per_seq = seq // tm
    row = lambda i: (i, 0)
    tab = pl.BlockSpec((tm, HEAD_DIM), lambda i: (i % tiles_per_seq, 0))
    nkm = t // MOBA_BLOCK * sub
    outs = pl.pallas_call(
        _ple_qkv_kernel,
        grid=(t // tm,),
        in_specs=[
            pl.BlockSpec((tm, d), row), _layer_rows_spec(p_all, layer, tm),
            _const_spec((1, d)), _const_spec((d, d)), _const_spec((pd, d)),
            _const_spec((1, d)), _const_spec((d, d)), _const_spec((d, d)),
            _const_spec((1, d)), _const_spec((d, d)),
            tab, tab, tab,
        ],
        out_specs=[
            pl.BlockSpec((tm, d), row), pl.BlockSpec((tm, d), row),
            pl.BlockSpec((tm, d), row),
            pl.BlockSpec((tm // MOBA_BLOCK, d, MOBA_BLOCK), lambda i: (i, 0, 0)),
            pl.BlockSpec((tm // MOBA_BLOCK * sub, d), row),
        ],
        out_shape=[
            jax.ShapeDtypeStruct((t, d), f32), jax.ShapeDtypeStruct((t, d), f32),
            jax.ShapeDtypeStruct((t, d), bf16),
            jax.ShapeDtypeStruct((t // MOBA_BLOCK, d, MOBA_BLOCK), bf16),
            jax.ShapeDtypeStruct((nkm, d), f32),
        ],
        compiler_params=_params("parallel"),
        name="ple_qkv",
    )(h, p_all, ple_norm.reshape(1, d), ple_gate.astype(bf16), ple_proj.astype(bf16),
      kv_norm.reshape(1, d), w_k.astype(bf16), w_v.astype(bf16),
      attn_norm.reshape(1, d), w_q.astype(bf16), cos_t, up_t, dn_t)
    h3, q, k, v, km = outs
    return h3, q, k, v, km.reshape(t // MOBA_BLOCK, sub, d)[:, 0, :]


def _attn_kernel(q_ref, k_ref, vt_ref, km_ref, o_ref, qa_ref, sa_ref, sb_ref, pa_ref, pb_ref, acc_ref, *, hg):
    j = pl.program_id(2)
    nb = km_ref.shape[0]
    bs = q_ref.shape[0]
    hd = HEAD_DIM
    qscale = (hd ** -0.5) * LOG2E
    blk = lax.broadcasted_iota(jnp.int32, (nb, bs), 0).astype(f32)
    jf = j.astype(f32)
    krow = lax.broadcasted_iota(jnp.int32, (bs, bs), 0)
    qcol = lax.broadcasted_iota(jnp.int32, (bs, bs), 1)
    start_j = pl.multiple_of(j * bs, bs)

    ones_rows = jnp.ones((SUM_ROWS, bs), bf16)

    def weighted_values(n, h, pb):
        return _dot(jnp.concatenate([vt_ref[n, h * hd:(h + 1) * hd, :], ones_rows], axis=0), pb)

    heads = [slice(h * hd, (h + 1) * hd) for h in range(hg)]
    qfts = [q_ref[:, cs].T for cs in heads]
    qbts = [(qft * qscale).astype(bf16) for qft in qfts]
    diag = [_dot(k_ref[pl.ds(start_j, bs), cs], qbt) for cs, qbt in zip(heads, qbts)]
    gates = [_dot_split(km_ref[:, cs], qft) for cs, qft in zip(heads, qfts)]
    cands = [jnp.where(blk < jf, gate, NEG_INF) for gate in gates]
    sels = [blk < 0.0] * hg
    for _ in range(min(MOBA_TOPK, nb)):
        for h in range(hg):
            mx = jnp.max(cands[h], axis=0, keepdims=True)
            pick = jnp.min(jnp.where(cands[h] == mx, blk, float(nb)), axis=0, keepdims=True)
            hit = blk == pick
            sels[h] = sels[h] | hit
            cands[h] = jnp.where(hit, REMOVED, cands[h])
    init = []
    for h in range(hg):
        bias = jnp.where(sels[h] & (blk < jf), 0.0, NEG_INF)
        if nb < hd:
            bias = jnp.concatenate([bias, jnp.zeros((hd - nb, bs), f32)], axis=0)
        qa_ref[h] = jnp.concatenate([qbts[h], bias.astype(bf16)], axis=0)
        s = jnp.where(krow <= qcol, diag[h], NEG_INF)
        m0 = jnp.max(s, axis=0, keepdims=True)
        acc_ref[h] = weighted_values(j, h, jnp.exp2((s - m0).astype(bf16)))
        init += [m0, jnp.ones_like(m0)]

    lane = lax.broadcasted_iota(jnp.int32, (bs, hd), 1)

    def block_scores(n, h):
        start = pl.multiple_of(n * bs, bs)
        onehot = (lane == n).astype(bf16)
        ka = jnp.concatenate([k_ref[pl.ds(start, bs), h * hd:(h + 1) * hd], onehot], axis=1)
        return _dot(ka, qa_ref[h])

    def stage(n, s_in, s_out, p_prev, p_out, stats):
        nn = jnp.minimum(n + 1, nb - 1)
        prev = jnp.clip(n - 1, 0, nb - 1)
        new = []
        for h in range(hg):
            m, alpha_p = stats[2 * h:2 * h + 2]
            acc_ref[h] = alpha_p * acc_ref[h] + weighted_values(prev, h, p_prev[h])
            s = s_in[h]
            m_new = jnp.maximum(m, jnp.max(s, axis=0, keepdims=True))
            p_out[h] = jnp.exp2((s - m_new).astype(bf16))
            s_out[h] = block_scores(nn, h)
            new += [m_new, jnp.exp2(m - m_new)]
        return new

    per_trip = ATTN_BLOCKS_PER_TRIP

    def body(i, stats):
        for u in range(0, per_trip, 2):
            stats = stage(per_trip * i + u, sa_ref, sb_ref, pb_ref, pa_ref, stats)
            stats = stage(per_trip * i + u + 1, sb_ref, sa_ref, pa_ref, pb_ref, stats)
        return tuple(stats)

    for h in range(hg):
        sa_ref[h] = block_scores(0, h)
        pb_ref[h] = jnp.zeros((bs, bs), bf16)
    trips = (j + per_trip - 1) // per_trip
    res = lax.fori_loop(0, trips, body, tuple(init))
    last = jnp.clip(per_trip * trips - 1, 0, nb - 1)
    tails = [weighted_values(last, h, pb_ref[h]) for h in range(hg)]
    for h in range(hg):
        acc = res[2 * h + 1] * acc_ref[h] + tails[h]
        o_ref[:, h * hd:(h + 1) * hd] = (acc[:hd] * (1.0 / acc[hd:hd + 1])).T.astype(o_ref.dtype)


def _attention(q, k, vt, km, batch, seq):
    t, d = q.shape
    nh = d // HEAD_DIM
    nb = seq // MOBA_BLOCK
    hg = ATTN_HEADS_PER_STEP
    assert nb <= HEAD_DIM and nh % hg == 0
    w = hg * HEAD_DIM
    return pl.pallas_call(
        functools.partial(_attn_kernel, hg=hg),
        grid=(batch, nh // hg, nb),
        in_specs=[
            pl.BlockSpec((MOBA_BLOCK, w), lambda b, g, j: (b * nb + j, g)),
            pl.BlockSpec((seq, w), lambda b, g, j: (b, g), pipeline_mode=pl.Buffered(1)),
            pl.BlockSpec((nb, w, MOBA_BLOCK), lambda b, g, j: (b, g, 0), pipeline_mode=pl.Buffered(1)),
            pl.BlockSpec((nb, w), lambda b, g, j: (b, g)),
        ],
        out_specs=pl.BlockSpec((MOBA_BLOCK, w), lambda b, g, j: (b * nb + j, g)),
        out_shape=jax.ShapeDtypeStruct((t, d), bf16),
        scratch_shapes=[
            pltpu.VMEM((hg, 2 * HEAD_DIM, MOBA_BLOCK), bf16),
            pltpu.VMEM((hg, MOBA_BLOCK, MOBA_BLOCK), f32), pltpu.VMEM((hg, MOBA_BLOCK, MOBA_BLOCK), f32),
            pltpu.VMEM((hg, MOBA_BLOCK, MOBA_BLOCK), bf16), pltpu.VMEM((hg, MOBA_BLOCK, MOBA_BLOCK), bf16),
            pltpu.VMEM((hg, HEAD_DIM + SUM_ROWS, MOBA_BLOCK), f32),
        ],
        compiler_params=_params("parallel", "parallel", "arbitrary"),
        name="attn",
    )(q, k, vt, km)


def _oproj_router_kernel(h_ref, a_ref, wo_ref, g_ref, r_ref, h_out, hn_out, route_out, route_t_out, *, n_exp):
    h4 = h_ref[...] + _dot(a_ref[...], wo_ref[...])
    h_out[...] = h4
    hn = _rms(h4, g_ref[...])
    hn_out[...] = _pack_pairs(hn)
    logits = _dot_split(hn, r_ref[...])
    lane = lax.broadcasted_iota(jnp.int32, logits.shape, 1).astype(f32)
    cand = jnp.where(lane < n_exp, logits, NEG_INF)
    m1 = jnp.max(cand, axis=1, keepdims=True)
    i1 = jnp.min(jnp.where(cand == m1, lane, float(LANES)), axis=1, keepdims=True)
    cand = jnp.where(lane == i1, REMOVED, cand)
    m2 = jnp.max(cand, axis=1, keepdims=True)
    i2 = jnp.min(jnp.where(cand == m2, lane, float(LANES)), axis=1, keepdims=True)
    e2 = jnp.exp(m2 - m1)
    den = 1.0 + e2
    route = jnp.where(lane == 0, i1, jnp.where(lane == 1, i2, jnp.where(
        lane == 2, 1.0 / den, jnp.where(lane == 3, e2 / den, 0.0))))
    route_out[...] = route
    route_t_out[...] = route.T[:SUBLANES]


def _oproj_router_layer(h, attn, w_o, norm, router):
    t, d = h.shape
    tm = ROW_TILE
    n_exp = router.shape[1]
    r_pad = jnp.zeros((d, LANES), f32).at[:, :n_exp].set(router)
    row = lambda i: (i, 0)
    return pl.pallas_call(
        functools.partial(_oproj_router_kernel, n_exp=n_exp),
        grid=(t // tm,),
        in_specs=[pl.BlockSpec((tm, d), row), pl.BlockSpec((tm, d), row), _const_spec((d, d)),
                  _const_spec((1, d)), _const_spec((d, LANES))],
        out_specs=[pl.BlockSpec((tm, d), row), pl.BlockSpec((tm, d // 2), row), pl.BlockSpec((tm, LANES), row),
                   pl.BlockSpec((SUBLANES, tm), lambda i: (0, i))],
        out_shape=[jax.ShapeDtypeStruct((t, d), f32), jax.ShapeDtypeStruct((t, d // 2), jnp.uint32),
                   jax.ShapeDtypeStruct((t, LANES), f32), jax.ShapeDtypeStruct((SUBLANES, t), f32)],
        compiler_params=_params("parallel"),
        name="oproj_router",
    )(h, attn, w_o.astype(bf16), norm.reshape(1, d), r_pad)


def _sc_gather(x, idx):
    n = idx.shape[0]
    assert x.shape[1] == SC_ROW and n % SC_WINDOW == 0
    mesh = plsc.VectorSubcoreMesh(core_axis_name="core", subcore_axis_name="subcore")

    @pl.kernel(out_type=jax.ShapeDtypeStruct((n, SC_ROW), x.dtype), mesh=mesh, scratch_types=[])
    def gather(x_hbm, i_hbm, o_hbm):
        def body(i_vmem, o_vmem):
            pltpu.sync_copy(x_hbm.at[i_vmem.at[0]], o_vmem)

        pltpu.emit_pipeline(
            body,
            grid=(n // SC_WINDOW,),
            in_specs=[pl.BlockSpec((1, SC_WINDOW), index_map=lambda i: (0, i))],
            out_specs=[pl.BlockSpec((SC_WINDOW, SC_ROW), index_map=lambda i: (i, 0))],
            core_axis_name=("core", "subcore"),
            dimension_semantics=(pltpu.PARALLEL,),
        )(i_hbm, o_hbm)

    return gather(x, idx.reshape(n // LANES, LANES).reshape(1, n))


def _pieces(a):
    n, d = a.shape
    return a.reshape(n // SUBLANES, SUBLANES, d // SC_ROW, SC_ROW).transpose(0, 2, 1, 3).reshape(-1, SC_ROW)


def _unpieces(p, d):
    per = d // SC_ROW
    n = p.shape[0] // per
    return p.reshape(n // SUBLANES, per, SUBLANES, SC_ROW).transpose(0, 2, 1, 3).reshape(n, d)


def _gather_tokens(x, rows):
    n, d = x.shape
    m = rows[0].shape[0]
    per = d // SC_ROW
    assert n % SUBLANES == 0 and m % SUBLANES == 0
    tiled = []
    for r in rows:
        first = (r // SUBLANES * (SUBLANES * per) + r % SUBLANES).reshape(m // SUBLANES, SUBLANES)
        tiled.append(jnp.tile(first, (1, per)))
    chunk = jnp.arange(per * SUBLANES, dtype=jnp.int32) // SUBLANES * SUBLANES
    idx = jnp.concatenate(tiled, axis=1) + jnp.tile(chunk, len(rows))[None, :]
    return _unpieces(_sc_gather(_pieces(x), idx.reshape(-1)), len(rows) * d)


def _expert_kernel(te_ref, tv_ref, x_ref, w1_ref, w3_ref, w2_ref, *rest, nf, first_tile):
    o_ref, xb_ref, acc_ref = rest[-3:]
    i = pl.program_id(0) + first_tile
    f = pl.program_id(1)
    valid = tv_ref[i] > 0

    @pl.when(f == 0)
    def _():
        xb_ref[...] = _unpack_pairs(x_ref[...]).astype(bf16)
        acc_ref[...] = jnp.zeros_like(acc_ref)

    @pl.when(valid)
    def _():
        x = xb_ref[...]
        a = _dot(x, w1_ref[...])
        b = _dot(x, w3_ref[...])
        acc_ref[...] += _dot((jax.nn.silu(a) * b).astype(bf16), w2_ref[...])

    @pl.when(f == nf - 1)
    def _():
        o_ref[...] = _pack_pairs(acc_ref[...])


def _route_tables(choices, n_exp, tm):
    t = choices[0].shape[0]
    nslots = len(choices) * t
    experts = jnp.arange(n_exp, dtype=jnp.int32)[:, None]
    onehots = [(c[None, :] == experts).astype(jnp.int32) for c in choices]
    cums = [jnp.cumsum(oh, axis=1) for oh in onehots]
    totals = [cu[:, -1] for cu in cums]
    counts = sum(totals)
    tiles_e = (counts + tm - 1) // tm
    tile_end = jnp.cumsum(tiles_e)
    tile_start = tile_end - tiles_e
    group_start = jnp.cumsum(counts) - counts
    nt = nslots // tm + n_exp
    total = tile_end[-1]
    ti = jnp.arange(nt, dtype=jnp.int32)
    tv = (ti < total).astype(jnp.int32)
    tc = jnp.minimum(ti, total - 1)
    te = jnp.minimum(jnp.sum((tc[:, None] >= tile_end[None, :]).astype(jnp.int32), axis=1), n_exp - 1)
    order = jnp.argsort(jnp.concatenate(choices), stable=True).astype(jnp.int32)
    lane = jnp.arange(tm, dtype=jnp.int32)[None, :]
    rank = ((tc - tile_start[te]) * tm)[:, None] + lane
    real = (rank < counts[te][:, None]) & (tv[:, None] > 0)
    sidx = jnp.clip(group_start[te][:, None] + rank, 0, nslots - 1)
    src = jnp.where(real, order[sidx] % t, (ti[:, None] * tm + lane) % t).reshape(-1)
    pos = []
    earlier = jnp.zeros((n_exp,), jnp.int32)
    for oh, cu, tot in zip(onehots, cums, totals):
        pos.append(jnp.sum(oh * ((tile_start * tm + earlier)[:, None] + cu - oh), axis=0))
        earlier = earlier + tot
    return te, tv, src, pos, nt


def _experts_layer(hn, choices, w1, w3, w2):
    n_exp, d, ff = w1.shape
    tm = ROW_TILE
    fc = _ff_chunk(ff, EXPERT_FF_CHUNK)
    nf = ff // fc
    te, tv, src, pos, nt = _route_tables(choices, n_exp, tm)
    assert nt % SC_OVERLAP_PARTS == 0
    ntp = nt // SC_OVERLAP_PARTS

    ys = None
    for part in range(SC_OVERLAP_PARTS):
        t0 = part * ntp
        xs = _gather_tokens(hn, [src[t0 * tm:(t0 + ntp) * tm]])

        def fsel(i, f, tv_r, t0=t0):
            return jnp.where(tv_r[i + t0] > 0, f, nf - 1)

        def wmap_in(i, f, te_r, tv_r, t0=t0, fsel=fsel):
            return (te_r[i + t0], 0, fsel(i, f, tv_r))

        def wmap_out(i, f, te_r, tv_r, t0=t0, fsel=fsel):
            return (te_r[i + t0], fsel(i, f, tv_r), 0)

        in_specs = [
            pl.BlockSpec((tm, d // 2), lambda i, f, *_: (i, 0)),
            pl.BlockSpec((None, d, fc), wmap_in),
            pl.BlockSpec((None, d, fc), wmap_in),
            pl.BlockSpec((None, fc, d), wmap_out),
        ]
        args = [te, tv, xs, w1, w3, w2]
        aliases = {}
        if ys is not None:
            in_specs.append(pl.BlockSpec(memory_space=pl.ANY))
            args.append(ys)
            aliases = {len(args) - 1: 0}
        ys = pl.pallas_call(
            functools.partial(_expert_kernel, nf=nf, first_tile=t0),
            grid_spec=pltpu.PrefetchScalarGridSpec(
                num_scalar_prefetch=2,
                grid=(ntp, nf),
                in_specs=in_specs,
                out_specs=pl.BlockSpec((tm, d // 2), lambda i, f, *_, t0=t0: (i + t0, 0)),
                scratch_shapes=[pltpu.VMEM((tm, d), bf16), pltpu.VMEM((tm, d), f32)],
            ),
            out_shape=jax.ShapeDtypeStruct((nt * tm, d // 2), jnp.uint32),
            input_output_aliases=aliases,
            compiler_params=_params("parallel", "arbitrary"),
            name="experts",
        )(*args)
    return ys, pos


def _final_kernel(h_ref, y_ref, route_ref, p_ref, pn_ref, wg_ref, wp_ref, fn_ref, *rest):
    o_ref = rest[-1]
    d = h_ref.shape[1]
    route = route_ref[...]
    y = y_ref[...]
    h5 = h_ref[...] + route[:, 2:3] * _unpack_pairs(y[:, :d // 2]) + route[:, 3:4] * _unpack_pairs(y[:, d // 2:])
    hn = _rms(h5, pn_ref[...]).astype(bf16)
    h6 = h5 + jax.nn.sigmoid(_dot(hn, wg_ref[...])) * _dot(p_ref[...].astype(bf16), wp_ref[...])
    o_ref[...] = _rms(h6, fn_ref[...])


def _final_layer(h, ys, pos, route, p_all, layer, ple_norm, ple_gate, ple_proj, final_norm):
    t, d = h.shape
    pd = p_all.shape[2]
    tm = ROW_TILE
    assert t % (SC_OVERLAP_PARTS * tm) == 0
    tp = t // SC_OVERLAP_PARTS
    out = None
    for part in range(SC_OVERLAP_PARTS):
        r0 = part * tp // tm
        y2 = _gather_tokens(ys, [p[part * tp:(part + 1) * tp] for p in pos])
        row = lambda i, r0=r0: (i + r0, 0)
        in_specs = [pl.BlockSpec((tm, d), row), pl.BlockSpec((tm, y2.shape[1]), lambda i: (i, 0)),
                    pl.BlockSpec((tm, LANES), row),
                    pl.BlockSpec((None, tm, pd), lambda i, r0=r0: (layer, i + r0, 0)),
                    _const_spec((1, d)), _const_spec((d, d)), _const_spec((pd, d)), _const_spec((1, d))]
        args = [h, y2, route, p_all, ple_norm.reshape(1, d), ple_gate.astype(bf16), ple_proj.astype(bf16),
                final_norm.reshape(1, d)]
        aliases = {}
        if out is not None:
            in_specs.append(pl.BlockSpec(memory_space=pl.ANY))
            args.append(out)
            aliases = {len(args) - 1: 0}
        out = pl.pallas_call(
            _final_kernel,
            grid=(tp // tm,),
            in_specs=in_specs,
            out_specs=pl.BlockSpec((tm, d), row),
            out_shape=jax.ShapeDtypeStruct((t, d), f32),
            input_output_aliases=aliases,
            compiler_params=_params("parallel"),
            name="final",
        )(*args)
    return out


def kernel(x, p, pool_norm, pool_w, pool_scale, kv_norm, w_k, w_v, attn_norm, w_q, w_o, ffn_norm, ffn_w1, ffn_w3, ffn_w2, router, exp_w1, exp_w3, exp_w2, ple_norm, ple_gate, ple_proj, final_norm):
    batch, seq, d = x.shape
    t = batch * seq
    assert seq % ROW_TILE == 0 and ROW_TILE % MOBA_BLOCK == 0 and d % HEAD_DIM == 0
    assert p.shape[0] == 2 and router.shape[2] <= LANES
    h = x.reshape(t, d)
    pf = p.reshape(p.shape[0], t, p.shape[-1])

    ew1, ew3, ew2 = _cast_bf16([exp_w1[0], exp_w3[0], exp_w2[0]], EXPERT_CAST_STEPS)
    h = _pool_layer(h, pool_norm[0], pool_w[0], pool_scale[0], seq, ew2[0, :2 * SUBLANES, :LANES])
    h = _swiglu_layer(h, ffn_norm[0], ffn_w1[0], ffn_w3[0], ffn_w2[0])
    h, q, k, v, km = _ple_qkv_layer(h, pf, 0, ple_norm[0], ple_gate[0], ple_proj[0],
                                    kv_norm, w_k, w_v, attn_norm[0], w_q[0], seq)
    attn = _attention(q, k, v, km, batch, seq)
    h, hn, route, route_t = _oproj_router_layer(h, attn, w_o[0], ffn_norm[1], router[0])
    choices = [route_t[k].astype(jnp.int32) for k in range(TOP_K_EXPERTS)]
    ys, pos = _experts_layer(hn, choices, ew1, ew3, ew2)
    out = _final_layer(h, ys, pos, route, pf, 1, ple_norm[1], ple_gate[1],
                       ple_proj[1], final_norm)
    return out.reshape(batch, seq, d)
```

```python
import functools

import jax
import jax.numpy as jnp
from jax import lax
from jax.experimental import pallas as pl
from jax.experimental.pallas import tpu as pltpu
from jax.experimental.pallas import tpu_sc as plsc

POOL_WINDOWS = (2, 4, 8, 16)
HEAD_DIM = 128
MOBA_BLOCK = 256
MOBA_TOPK = 3
ROPE_THETA = 500000.0
ROPE_DIM = HEAD_DIM // 4
TOP_K_EXPERTS = 2
RMS_EPS = 1e-6
NEG_INF = -1e30
REMOVED = -3e38

LANES = 128
SUBLANES = 8
ROW_TILE = 512
POOL_SUB = 128
ATTN_HEADS_PER_STEP = 4
ATTN_BLOCKS_PER_TRIP = 4
SUM_ROWS = 16
LOG2E = 1.4426950408889634
EXPERT_FF_CHUNK = 1792
EXPERT_CAST_STEPS = 32
SC_OVERLAP_PARTS = 2
SC_ROW = 128
SC_WINDOW = 256
VMEM_LIMIT = 56 * 1024 * 1024

bf16 = jnp.bfloat16
f32 = jnp.float32


def _dot(a, b):
    return jnp.dot(a, b, preferred_element_type=f32)


def _dot_split(a, b):
    a_hi = a.astype(bf16)
    a_lo = (a - a_hi.astype(f32)).astype(bf16)
    b_hi = b.astype(bf16)
    b_lo = (b - b_hi.astype(f32)).astype(bf16)
    return _dot(a_hi, b_hi) + (_dot(a_lo, b_hi) + _dot(a_hi, b_lo))


def _pack_pairs(x):
    half = x.shape[1] // 2
    lo = lax.bitcast_convert_type(x[:, :half].astype(bf16).astype(f32), jnp.uint32)
    hi = lax.bitcast_convert_type(x[:, half:].astype(bf16).astype(f32), jnp.uint32)
    return (hi & jnp.uint32(0xFFFF0000)) | (lo >> 16)


def _unpack_pairs(w):
    lo = lax.bitcast_convert_type(w << 16, f32)
    hi = lax.bitcast_convert_type(w & jnp.uint32(0xFFFF0000), f32)
    return jnp.concatenate([lo, hi], axis=1)


def _cast_kernel(*refs):
    n = len(refs) // 2
    for src, dst in zip(refs[:n], refs[n:]):
        dst[...] = src[...].astype(bf16)


def _cast_bf16(ws, steps):
    flat = [w.reshape(-1, w.shape[-1]) for w in ws]
    for a in flat:
        assert a.shape[0] % steps == 0 and (a.shape[0] // steps) % (2 * SUBLANES) == 0
    specs = [pl.BlockSpec((a.shape[0] // steps, a.shape[1]), lambda s: (s, 0)) for a in flat]
    outs = pl.pallas_call(
        _cast_kernel,
        grid=(steps,),
        in_specs=specs,
        out_specs=specs,
        out_shape=[jax.ShapeDtypeStruct(a.shape, bf16) for a in flat],
        compiler_params=_params("parallel"),
        name="cast_bf16",
    )(*flat)
    return [o.reshape(w.shape) for o, w in zip(outs, ws)]


def _rms(x, g):
    var = jnp.mean(x * x, axis=-1, keepdims=True)
    return x * lax.rsqrt(var + RMS_EPS) * g


def _params(*sem):
    return pltpu.CompilerParams(dimension_semantics=sem, vmem_limit_bytes=VMEM_LIMIT)


def _const_spec(shape):
    nd = len(shape)
    return pl.BlockSpec(shape, lambda *_: (0,) * nd)


def _pool_kernel(x_ref, halo_ref, g_ref, pw_ref, ps_ref, after_ref, o_ref, pooled_ref, *, ts, seq):
    del after_ref
    i = pl.program_id(0)
    g = g_ref[...]
    x = x_ref[...]
    xn = _rms(x, g)
    keep = jnp.where((i * ts) % seq == 0, 0.0, 1.0)
    hnb = (_rms(halo_ref[...], g) * keep).astype(bf16)
    xnb = xn.astype(bf16)
    gd = x.shape[1] // len(POOL_WINDOWS)
    r = lax.broadcasted_iota(jnp.int32, (POOL_SUB, 2 * POOL_SUB), 0)
    c = lax.broadcasted_iota(jnp.int32, (POOL_SUB, 2 * POOL_SUB), 1)
    dist = r + POOL_SUB - c
    bands = [((dist >= 0) & (dist < w)).astype(bf16) for w in POOL_WINDOWS]
    rows = lax.broadcasted_iota(jnp.int32, (POOL_SUB, 1), 0)
    for sb in range(ts // POOL_SUB):
        lo, hi = sb * POOL_SUB, (sb + 1) * POOL_SUB
        prev = hnb if sb == 0 else xnb[lo - POOL_SUB:lo]
        ext = jnp.concatenate([prev, xnb[lo:hi]], axis=0)
        tpos = (i * ts + lo) % seq + rows
        for gi, w in enumerate(POOL_WINDOWS):
            cs = slice(gi * gd, (gi + 1) * gd)
            wsum = _dot(bands[gi], ext[:, cs])
            inv_cnt = 1.0 / jnp.minimum(tpos + 1, w).astype(f32)
            pooled_ref[lo:hi, cs] = (wsum * inv_cnt - xn[lo:hi, cs]).astype(bf16)
    for gi in range(len(POOL_WINDOWS)):
        cs = slice(gi * gd, (gi + 1) * gd)
        mixed = _dot(pooled_ref[:, cs], pw_ref[gi])
        o_ref[:, cs] = x[:, cs] + mixed * ps_ref[:, cs]


def _pool_layer(h, norm, pool_w, pool_scale, seq, after):
    t, d = h.shape
    ts = ROW_TILE
    ng = len(POOL_WINDOWS)
    gd = d // ng
    per = ts // POOL_SUB
    return pl.pallas_call(
        functools.partial(_pool_kernel, ts=ts, seq=seq),
        grid=(t // ts,),
        in_specs=[
            pl.BlockSpec((ts, d), lambda i: (i, 0)),
            pl.BlockSpec((POOL_SUB, d), lambda i: (jnp.maximum(i * per - 1, 0), 0)),
            _const_spec((1, d)),
            _const_spec((ng, gd, gd)),
            _const_spec((1, d)),
            _const_spec(after.shape),
        ],
        out_specs=pl.BlockSpec((ts, d), lambda i: (i, 0)),
        out_shape=jax.ShapeDtypeStruct((t, d), f32),
        scratch_shapes=[pltpu.VMEM((ts, d), bf16)],
        compiler_params=_params("parallel"),
        name="pool",
    )(h, h, norm.reshape(1, d), pool_w.astype(bf16), pool_scale.reshape(1, d), after)


def _swiglu_kernel(h_ref, g_ref, w1_ref, w3_ref, w2_ref, o_ref):
    x = h_ref[...]
    hn = _rms(x, g_ref[...]).astype(bf16)
    a = _dot(hn, w1_ref[...])
    b = _dot(hn, w3_ref[...])
    o_ref[...] = x + _dot((jax.nn.silu(a) * b).astype(bf16), w2_ref[...])


def _ff_chunk(ff, target):
    units = ff // LANES
    best = 1
    for k in range(1, units + 1):
        if units % k == 0 and k * LANES <= target:
            best = k
    return best * LANES


def _swiglu_layer(h, norm, w1, w3, w2):
    t, d = h.shape
    ff = w1.shape[1]
    tm = ROW_TILE
    w1r, w3r, w2r = w1.astype(bf16), w3.astype(bf16), w2.astype(bf16)
    resident = lambda shape: pl.BlockSpec(shape, lambda i: (0, 0), pipeline_mode=pl.Buffered(1))
    return pl.pallas_call(
        _swiglu_kernel,
        grid=(t // tm,),
        in_specs=[
            pl.BlockSpec((tm, d), lambda i: (i, 0)),
            _const_spec((1, d)),
            resident((d, ff)), resident((d, ff)), resident((ff, d)),
        ],
        out_specs=pl.BlockSpec((tm, d), lambda i: (i, 0)),
        out_shape=jax.ShapeDtypeStruct((t, d), f32),
        compiler_params=_params("parallel"),
        name="swiglu",
    )(h, norm.reshape(1, d), w1r, w3r, w2r)


def _rope_tables(seq):
    half = ROPE_DIM // 2
    inv_freq = jnp.float32(ROPE_THETA) ** (-(jnp.arange(0, ROPE_DIM, 2, dtype=f32) / ROPE_DIM))
    ang = jnp.arange(seq, dtype=f32)[:, None] * inv_freq[None, :]
    cos, sin = jnp.cos(ang), jnp.sin(ang)
    ones = jnp.ones((seq, HEAD_DIM - ROPE_DIM), f32)
    zeros = jnp.zeros((seq, HEAD_DIM - half), f32)
    cos_t = jnp.concatenate([cos, cos, ones], axis=1)
    up_t = jnp.concatenate([-sin, zeros], axis=1)
    dn_t = jnp.concatenate([jnp.zeros((seq, half), f32), sin, jnp.zeros((seq, HEAD_DIM - ROPE_DIM), f32)], axis=1)
    return cos_t, up_t, dn_t


def _rope(xh, cos_t, up_t, dn_t):
    half = ROPE_DIM // 2
    return (xh * cos_t + pltpu.roll(xh, HEAD_DIM - half, 1) * up_t + pltpu.roll(xh, half, 1) * dn_t)


def _ple_qkv_kernel(h_ref, p_ref, pn_ref, wg_ref, wp_ref, kvn_ref, wk_ref, wv_ref, an_ref, wq_ref,
                    cos_ref, up_ref, dn_ref, h_out, q_out, k_out, v_out, km_out):
    h = h_ref[...]
    hn = _rms(h, pn_ref[...]).astype(bf16)
    h3 = h + jax.nn.sigmoid(_dot(hn, wg_ref[...])) * _dot(p_ref[...].astype(bf16), wp_ref[...])
    h_out[...] = h3
    base = h3 * lax.rsqrt(jnp.mean(h3 * h3, axis=-1, keepdims=True) + RMS_EPS)
    kn = (base * kvn_ref[...]).astype(bf16)
    qn = (base * an_ref[...]).astype(bf16)
    v = _dot(kn, wv_ref[...])
    for bi in range(h.shape[0] // MOBA_BLOCK):
        v_out[bi] = v[bi * MOBA_BLOCK:(bi + 1) * MOBA_BLOCK].T.astype(bf16)
    k = _dot(kn, wk_ref[...])
    q = _dot(qn, wq_ref[...])
    cos_t, up_t, dn_t = cos_ref[...], up_ref[...], dn_ref[...]
    tm, d = h.shape
    sub = km_out.shape[0] // (tm // MOBA_BLOCK)
    for hh in range(d // HEAD_DIM):
        cs = slice(hh * HEAD_DIM, (hh + 1) * HEAD_DIM)
        q_out[:, cs] = _rope(q[:, cs], cos_t, up_t, dn_t)
        kr = _rope(k[:, cs], cos_t, up_t, dn_t)
        k_out[:, cs] = kr.astype(bf16)
        for bi in range(tm // MOBA_BLOCK):
            m = jnp.mean(kr[bi * MOBA_BLOCK:(bi + 1) * MOBA_BLOCK], axis=0, keepdims=True)
            km_out[bi * sub:(bi + 1) * sub, cs] = jnp.broadcast_to(m, (sub, HEAD_DIM))


def _layer_rows_spec(p_all, layer, tm):
    return pl.BlockSpec((None, tm, p_all.shape[2]), lambda i: (layer, i, 0))


def _ple_qkv_layer(h, p_all, layer, ple_norm, ple_gate, ple_proj, kv_norm, w_k, w_v, attn_norm, w_q, seq):
    t, d = h.shape
    pd = p_all.shape[2]
    tm = ROW_TILE
    sub = SUBLANES
    cos_t, up_t, dn_t = _rope_tables(seq)
    tiles_per_seq = seq // tm
    row = lambda i: (i, 0)
    tab = pl.BlockSpec((tm, HEAD_DIM), lambda i: (i % tiles_per_seq, 0))
    nkm = t // MOBA_BLOCK * sub
    outs = pl.pallas_call(
        _ple_qkv_kernel,
        grid=(t // tm,),
        in_specs=[
            pl.BlockSpec((tm, d), row), _layer_rows_spec(p_all, layer, tm),
            _const_spec((1, d)), _const_spec((d, d)), _const_spec((pd, d)),
            _const_spec((1, d)), _const_spec((d, d)), _const_spec((d, d)),
            _const_spec((1, d)), _const_spec((d, d)),
            tab, tab, tab,
        ],
        out_specs=[
            pl.BlockSpec((tm, d), row), pl.BlockSpec((tm, d), row),
            pl.BlockSpec((tm, d), row),
            pl.BlockSpec((tm // MOBA_BLOCK, d, MOBA_BLOCK), lambda i: (i, 0, 0)),
            pl.BlockSpec((tm // MOBA_BLOCK * sub, d), row),
        ],
        out_shape=[
            jax.ShapeDtypeStruct((t, d), f32), jax.ShapeDtypeStruct((t, d), f32),
            jax.ShapeDtypeStruct((t, d), bf16),
            jax.ShapeDtypeStruct((t // MOBA_BLOCK, d, MOBA_BLOCK), bf16),
            jax.ShapeDtypeStruct((nkm, d), f32),
        ],
        compiler_params=_params("parallel"),
        name="ple_qkv",
    )(h, p_all, ple_norm.reshape(1, d), ple_gate.astype(bf16), ple_proj.astype(bf16),
      kv_norm.reshape(1, d), w_k.astype(bf16), w_v.astype(bf16),
      attn_norm.reshape(1, d), w_q.astype(bf16), cos_t, up_t, dn_t)
    h3, q, k, v, km = outs
    return h3, q, k, v, km.reshape(t // MOBA_BLOCK, sub, d)[:, 0, :]


def _attn_kernel(q_ref, k_ref, vt_ref, km_ref, o_ref, qa_ref, sa_ref, sb_ref, pa_ref, pb_ref, acc_ref, *, hg):
    j = pl.program_id(2)
    nb = km_ref.shape[0]
    bs = q_ref.shape[0]
    hd = HEAD_DIM
    qscale = (hd ** -0.5) * LOG2E
    blk = lax.broadcasted_iota(jnp.int32, (nb, bs), 0).astype(f32)
    jf = j.astype(f32)
    krow = lax.broadcasted_iota(jnp.int32, (bs, bs), 0)
    qcol = lax.broadcasted_iota(jnp.int32, (bs, bs), 1)
    start_j = pl.multiple_of(j * bs, bs)

    ones_rows = jnp.ones((SUM_ROWS, bs), bf16)

    def weighted_values(n, h, pb):
        return _dot(jnp.concatenate([vt_ref[n, h * hd:(h + 1) * hd, :], ones_rows], axis=0), pb)

    heads = [slice(h * hd, (h + 1) * hd) for h in range(hg)]
    qfts = [q_ref[:, cs].T for cs in heads]
    qbts = [(qft * qscale).astype(bf16) for qft in qfts]
    diag = [_dot(k_ref[pl.ds(start_j, bs), cs], qbt) for cs, qbt in zip(heads, qbts)]
    gates = [_dot_split(km_ref[:, cs], qft) for cs, qft in zip(heads, qfts)]
    cands = [jnp.where(blk < jf, gate, NEG_INF) for gate in gates]
    sels = [blk < 0.0] * hg
    for _ in range(min(MOBA_TOPK, nb)):
        for h in range(hg):
            mx = jnp.max(cands[h], axis=0, keepdims=True)
            pick = jnp.min(jnp.where(cands[h] == mx, blk, float(nb)), axis=0, keepdims=True)
            hit = blk == pick
            sels[h] = sels[h] | hit
            cands[h] = jnp.where(hit, REMOVED, cands[h])
    init = []
    for h in range(hg):
        bias = jnp.where(sels[h] & (blk < jf), 0.0, NEG_INF)
        if nb < hd:
            bias = jnp.concatenate([bias, jnp.zeros((hd - nb, bs), f32)], axis=0)
        qa_ref[h] = jnp.concatenate([qbts[h], bias.astype(bf16)], axis=0)
        s = jnp.where(krow <= qcol, diag[h], NEG_INF)
        m0 = jnp.max(s, axis=0, keepdims=True)
        acc_ref[h] = weighted_values(j, h, jnp.exp2((s - m0).astype(bf16)))
        init += [m0, jnp.ones_like(m0)]

    lane = lax.broadcasted_iota(jnp.int32, (bs, hd), 1)

    def block_scores(n, h):
        start = pl.multiple_of(n * bs, bs)
        onehot = (lane == n).astype(bf16)
        ka = jnp.concatenate([k_ref[pl.ds(start, bs), h * hd:(h + 1) * hd], onehot], axis=1)
        return _dot(ka, qa_ref[h])

    def stage(n, s_in, s_out, p_prev, p_out, stats):
        nn = jnp.minimum(n + 1, nb - 1)
        prev = jnp.clip(n - 1, 0, nb - 1)
        new = []
        for h in range(hg):
            m, alpha_p = stats[2 * h:2 * h + 2]
            acc_ref[h] = alpha_p * acc_ref[h] + weighted_values(prev, h, p_prev[h])
            s = s_in[h]
            m_new = jnp.maximum(m, jnp.max(s, axis=0, keepdims=True))
            p_out[h] = jnp.exp2((s - m_new).astype(bf16))
            s_out[h] = block_scores(nn, h)
            new += [m_new, jnp.exp2(m - m_new)]
        return new

    per_trip = ATTN_BLOCKS_PER_TRIP

    def body(i, stats):
        for u in range(0, per_trip, 2):
            stats = stage(per_trip * i + u, sa_ref, sb_ref, pb_ref, pa_ref, stats)
            stats = stage(per_trip * i + u + 1, sb_ref, sa_ref, pa_ref, pb_ref, stats)
        return tuple(stats)

    for h in range(hg):
        sa_ref[h] = block_scores(0, h)
        pb_ref[h] = jnp.zeros((bs, bs), bf16)
    trips = (j + per_trip - 1) // per_trip
    res = lax.fori_loop(0, trips, body, tuple(init))
    last = jnp.clip(per_trip * trips - 1, 0, nb - 1)
    tails = [weighted_values(last, h, pb_ref[h]) for h in range(hg)]
    for h in range(hg):
        acc = res[2 * h + 1] * acc_ref[h] + tails[h]
        o_ref[:, h * hd:(h + 1) * hd] = (acc[:hd] * (1.0 / acc[hd:hd + 1])).T.astype(o_ref.dtype)


def _attention(q, k, vt, km, batch, seq):
    t, d = q.shape
    nh = d // HEAD_DIM
    nb = seq // MOBA_BLOCK
    hg = ATTN_HEADS_PER_STEP
    assert nb <= HEAD_DIM and nh % hg == 0
    w = hg * HEAD_DIM
    return pl.pallas_call(
        functools.partial(_attn_kernel, hg=hg),
        grid=(batch, nh // hg, nb),
        in_specs=[
            pl.BlockSpec((MOBA_BLOCK, w), lambda b, g, j: (b * nb + j, g)),
            pl.BlockSpec((seq, w), lambda b, g, j: (b, g), pipeline_mode=pl.Buffered(1)),
            pl.BlockSpec((nb, w, MOBA_BLOCK), lambda b, g, j: (b, g, 0), pipeline_mode=pl.Buffered(1)),
            pl.BlockSpec((nb, w), lambda b, g, j: (b, g)),
        ],
        out_specs=pl.BlockSpec((MOBA_BLOCK, w), lambda b, g, j: (b * nb + j, g)),
        out_shape=jax.ShapeDtypeStruct((t, d), bf16),
        scratch_shapes=[
            pltpu.VMEM((hg, 2 * HEAD_DIM, MOBA_BLOCK), bf16),
            pltpu.VMEM((hg, MOBA_BLOCK, MOBA_BLOCK), f32), pltpu.VMEM((hg, MOBA_BLOCK, MOBA_BLOCK), f32),
            pltpu.VMEM((hg, MOBA_BLOCK, MOBA_BLOCK), bf16), pltpu.VMEM((hg, MOBA_BLOCK, MOBA_BLOCK), bf16),
            pltpu.VMEM((hg, HEAD_DIM + SUM_ROWS, MOBA_BLOCK), f32),
        ],
        compiler_params=_params("parallel", "parallel", "arbitrary"),
        name="attn",
    )(q, k, vt, km)


def _oproj_router_kernel(h_ref, a_ref, wo_ref, g_ref, r_ref, h_out, hn_out, route_out, route_t_out, *, n_exp):
    h4 = h_ref[...] + _dot(a_ref[...], wo_ref[...])
    h_out[...] = h4
    hn = _rms(h4, g_ref[...])
    hn_out[...] = _pack_pairs(hn)
    logits = _dot_split(hn, r_ref[...])
    lane = lax.broadcasted_iota(jnp.int32, logits.shape, 1).astype(f32)
    cand = jnp.where(lane < n_exp, logits, NEG_INF)
    m1 = jnp.max(cand, axis=1, keepdims=True)
    i1 = jnp.min(jnp.where(cand == m1, lane, float(LANES)), axis=1, keepdims=True)
    cand = jnp.where(lane == i1, REMOVED, cand)
    m2 = jnp.max(cand, axis=1, keepdims=True)
    i2 = jnp.min(jnp.where(cand == m2, lane, float(LANES)), axis=1, keepdims=True)
    e2 = jnp.exp(m2 - m1)
    den = 1.0 + e2
    route = jnp.where(lane == 0, i1, jnp.where(lane == 1, i2, jnp.where(
        lane == 2, 1.0 / den, jnp.where(lane == 3, e2 / den, 0.0))))
    route_out[...] = route
    route_t_out[...] = route.T[:SUBLANES]


def _oproj_router_layer(h, attn, w_o, norm, router):
    t, d = h.shape
    tm = ROW_TILE
    n_exp = router.shape[1]
    r_pad = jnp.zeros((d, LANES), f32).at[:, :n_exp].set(router)
    row = lambda i: (i, 0)
    return pl.pallas_call(
        functools.partial(_oproj_router_kernel, n_exp=n_exp),
        grid=(t // tm,),
        in_specs=[pl.BlockSpec((tm, d), row), pl.BlockSpec((tm, d), row), _const_spec((d, d)),
                  _const_spec((1, d)), _const_spec((d, LANES))],
        out_specs=[pl.BlockSpec((tm, d), row), pl.BlockSpec((tm, d // 2), row), pl.BlockSpec((tm, LANES), row),
                   pl.BlockSpec((SUBLANES, tm), lambda i: (0, i))],
        out_shape=[jax.ShapeDtypeStruct((t, d), f32), jax.ShapeDtypeStruct((t, d // 2), jnp.uint32),
                   jax.ShapeDtypeStruct((t, LANES), f32), jax.ShapeDtypeStruct((SUBLANES, t), f32)],
        compiler_params=_params("parallel"),
        name="oproj_router",
    )(h, attn, w_o.astype(bf16), norm.reshape(1, d), r_pad)


def _sc_gather(x, idx):
    n = idx.shape[0]
    assert x.shape[1] == SC_ROW and n % SC_WINDOW == 0
    mesh = plsc.VectorSubcoreMesh(core_axis_name="core", subcore_axis_name="subcore")

    @pl.kernel(out_type=jax.ShapeDtypeStruct((n, SC_ROW), x.dtype), mesh=mesh, scratch_types=[])
    def gather(x_hbm, i_hbm, o_hbm):
        def body(i_vmem, o_vmem):
            pltpu.sync_copy(x_hbm.at[i_vmem.at[0]], o_vmem)

        pltpu.emit_pipeline(
            body,
            grid=(n // SC_WINDOW,),
            in_specs=[pl.BlockSpec((1, SC_WINDOW), index_map=lambda i: (0, i))],
            out_specs=[pl.BlockSpec((SC_WINDOW, SC_ROW), index_map=lambda i: (i, 0))],
            core_axis_name=("core", "subcore"),
            dimension_semantics=(pltpu.PARALLEL,),
        )(i_hbm, o_hbm)

    return gather(x, idx.reshape(n // LANES, LANES).reshape(1, n))


def _pieces(a):
    n, d = a.shape
    return a.reshape(n // SUBLANES, SUBLANES, d // SC_ROW, SC_ROW).transpose(0, 2, 1, 3).reshape(-1, SC_ROW)


def _unpieces(p, d):
    per = d // SC_ROW
    n = p.shape[0] // per
    return p.reshape(n // SUBLANES, per, SUBLANES, SC_ROW).transpose(0, 2, 1, 3).reshape(n, d)


def _gather_tokens(x, rows):
    n, d = x.shape
    m = rows[0].shape[0]
    per = d // SC_ROW
    assert n % SUBLANES == 0 and m % SUBLANES == 0
    tiled = []
    for r in rows:
        first = (r // SUBLANES * (SUBLANES * per) + r % SUBLANES).reshape(m // SUBLANES, SUBLANES)
        tiled.append(jnp.tile(first, (1, per)))
    chunk = jnp.arange(per * SUBLANES, dtype=jnp.int32) // SUBLANES * SUBLANES
    idx = jnp.concatenate(tiled, axis=1) + jnp.tile(chunk, len(rows))[None, :]
    return _unpieces(_sc_gather(_pieces(x), idx.reshape(-1)), len(rows) * d)


def _expert_kernel(te_ref, tv_ref, x_ref, w1_ref, w3_ref, w2_ref, *rest, nf, first_tile):
    o_ref, xb_ref, acc_ref = rest[-3:]
    i = pl.program_id(0) + first_tile
    f = pl.program_id(1)
    valid = tv_ref[i] > 0

    @pl.when(f == 0)
    def _():
        xb_ref[...] = _unpack_pairs(x_ref[...]).astype(bf16)
        acc_ref[...] = jnp.zeros_like(acc_ref)

    @pl.when(valid)
    def _():
        x = xb_ref[...]
        a = _dot(x, w1_ref[...])
        b = _dot(x, w3_ref[...])
        acc_ref[...] += _dot((jax.nn.silu(a) * b).astype(bf16), w2_ref[...])

    @pl.when(f == nf - 1)
    def _():
        o_ref[...] = _pack_pairs(acc_ref[...])


def _route_tables(choices, n_exp, tm):
    t = choices[0].shape[0]
    nslots = len(choices) * t
    experts = jnp.arange(n_exp, dtype=jnp.int32)[:, None]
    onehots = [(c[None, :] == experts).astype(jnp.int32) for c in choices]
    cums = [jnp.cumsum(oh, axis=1) for oh in onehots]
    totals = [cu[:, -1] for cu in cums]
    counts = sum(totals)
    tiles_e = (counts + tm - 1) // tm
    tile_end = jnp.cumsum(tiles_e)
    tile_start = tile_end - tiles_e
    group_start = jnp.cumsum(counts) - counts
    nt = nslots // tm + n_exp
    total = tile_end[-1]
    ti = jnp.arange(nt, dtype=jnp.int32)
    tv = (ti < total).astype(jnp.int32)
    tc = jnp.minimum(ti, total - 1)
    te = jnp.minimum(jnp.sum((tc[:, None] >= tile_end[None, :]).astype(jnp.int32), axis=1), n_exp - 1)
    order = jnp.argsort(jnp.concatenate(choices), stable=True).astype(jnp.int32)
    lane = jnp.arange(tm, dtype=jnp.int32)[None, :]
    rank = ((tc - tile_start[te]) * tm)[:, None] + lane
    real = (rank < counts[te][:, None]) & (tv[:, None] > 0)
    sidx = jnp.clip(group_start[te][:, None] + rank, 0, nslots - 1)
    src = jnp.where(real, order[sidx] % t, (ti[:, None] * tm + lane) % t).reshape(-1)
    pos = []
    earlier = jnp.zeros((n_exp,), jnp.int32)
    for oh, cu, tot in zip(onehots, cums, totals):
        pos.append(jnp.sum(oh * ((tile_start * tm + earlier)[:, None] + cu - oh), axis=0))
        earlier = earlier + tot
    return te, tv, src, pos, nt


def _experts_layer(hn, choices, w1, w3, w2):
    n_exp, d, ff = w1.shape
    tm = ROW_TILE
    fc = _ff_chunk(ff, EXPERT_FF_CHUNK)
    nf = ff // fc
    te, tv, src, pos, nt = _route_tables(choices, n_exp, tm)
    assert nt % SC_OVERLAP_PARTS == 0
    ntp = nt // SC_OVERLAP_PARTS

    ys = None
    for part in range(SC_OVERLAP_PARTS):
        t0 = part * ntp
        xs = _gather_tokens(hn, [src[t0 * tm:(t0 + ntp) * tm]])

        def fsel(i, f, tv_r, t0=t0):
            return jnp.where(tv_r[i + t0] > 0, f, nf - 1)

        def wmap_in(i, f, te_r, tv_r, t0=t0, fsel=fsel):
            return (te_r[i + t0], 0, fsel(i, f, tv_r))

        def wmap_out(i, f, te_r, tv_r, t0=t0, fsel=fsel):
            return (te_r[i + t0], fsel(i, f, tv_r), 0)

        in_specs = [
            pl.BlockSpec((tm, d // 2), lambda i, f, *_: (i, 0)),
            pl.BlockSpec((None, d, fc), wmap_in),
            pl.BlockSpec((None, d, fc), wmap_in),
            pl.BlockSpec((None, fc, d), wmap_out),
        ]
        args = [te, tv, xs, w1, w3, w2]
        aliases = {}
        if ys is not None:
            in_specs.append(pl.BlockSpec(memory_space=pl.ANY))
            args.append(ys)
            aliases = {len(args) - 1: 0}
        ys = pl.pallas_call(
            functools.partial(_expert_kernel, nf=nf, first_tile=t0),
            grid_spec=pltpu.PrefetchScalarGridSpec(
                num_scalar_prefetch=2,
                grid=(ntp, nf),
                in_specs=in_specs,
                out_specs=pl.BlockSpec((tm, d // 2), lambda i, f, *_, t0=t0: (i + t0, 0)),
                scratch_shapes=[pltpu.VMEM((tm, d), bf16), pltpu.VMEM((tm, d), f32)],
            ),
            out_shape=jax.ShapeDtypeStruct((nt * tm, d // 2), jnp.uint32),
            input_output_aliases=aliases,
            compiler_params=_params("parallel", "arbitrary"),
            name="experts",
        )(*args)
    return ys, pos


def _final_kernel(h_ref, y_ref, route_ref, p_ref, pn_ref, wg_ref, wp_ref, fn_ref, *rest):
    o_ref = rest[-1]
    d = h_ref.shape[1]
    route = route_ref[...]
    y = y_ref[...]
    h5 = h_ref[...] + route[:, 2:3] * _unpack_pairs(y[:, :d // 2]) + route[:, 3:4] * _unpack_pairs(y[:, d // 2:])
    hn = _rms(h5, pn_ref[...]).astype(bf16)
    h6 = h5 + jax.nn.sigmoid(_dot(hn, wg_ref[...])) * _dot(p_ref[...].astype(bf16), wp_ref[...])
    o_ref[...] = _rms(h6, fn_ref[...])


def _final_layer(h, ys, pos, route, p_all, layer, ple_norm, ple_gate, ple_proj, final_norm):
    t, d = h.shape
    pd = p_all.shape[2]
    tm = ROW_TILE
    assert t % (SC_OVERLAP_PARTS * tm) == 0
    tp = t // SC_OVERLAP_PARTS
    out = None
    for part in range(SC_OVERLAP_PARTS):
        r0 = part * tp // tm
        y2 = _gather_tokens(ys, [p[part * tp:(part + 1) * tp] for p in pos])
        row = lambda i, r0=r0: (i + r0, 0)
        in_specs = [pl.BlockSpec((tm, d), row), pl.BlockSpec((tm, y2.shape[1]), lambda i: (i, 0)),
                    pl.BlockSpec((tm, LANES), row),
                    pl.BlockSpec((None, tm, pd), lambda i, r0=r0: (layer, i + r0, 0)),
                    _const_spec((1, d)), _const_spec((d, d)), _const_spec((pd, d)), _const_spec((1, d))]
        args = [h, y2, route, p_all, ple_norm.reshape(1, d), ple_gate.astype(bf16), ple_proj.astype(bf16),
                final_norm.reshape(1, d)]
        aliases = {}
        if out is not None:
            in_specs.append(pl.BlockSpec(memory_space=pl.ANY))
            args.append(out)
            aliases = {len(args) - 1: 0}
        out = pl.pallas_call(
            _final_kernel,
            grid=(tp // tm,),
            in_specs=in_specs,
            out_specs=pl.BlockSpec((tm, d), row),
            out_shape=jax.ShapeDtypeStruct((t, d), f32),
            input_output_aliases=aliases,
            compiler_params=_params("parallel"),
            name="final",
        )(*args)
    return out


def kernel(x, p, pool_norm, pool_w, pool_scale, kv_norm, w_k, w_v, attn_norm, w_q, w_o, ffn_norm, ffn_w1, ffn_w3, ffn_w2, router, exp_w1, exp_w3, exp_w2, ple_norm, ple_gate, ple_proj, final_norm):
    batch, seq, d = x.shape
    t = batch * seq
    assert seq % ROW_TILE == 0 and ROW_TILE % MOBA_BLOCK == 0 and d % HEAD_DIM == 0
    assert p.shape[0] == 2 and router.shape[2] <= LANES
    h = x.reshape(t, d)
    pf = p.reshape(p.shape[0], t, p.shape[-1])

    ew1, ew3, ew2 = _cast_bf16([exp_w1[0], exp_w3[0], exp_w2[0]], EXPERT_CAST_STEPS)
    h = _pool_layer(h, pool_norm[0], pool_w[0], pool_scale[0], seq, ew2[0, :2 * SUBLANES, :LANES])
    h = _swiglu_layer(h, ffn_norm[0], ffn_w1[0], ffn_w3[0], ffn_w2[0])
    h, q, k, v, km = _ple_qkv_layer(h, pf, 0, ple_norm[0], ple_gate[0], ple_proj[0],
                                    kv_norm, w_k, w_v, attn_norm[0], w_q[0], seq)
    attn = _attention(q, k, v, km, batch, seq)
    h, hn, route, route_t = _oproj_router_layer(h, attn, w_o[0], ffn_norm[1], router[0])
    choices = [route_t[k].astype(jnp.int32) for k in range(TOP_K_EXPERTS)]
    ys, pos = _experts_layer(hn, choices, ew1, ew3, ew2)
    out = _final_layer(h, ys, pos, route, pf, 1, ple_norm[1], ple_gate[1],
                       ple_proj[1], final_norm)
    return out.reshape(batch, seq, d)
```

```python
import functools

import jax
import jax.numpy as jnp
from jax import lax
from jax.experimental import pallas as pl
from jax.experimental.pallas import tpu as pltpu
from jax.experimental.pallas import tpu_sc as plsc

POOL_WINDOWS = (2, 4, 8, 16)
HEAD_DIM = 128
MOBA_BLOCK = 256
MOBA_TOPK = 3
ROPE_THETA = 500000.0
ROPE_DIM = HEAD_DIM // 4
TOP_K_EXPERTS = 2
RMS_EPS = 1e-6
NEG_INF = -1e30
REMOVED = -3e38

LANES = 128
SUBLANES = 8
ROW_TILE = 512
POOL_SUB = 128
ATTN_HEADS_PER_STEP = 4
ATTN_BLOCKS_PER_TRIP = 4
SUM_ROWS = 16
LOG2E = 1.4426950408889634
EXPERT_FF_CHUNK = 1792
EXPERT_CAST_STEPS = 32
SC_OVERLAP_PARTS = 4
SC_ROW = 128
SC_WINDOW = 128
VMEM_LIMIT = 56 * 1024 * 1024

bf16 = jnp.bfloat16
f32 = jnp.float32


def _dot(a, b):
    return jnp.dot(a, b, preferred_element_type=f32)


def _dot_split(a, b):
    a_hi = a.astype(bf16)
    a_lo = (a - a_hi.astype(f32)).astype(bf16)
    b_hi = b.astype(bf16)
    b_lo = (b - b_hi.astype(f32)).astype(bf16)
    return _dot(a_hi, b_hi) + (_dot(a_lo, b_hi) + _dot(a_hi, b_lo))


def _pack_pairs(x):
    half = x.shape[1] // 2
    lo = lax.bitcast_convert_type(x[:, :half].astype(bf16).astype(f32), jnp.uint32)
    hi = lax.bitcast_convert_type(x[:, half:].astype(bf16).astype(f32), jnp.uint32)
    return (hi & jnp.uint32(0xFFFF0000)) | (lo >> 16)


def _unpack_pairs(w):
    lo = lax.bitcast_convert_type(w << 16, f32)
    hi = lax.bitcast_convert_type(w & jnp.uint32(0xFFFF0000), f32)
    return jnp.concatenate([lo, hi], axis=1)


def _cast_kernel(*refs):
    n = len(refs) // 2
    for src, dst in zip(refs[:n], refs[n:]):
        dst[...] = src[...].astype(bf16)


def _cast_bf16(ws, steps):
    flat = [w.reshape(-1, w.shape[-1]) for w in ws]
    for a in flat:
        assert a.shape[0] % steps == 0 and (a.shape[0] // steps) % (2 * SUBLANES) == 0
    specs = [pl.BlockSpec((a.shape[0] // steps, a.shape[1]), lambda s: (s, 0)) for a in flat]
    outs = pl.pallas_call(
        _cast_kernel,
        grid=(steps,),
        in_specs=specs,
        out_specs=specs,
        out_shape=[jax.ShapeDtypeStruct(a.shape, bf16) for a in flat],
        compiler_params=_params("parallel"),
        name="cast_bf16",
    )(*flat)
    return [o.reshape(w.shape) for o, w in zip(outs, ws)]


def _rms(x, g):
    var = jnp.mean(x * x, axis=-1, keepdims=True)
    return x * lax.rsqrt(var + RMS_EPS) * g


def _params(*sem):
    return pltpu.CompilerParams(dimension_semantics=sem, vmem_limit_bytes=VMEM_LIMIT)


def _const_spec(shape):
    nd = len(shape)
    return pl.BlockSpec(shape, lambda *_: (0,) * nd)


def _pool_kernel(x_ref, halo_ref, g_ref, pw_ref, ps_ref, after_ref, o_ref, pooled_ref, *, ts, seq):
    del after_ref
    i = pl.program_id(0)
    g = g_ref[...]
    x = x_ref[...]
    xn = _rms(x, g)
    keep = jnp.where((i * ts) % seq == 0, 0.0, 1.0)
    hnb = (_rms(halo_ref[...], g) * keep).astype(bf16)
    xnb = xn.astype(bf16)
    gd = x.shape[1] // len(POOL_WINDOWS)
    r = lax.broadcasted_iota(jnp.int32, (POOL_SUB, 2 * POOL_SUB), 0)
    c = lax.broadcasted_iota(jnp.int32, (POOL_SUB, 2 * POOL_SUB), 1)
    dist = r + POOL_SUB - c
    bands = [((dist >= 0) & (dist < w)).astype(bf16) for w in POOL_WINDOWS]
    rows = lax.broadcasted_iota(jnp.int32, (POOL_SUB, 1), 0)
    for sb in range(ts // POOL_SUB):
        lo, hi = sb * POOL_SUB, (sb + 1) * POOL_SUB
        prev = hnb if sb == 0 else xnb[lo - POOL_SUB:lo]
        ext = jnp.concatenate([prev, xnb[lo:hi]], axis=0)
        tpos = (i * ts + lo) % seq + rows
        for gi, w in enumerate(POOL_WINDOWS):
            cs = slice(gi * gd, (gi + 1) * gd)
            wsum = _dot(bands[gi], ext[:, cs])
            inv_cnt = 1.0 / jnp.minimum(tpos + 1, w).astype(f32)
            pooled_ref[lo:hi, cs] = (wsum * inv_cnt - xn[lo:hi, cs]).astype(bf16)
    for gi in range(len(POOL_WINDOWS)):
        cs = slice(gi * gd, (gi + 1) * gd)
        mixed = _dot(pooled_ref[:, cs], pw_ref[gi])
        o_ref[:, cs] = x[:, cs] + mixed * ps_ref[:, cs]


def _pool_layer(h, norm, pool_w, pool_scale, seq, after):
    t, d = h.shape
    ts = ROW_TILE
    ng = len(POOL_WINDOWS)
    gd = d // ng
    per = ts // POOL_SUB
    return pl.pallas_call(
        functools.partial(_pool_kernel, ts=ts, seq=seq),
        grid=(t // ts,),
        in_specs=[
            pl.BlockSpec((ts, d), lambda i: (i, 0)),
            pl.BlockSpec((POOL_SUB, d), lambda i: (jnp.maximum(i * per - 1, 0), 0)),
            _const_spec((1, d)),
            _const_spec((ng, gd, gd)),
            _const_spec((1, d)),
            _const_spec(after.shape),
        ],
        out_specs=pl.BlockSpec((ts, d), lambda i: (i, 0)),
        out_shape=jax.ShapeDtypeStruct((t, d), f32),
        scratch_shapes=[pltpu.VMEM((ts, d), bf16)],
        compiler_params=_params("parallel"),
        name="pool",
    )(h, h, norm.reshape(1, d), pool_w.astype(bf16), pool_scale.reshape(1, d), after)


def _swiglu_kernel(h_ref, g_ref, w1_ref, w3_ref, w2_ref, o_ref):
    x = h_ref[...]
    hn = _rms(x, g_ref[...]).astype(bf16)
    a = _dot(hn, w1_ref[...])
    b = _dot(hn, w3_ref[...])
    o_ref[...] = x + _dot((jax.nn.silu(a) * b).astype(bf16), w2_ref[...])


def _ff_chunk(ff, target):
    units = ff // LANES
    best = 1
    for k in range(1, units + 1):
        if units % k == 0 and k * LANES <= target:
            best = k
    return best * LANES


def _swiglu_layer(h, norm, w1, w3, w2):
    t, d = h.shape
    ff = w1.shape[1]
    tm = ROW_TILE
    w1r, w3r, w2r = w1.astype(bf16), w3.astype(bf16), w2.astype(bf16)
    resident = lambda shape: pl.BlockSpec(shape, lambda i: (0, 0), pipeline_mode=pl.Buffered(1))
    return pl.pallas_call(
        _swiglu_kernel,
        grid=(t // tm,),
        in_specs=[
            pl.BlockSpec((tm, d), lambda i: (i, 0)),
            _const_spec((1, d)),
            resident((d, ff)), resident((d, ff)), resident((ff, d)),
        ],
        out_specs=pl.BlockSpec((tm, d), lambda i: (i, 0)),
        out_shape=jax.ShapeDtypeStruct((t, d), f32),
        compiler_params=_params("parallel"),
        name="swiglu",
    )(h, norm.reshape(1, d), w1r, w3r, w2r)


def _rope_tables(seq):
    half = ROPE_DIM // 2
    inv_freq = jnp.float32(ROPE_THETA) ** (-(jnp.arange(0, ROPE_DIM, 2, dtype=f32) / ROPE_DIM))
    ang = jnp.arange(seq, dtype=f32)[:, None] * inv_freq[None, :]
    cos, sin = jnp.cos(ang), jnp.sin(ang)
    ones = jnp.ones((seq, HEAD_DIM - ROPE_DIM), f32)
    zeros = jnp.zeros((seq, HEAD_DIM - half), f32)
    cos_t = jnp.concatenate([cos, cos, ones], axis=1)
    up_t = jnp.concatenate([-sin, zeros], axis=1)
    dn_t = jnp.concatenate([jnp.zeros((seq, half), f32), sin, jnp.zeros((seq, HEAD_DIM - ROPE_DIM), f32)], axis=1)
    return cos_t, up_t, dn_t


def _rope(xh, cos_t, up_t, dn_t):
    half = ROPE_DIM // 2
    return (xh * cos_t + pltpu.roll(xh, HEAD_DIM - half, 1) * up_t + pltpu.roll(xh, half, 1) * dn_t)


def _ple_qkv_kernel(h_ref, p_ref, pn_ref, wg_ref, wp_ref, kvn_ref, wk_ref, wv_ref, an_ref, wq_ref,
                    cos_ref, up_ref, dn_ref, h_out, q_out, k_out, v_out, km_out):
    h = h_ref[...]
    hn = _rms(h, pn_ref[...]).astype(bf16)
    h3 = h + jax.nn.sigmoid(_dot(hn, wg_ref[...])) * _dot(p_ref[...].astype(bf16), wp_ref[...])
    h_out[...] = h3
    base = h3 * lax.rsqrt(jnp.mean(h3 * h3, axis=-1, keepdims=True) + RMS_EPS)
    kn = (base * kvn_ref[...]).astype(bf16)
    qn = (base * an_ref[...]).astype(bf16)
    v = _dot(kn, wv_ref[...])
    for bi in range(h.shape[0] // MOBA_BLOCK):
        v_out[bi] = v[bi * MOBA_BLOCK:(bi + 1) * MOBA_BLOCK].T.astype(bf16)
    k = _dot(kn, wk_ref[...])
    q = _dot(qn, wq_ref[...])
    cos_t, up_t, dn_t = cos_ref[...], up_ref[...], dn_ref[...]
    tm, d = h.shape
    sub = km_out.shape[0] // (tm // MOBA_BLOCK)
    for hh in range(d // HEAD_DIM):
        cs = slice(hh * HEAD_DIM, (hh + 1) * HEAD_DIM)
        q_out[:, cs] = _rope(q[:, cs], cos_t, up_t, dn_t)
        kr = _rope(k[:, cs], cos_t, up_t, dn_t)
        k_out[:, cs] = kr.astype(bf16)
        for bi in range(tm // MOBA_BLOCK):
            m = jnp.mean(kr[bi * MOBA_BLOCK:(bi + 1) * MOBA_BLOCK], axis=0, keepdims=True)
            km_out[bi * sub:(bi + 1) * sub, cs] = jnp.broadcast_to(m, (sub, HEAD_DIM))


def _layer_rows_spec(p_all, layer, tm):
    return pl.BlockSpec((None, tm, p_all.shape[2]), lambda i: (layer, i, 0))


def _ple_qkv_layer(h, p_all, layer, ple_norm, ple_gate, ple_proj, kv_norm, w_k, w_v, attn_norm, w_q, seq):
    t, d = h.shape
    pd = p_all.shape[2]
    tm = ROW_TILE
    sub = SUBLANES
    cos_t, up_t, dn_t = _rope_tables(seq)
    tiles_per_seq = seq // tm
    row = lambda i: (i, 0)
    tab = pl.BlockSpec((tm, HEAD_DIM), lambda i: (i % tiles_per_seq, 0))
    nkm = t // MOBA_BLOCK * sub
    outs = pl.pallas_call(
        _ple_qkv_kernel,
        grid=(t // tm,),
        in_specs=[
            pl.BlockSpec((tm, d), row), _layer_rows_spec(p_all, layer, tm),
            _const_spec((1, d)), _const_spec((d, d)), _const_spec((pd, d)),
            _const_spec((1, d)), _const_spec((d, d)), _const_spec((d, d)),
            _const_spec((1, d)), _const_spec((d, d)),
            tab, tab, tab,
        ],
        out_specs=[
            pl.BlockSpec((tm, d), row), pl.BlockSpec((tm, d), row),
            pl.BlockSpec((tm, d), row),
            pl.BlockSpec((tm // MOBA_BLOCK, d, MOBA_BLOCK), lambda i: (i, 0, 0)),
            pl.BlockSpec((tm // MOBA_BLOCK * sub, d), row),
        ],
        out_shape=[
            jax.ShapeDtypeStruct((t, d), f32), jax.ShapeDtypeStruct((t, d), f32),
            jax.ShapeDtypeStruct((t, d), bf16),
            jax.ShapeDtypeStruct((t // MOBA_BLOCK, d, MOBA_BLOCK), bf16),
            jax.ShapeDtypeStruct((nkm, d), f32),
        ],
        compiler_params=_params("parallel"),
        name="ple_qkv",
    )(h, p_all, ple_norm.reshape(1, d), ple_gate.astype(bf16), ple_proj.astype(bf16),
      kv_norm.reshape(1, d), w_k.astype(bf16), w_v.astype(bf16),
      attn_norm.reshape(1, d), w_q.astype(bf16), cos_t, up_t, dn_t)
    h3, q, k, v, km = outs
    return h3, q, k, v, km.reshape(t // MOBA_BLOCK, sub, d)[:, 0, :]


def _attn_kernel(q_ref, k_ref, vt_ref, km_ref, o_ref, qa_ref, sa_ref, sb_ref, pa_ref, pb_ref, acc_ref, *, hg):
    j = pl.program_id(2)
    nb = km_ref.shape[0]
    bs = q_ref.shape[0]
    hd = HEAD_DIM
    qscale = (hd ** -0.5) * LOG2E
    blk = lax.broadcasted_iota(jnp.int32, (nb, bs), 0).astype(f32)
    jf = j.astype(f32)
    krow = lax.broadcasted_iota(jnp.int32, (bs, bs), 0)
    qcol = lax.broadcasted_iota(jnp.int32, (bs, bs), 1)
    start_j = pl.multiple_of(j * bs, bs)

    ones_rows = jnp.ones((SUM_ROWS, bs), bf16)

    def weighted_values(n, h, pb):
        return _dot(jnp.concatenate([vt_ref[n, h * hd:(h + 1) * hd, :], ones_rows], axis=0), pb)

    heads = [slice(h * hd, (h + 1) * hd) for h in range(hg)]
    qfts = [q_ref[:, cs].T for cs in heads]
    qbts = [(qft * qscale).astype(bf16) for qft in qfts]
    diag = [_dot(k_ref[pl.ds(start_j, bs), cs], qbt) for cs, qbt in zip(heads, qbts)]
    gates = [_dot_split(km_ref[:, cs], qft) for cs, qft in zip(heads, qfts)]
    cands = [jnp.where(blk < jf, gate, NEG_INF) for gate in gates]
    sels = [blk < 0.0] * hg
    for _ in range(min(MOBA_TOPK, nb)):
        for h in range(hg):
            mx = jnp.max(cands[h], axis=0, keepdims=True)
            pick = jnp.min(jnp.where(cands[h] == mx, blk, float(nb)), axis=0, keepdims=True)
            hit = blk == pick
            sels[h] = sels[h] | hit
            cands[h] = jnp.where(hit, REMOVED, cands[h])
    init = []
    for h in range(hg):
        bias = jnp.where(sels[h] & (blk < jf), 0.0, NEG_INF)
        if nb < hd:
            bias = jnp.concatenate([bias, jnp.zeros((hd - nb, bs), f32)], axis=0)
        qa_ref[h] = jnp.concatenate([qbts[h], bias.astype(bf16)], axis=0)
        s = jnp.where(krow <= qcol, diag[h], NEG_INF)
        m0 = jnp.max(s, axis=0, keepdims=True)
        acc_ref[h] = weighted_values(j, h, jnp.exp2((s - m0).astype(bf16)))
        init += [m0, jnp.ones_like(m0)]

    lane = lax.broadcasted_iota(jnp.int32, (bs, hd), 1)

    def block_scores(n, h):
        start = pl.multiple_of(n * bs, bs)
        onehot = (lane == n).astype(bf16)
        ka = jnp.concatenate([k_ref[pl.ds(start, bs), h * hd:(h + 1) * hd], onehot], axis=1)
        return _dot(ka, qa_ref[h])

    def stage(n, s_in, s_out, p_prev, p_out, stats):
        nn = jnp.minimum(n + 1, nb - 1)
        prev = jnp.clip(n - 1, 0, nb - 1)
        new = []
        for h in range(hg):
            m, alpha_p = stats[2 * h:2 * h + 2]
            acc_ref[h] = alpha_p * acc_ref[h] + weighted_values(prev, h, p_prev[h])
            s = s_in[h]
            m_new = jnp.maximum(m, jnp.max(s, axis=0, keepdims=True))
            p_out[h] = jnp.exp2((s - m_new).astype(bf16))
            s_out[h] = block_scores(nn, h)
            new += [m_new, jnp.exp2(m - m_new)]
        return new

    per_trip = ATTN_BLOCKS_PER_TRIP

    def body(i, stats):
        for u in range(0, per_trip, 2):
            stats = stage(per_trip * i + u, sa_ref, sb_ref, pb_ref, pa_ref, stats)
            stats = stage(per_trip * i + u + 1, sb_ref, sa_ref, pa_ref, pb_ref, stats)
        return tuple(stats)

    for h in range(hg):
        sa_ref[h] = block_scores(0, h)
        pb_ref[h] = jnp.zeros((bs, bs), bf16)
    trips = (j + per_trip - 1) // per_trip
    res = lax.fori_loop(0, trips, body, tuple(init))
    last = jnp.clip(per_trip * trips - 1, 0, nb - 1)
    tails = [weighted_values(last, h, pb_ref[h]) for h in range(hg)]
    for h in range(hg):
        acc = res[2 * h + 1] * acc_ref[h] + tails[h]
        o_ref[:, h * hd:(h + 1) * hd] = (acc[:hd] * (1.0 / acc[hd:hd + 1])).T.astype(o_ref.dtype)


def _attention(q, k, vt, km, batch, seq):
    t, d = q.shape
    nh = d // HEAD_DIM
    nb = seq // MOBA_BLOCK
    hg = ATTN_HEADS_PER_STEP
    assert nb <= HEAD_DIM and nh % hg == 0
    w = hg * HEAD_DIM
    return pl.pallas_call(
        functools.partial(_attn_kernel, hg=hg),
        grid=(batch, nh // hg, nb),
        in_specs=[
            pl.BlockSpec((MOBA_BLOCK, w), lambda b, g, j: (b * nb + j, g)),
            pl.BlockSpec((seq, w), lambda b, g, j: (b, g), pipeline_mode=pl.Buffered(1)),
            pl.BlockSpec((nb, w, MOBA_BLOCK), lambda b, g, j: (b, g, 0), pipeline_mode=pl.Buffered(1)),
            pl.BlockSpec((nb, w), lambda b, g, j: (b, g)),
        ],
        out_specs=pl.BlockSpec((MOBA_BLOCK, w), lambda b, g, j: (b * nb + j, g)),
        out_shape=jax.ShapeDtypeStruct((t, d), bf16),
        scratch_shapes=[
            pltpu.VMEM((hg, 2 * HEAD_DIM, MOBA_BLOCK), bf16),
            pltpu.VMEM((hg, MOBA_BLOCK, MOBA_BLOCK), f32), pltpu.VMEM((hg, MOBA_BLOCK, MOBA_BLOCK), f32),
            pltpu.VMEM((hg, MOBA_BLOCK, MOBA_BLOCK), bf16), pltpu.VMEM((hg, MOBA_BLOCK, MOBA_BLOCK), bf16),
            pltpu.VMEM((hg, HEAD_DIM + SUM_ROWS, MOBA_BLOCK), f32),
        ],
        compiler_params=_params("parallel", "parallel", "arbitrary"),
        name="attn",
    )(q, k, vt, km)


def _oproj_router_kernel(h_ref, a_ref, wo_ref, g_ref, r_ref, h_out, hn_out, route_out, route_t_out, *, n_exp):
    h4 = h_ref[...] + _dot(a_ref[...], wo_ref[...])
    h_out[...] = h4
    hn = _rms(h4, g_ref[...])
    hn_out[...] = _pack_pairs(hn)
    logits = _dot_split(hn, r_ref[...])
    lane = lax.broadcasted_iota(jnp.int32, logits.shape, 1).astype(f32)
    cand = jnp.where(lane < n_exp, logits, NEG_INF)
    m1 = jnp.max(cand, axis=1, keepdims=True)
    i1 = jnp.min(jnp.where(cand == m1, lane, float(LANES)), axis=1, keepdims=True)
    cand = jnp.where(lane == i1, REMOVED, cand)
    m2 = jnp.max(cand, axis=1, keepdims=True)
    i2 = jnp.min(jnp.where(cand == m2, lane, float(LANES)), axis=1, keepdims=True)
    e2 = jnp.exp(m2 - m1)
    den = 1.0 + e2
    route = jnp.where(lane == 0, i1, jnp.where(lane == 1, i2, jnp.where(
        lane == 2, 1.0 / den, jnp.where(lane == 3, e2 / den, 0.0))))
    route_out[...] = route
    route_t_out[...] = route.T[:SUBLANES]


def _oproj_router_layer(h, attn, w_o, norm, router):
    t, d = h.shape
    tm = ROW_TILE
    n_exp = router.shape[1]
    r_pad = jnp.zeros((d, LANES), f32).at[:, :n_exp].set(router)
    row = lambda i: (i, 0)
    return pl.pallas_call(
        functools.partial(_oproj_router_kernel, n_exp=n_exp),
        grid=(t // tm,),
        in_specs=[pl.BlockSpec((tm, d), row), pl.BlockSpec((tm, d), row), _const_spec((d, d)),
                  _const_spec((1, d)), _const_spec((d, LANES))],
        out_specs=[pl.BlockSpec((tm, d), row), pl.BlockSpec((tm, d // 2), row), pl.BlockSpec((tm, LANES), row),
                   pl.BlockSpec((SUBLANES, tm), lambda i: (0, i))],
        out_shape=[jax.ShapeDtypeStruct((t, d), f32), jax.ShapeDtypeStruct((t, d // 2), jnp.uint32),
                   jax.ShapeDtypeStruct((t, LANES), f32), jax.ShapeDtypeStruct((SUBLANES, t), f32)],
        compiler_params=_params("parallel"),
        name="oproj_router",
    )(h, attn, w_o.astype(bf16), norm.reshape(1, d), r_pad)


def _sc_gather(x, idx):
    n = idx.shape[0]
    assert x.shape[1] == SC_ROW and n % SC_WINDOW == 0
    mesh = plsc.VectorSubcoreMesh(core_axis_name="core", subcore_axis_name="subcore")

    @pl.kernel(out_type=jax.ShapeDtypeStruct((n, SC_ROW), x.dtype), mesh=mesh, scratch_types=[])
    def gather(x_hbm, i_hbm, o_hbm):
        def body(i_vmem, o_vmem):
            pltpu.sync_copy(x_hbm.at[i_vmem.at[0]], o_vmem)

        pltpu.emit_pipeline(
            body,
            grid=(n // SC_WINDOW,),
            in_specs=[pl.BlockSpec((1, SC_WINDOW), index_map=lambda i: (0, i))],
            out_specs=[pl.BlockSpec((SC_WINDOW, SC_ROW), index_map=lambda i: (i, 0))],
            core_axis_name=("core", "subcore"),
            dimension_semantics=(pltpu.PARALLEL,),
        )(i_hbm, o_hbm)

    return gather(x, idx.reshape(n // LANES, LANES).reshape(1, n))


def _pieces(a):
    n, d = a.shape
    return a.reshape(n // SUBLANES, SUBLANES, d // SC_ROW, SC_ROW).transpose(0, 2, 1, 3).reshape(-1, SC_ROW)


def _unpieces(p, d):
    per = d // SC_ROW
    n = p.shape[0] // per
    return p.reshape(n // SUBLANES, per, SUBLANES, SC_ROW).transpose(0, 2, 1, 3).reshape(n, d)


def _gather_tokens(x, rows):
    n, d = x.shape
    m = rows[0].shape[0]
    per = d // SC_ROW
    assert n % SUBLANES == 0 and m % SUBLANES == 0
    tiled = []
    for r in rows:
        first = (r // SUBLANES * (SUBLANES * per) + r % SUBLANES).reshape(m // SUBLANES, SUBLANES)
        tiled.append(jnp.tile(first, (1, per)))
    chunk = jnp.arange(per * SUBLANES, dtype=jnp.int32) // SUBLANES * SUBLANES
    idx = jnp.concatenate(tiled, axis=1) + jnp.tile(chunk, len(rows))[None, :]
    return _unpieces(_sc_gather(_pieces(x), idx.reshape(-1)), len(rows) * d)


def _expert_kernel(te_ref, tv_ref, x_ref, w1_ref, w3_ref, w2_ref, *rest, nf, first_tile):
    o_ref, xb_ref, acc_ref = rest[-3:]
    i = pl.program_id(0) + first_tile
    f = pl.program_id(1)
    valid = tv_ref[i] > 0

    @pl.when(f == 0)
    def _():
        xb_ref[...] = _unpack_pairs(x_ref[...]).astype(bf16)
        acc_ref[...] = jnp.zeros_like(acc_ref)

    @pl.when(valid)
    def _():
        x = xb_ref[...]
        a = _dot(x, w1_ref[...])
        b = _dot(x, w3_ref[...])
        acc_ref[...] += _dot((jax.nn.silu(a) * b).astype(bf16), w2_ref[...])

    @pl.when(f == nf - 1)
    def _():
        o_ref[...] = _pack_pairs(acc_ref[...])


def _route_tables(choices, n_exp, tm):
    t = choices[0].shape[0]
    nslots = len(choices) * t
    experts = jnp.arange(n_exp, dtype=jnp.int32)[:, None]
    onehots = [(c[None, :] == experts).astype(jnp.int32) for c in choices]
    cums = [jnp.cumsum(oh, axis=1) for oh in onehots]
    totals = [cu[:, -1] for cu in cums]
    counts = sum(totals)
    tiles_e = (counts + tm - 1) // tm
    tile_end = jnp.cumsum(tiles_e)
    tile_start = tile_end - tiles_e
    group_start = jnp.cumsum(counts) - counts
    nt = nslots // tm + n_exp
    total = tile_end[-1]
    ti = jnp.arange(nt, dtype=jnp.int32)
    tv = (ti < total).astype(jnp.int32)
    tc = jnp.minimum(ti, total - 1)
    te = jnp.minimum(jnp.sum((tc[:, None] >= tile_end[None, :]).astype(jnp.int32), axis=1), n_exp - 1)
    order = jnp.argsort(jnp.concatenate(choices), stable=True).astype(jnp.int32)
    lane = jnp.arange(tm, dtype=jnp.int32)[None, :]
    rank = ((tc - tile_start[te]) * tm)[:, None] + lane
    real = (rank < counts[te][:, None]) & (tv[:, None] > 0)
    sidx = jnp.clip(group_start[te][:, None] + rank, 0, nslots - 1)
    src = jnp.where(real, order[sidx] % t, (ti[:, None] * tm + lane) % t).reshape(-1)
    pos = []
    earlier = jnp.zeros((n_exp,), jnp.int32)
    for oh, cu, tot in zip(onehots, cums, totals):
        pos.append(jnp.sum(oh * ((tile_start * tm + earlier)[:, None] + cu - oh), axis=0))
        earlier = earlier + tot
    return te, tv, src, pos, nt


def _experts_layer(hn, choices, w1, w3, w2):
    n_exp, d, ff = w1.shape
    tm = ROW_TILE
    fc = _ff_chunk(ff, EXPERT_FF_CHUNK)
    nf = ff // fc
    te, tv, src, pos, nt = _route_tables(choices, n_exp, tm)
    assert nt % SC_OVERLAP_PARTS == 0
    ntp = nt // SC_OVERLAP_PARTS

    ys = None
    for part in range(SC_OVERLAP_PARTS):
        t0 = part * ntp
        xs = _gather_tokens(hn, [src[t0 * tm:(t0 + ntp) * tm]])

        def fsel(i, f, tv_r, t0=t0):
            return jnp.where(tv_r[i + t0] > 0, f, nf - 1)

        def wmap_in(i, f, te_r, tv_r, t0=t0, fsel=fsel):
            return (te_r[i + t0], 0, fsel(i, f, tv_r))

        def wmap_out(i, f, te_r, tv_r, t0=t0, fsel=fsel):
            return (te_r[i + t0], fsel(i, f, tv_r), 0)

        in_specs = [
            pl.BlockSpec((tm, d // 2), lambda i, f, *_: (i, 0)),
            pl.BlockSpec((None, d, fc), wmap_in),
            pl.BlockSpec((None, d, fc), wmap_in),
            pl.BlockSpec((None, fc, d), wmap_out),
        ]
        args = [te, tv, xs, w1, w3, w2]
        aliases = {}
        if ys is not None:
            in_specs.append(pl.BlockSpec(memory_space=pl.ANY))
            args.append(ys)
            aliases = {len(args) - 1: 0}
        ys = pl.pallas_call(
            functools.partial(_expert_kernel, nf=nf, first_tile=t0),
            grid_spec=pltpu.PrefetchScalarGridSpec(
                num_scalar_prefetch=2,
                grid=(ntp, nf),
                in_specs=in_specs,
                out_specs=pl.BlockSpec((tm, d // 2), lambda i, f, *_, t0=t0: (i + t0, 0)),
                scratch_shapes=[pltpu.VMEM((tm, d), bf16), pltpu.VMEM((tm, d), f32)],
            ),
            out_shape=jax.ShapeDtypeStruct((nt * tm, d // 2), jnp.uint32),
            input_output_aliases=aliases,
            compiler_params=_params("parallel", "arbitrary"),
            name="experts",
        )(*args)
    return ys, pos


def _final_kernel(h_ref, y_ref, route_ref, p_ref, pn_ref, wg_ref, wp_ref, fn_ref, *rest):
    o_ref = rest[-1]
    d = h_ref.shape[1]
    route = route_ref[...]
    y = y_ref[...]
    h5 = h_ref[...] + route[:, 2:3] * _unpack_pairs(y[:, :d // 2]) + route[:, 3:4] * _unpack_pairs(y[:, d // 2:])
    hn = _rms(h5, pn_ref[...]).astype(bf16)
    h6 = h5 + jax.nn.sigmoid(_dot(hn, wg_ref[...])) * _dot(p_ref[...].astype(bf16), wp_ref[...])
    o_ref[...] = _rms(h6, fn_ref[...])


def _final_layer(h, ys, pos, route, p_all, layer, ple_norm, ple_gate, ple_proj, final_norm):
    t, d = h.shape
    pd = p_all.shape[2]
    tm = ROW_TILE
    assert t % (SC_OVERLAP_PARTS * tm) == 0
    tp = t // SC_OVERLAP_PARTS
    out = None
    for part in range(SC_OVERLAP_PARTS):
        r0 = part * tp // tm
        y2 = _gather_tokens(ys, [p[part * tp:(part + 1) * tp] for p in pos])
        row = lambda i, r0=r0: (i + r0, 0)
        in_specs = [pl.BlockSpec((tm, d), row), pl.BlockSpec((tm, y2.shape[1]), lambda i: (i, 0)),
                    pl.BlockSpec((tm, LANES), row),
                    pl.BlockSpec((None, tm, pd), lambda i, r0=r0: (layer, i + r0, 0)),
                    _const_spec((1, d)), _const_spec((d, d)), _const_spec((pd, d)), _const_spec((1, d))]
        args = [h, y2, route, p_all, ple_norm.reshape(1, d), ple_gate.astype(bf16), ple_proj.astype(bf16),
                final_norm.reshape(1, d)]
        aliases = {}
        if out is not None:
            in_specs.append(pl.BlockSpec(memory_space=pl.ANY))
            args.append(out)
            aliases = {len(args) - 1: 0}
        out = pl.pallas_call(
            _final_kernel,
            grid=(tp // tm,),
            in_specs=in_specs,
            out_specs=pl.BlockSpec((tm, d), row),
            out_shape=jax.ShapeDtypeStruct((t, d), f32),
            input_output_aliases=aliases,
            compiler_params=_params("parallel"),
            name="final",
        )(*args)
    return out


def kernel(x, p, pool_norm, pool_w, pool_scale, kv_norm, w_k, w_v, attn_norm, w_q, w_o, ffn_norm, ffn_w1, ffn_w3, ffn_w2, router, exp_w1, exp_w3, exp_w2, ple_norm, ple_gate, ple_proj, final_norm):
    batch, seq, d = x.shape
    t = batch * seq
    assert seq % ROW_TILE == 0 and ROW_TILE % MOBA_BLOCK == 0 and d % HEAD_DIM == 0
    assert p.shape[0] == 2 and router.shape[2] <= LANES
    h = x.reshape(t, d)
    pf = p.reshape(p.shape[0], t, p.shape[-1])

    ew1, ew3, ew2 = _cast_bf16([exp_w1[0], exp_w3[0], exp_w2[0]], EXPERT_CAST_STEPS)
    h = _pool_layer(h, pool_norm[0], pool_w[0], pool_scale[0], seq, ew2[0, :2 * SUBLANES, :LANES])
    h = _swiglu_layer(h, ffn_norm[0], ffn_w1[0], ffn_w3[0], ffn_w2[0])
    h, q, k, v, km = _ple_qkv_layer(h, pf, 0, ple_norm[0], ple_gate[0], ple_proj[0],
                                    kv_norm, w_k, w_v, attn_norm[0], w_q[0], seq)
    attn = _attention(q, k, v, km, batch, seq)
    h, hn, route, route_t = _oproj_router_layer(h, attn, w_o[0], ffn_norm[1], router[0])
    choices = [route_t[k].astype(jnp.int32) for k in range(TOP_K_EXPERTS)]
    ys, pos = _experts_layer(hn, choices, ew1, ew3, ew2)
    out = _final_layer(h, ys, pos, route, pf, 1, ple_norm[1], ple_gate[1],
                       ple_proj[1], final_norm)
    return out.reshape(batch, seq, d)
```

```python
import functools

import jax
import jax.numpy as jnp
from jax import lax
from jax.experimental import pallas as pl
from jax.experimental.pallas import tpu as pltpu
from jax.experimental.pallas import tpu_sc as plsc

POOL_WINDOWS = (2, 4, 8, 16)
HEAD_DIM = 128
MOBA_BLOCK = 256
MOBA_TOPK = 3
ROPE_THETA = 500000.0
ROPE_DIM = HEAD_DIM // 4
TOP_K_EXPERTS = 2
RMS_EPS = 1e-6
NEG_INF = -1e30
REMOVED = -3e38

LANES = 128
SUBLANES = 8
ROW_TILE = 512
POOL_SUB = 128
ATTN_HEADS_PER_STEP = 4
ATTN_BLOCKS_PER_TRIP = 4
SUM_ROWS = 16
LOG2E = 1.4426950408889634
EXPERT_FF_CHUNK = 1792
EXPERT_CAST_STEPS = 32
SC_OVERLAP_PARTS = 4
SC_ROW = 128
SC_WINDOW = 128
VMEM_LIMIT = 56 * 1024 * 1024

bf16 = jnp.bfloat16
f32 = jnp.float32


def _dot(a, b):
    return jnp.dot(a, b, preferred_element_type=f32)


def _dot_split(a, b):
    a_hi = a.astype(bf16)
    a_lo = (a - a_hi.astype(f32)).astype(bf16)
    b_hi = b.astype(bf16)
    b_lo = (b - b_hi.astype(f32)).astype(bf16)
    return _dot(a_hi, b_hi) + (_dot(a_lo, b_hi) + _dot(a_hi, b_lo))


def _pack_pairs(x):
    half = x.shape[1] // 2
    lo = lax.bitcast_convert_type(x[:, :half].astype(bf16).astype(f32), jnp.uint32)
    hi = lax.bitcast_convert_type(x[:, half:].astype(bf16).astype(f32), jnp.uint32)
    return (hi & jnp.uint32(0xFFFF0000)) | (lo >> 16)


def _unpack_pairs(w):
    lo = lax.bitcast_convert_type(w << 16, f32)
    hi = lax.bitcast_convert_type(w & jnp.uint32(0xFFFF0000), f32)
    return jnp.concatenate([lo, hi], axis=1)


def _cast_kernel(*refs):
    n = len(refs) // 2
    for src, dst in zip(refs[:n], refs[n:]):
        dst[...] = src[...].astype(bf16)


def _cast_bf16(ws, steps):
    flat = [w.reshape(-1, w.shape[-1]) for w in ws]
    for a in flat:
        assert a.shape[0] % steps == 0 and (a.shape[0] // steps) % (2 * SUBLANES) == 0
    specs = [pl.BlockSpec((a.shape[0] // steps, a.shape[1]), lambda s: (s, 0)) for a in flat]
    outs = pl.pallas_call(
        _cast_kernel,
        grid=(steps,),
        in_specs=specs,
        out_specs=specs,
        out_shape=[jax.ShapeDtypeStruct(a.shape, bf16) for a in flat],
        compiler_params=_params("parallel"),
        name="cast_bf16",
    )(*flat)
    return [o.reshape(w.shape) for o, w in zip(outs, ws)]


def _rms(x, g):
    var = jnp.mean(x * x, axis=-1, keepdims=True)
    return x * lax.rsqrt(var + RMS_EPS) * g


def _params(*sem):
    return pltpu.CompilerParams(dimension_semantics=sem, vmem_limit_bytes=VMEM_LIMIT)


def _const_spec(shape):
    nd = len(shape)
    return pl.BlockSpec(shape, lambda *_: (0,) * nd)


def _pool_kernel(x_ref, halo_ref, g_ref, pw_ref, ps_ref, after_ref, o_ref, pooled_ref, *, ts, seq):
    del after_ref
    i = pl.program_id(0)
    g = g_ref[...]
    x = x_ref[...]
    xn = _rms(x, g)
    keep = jnp.where((i * ts) % seq == 0, 0.0, 1.0)
    hnb = (_rms(halo_ref[...], g) * keep).astype(bf16)
    xnb = xn.astype(bf16)
    gd = x.shape[1] // len(POOL_WINDOWS)
    r = lax.broadcasted_iota(jnp.int32, (POOL_SUB, 2 * POOL_SUB), 0)
    c = lax.broadcasted_iota(jnp.int32, (POOL_SUB, 2 * POOL_SUB), 1)
    dist = r + POOL_SUB - c
    bands = [((dist >= 0) & (dist < w)).astype(bf16) for w in POOL_WINDOWS]
    rows = lax.broadcasted_iota(jnp.int32, (POOL_SUB, 1), 0)
    for sb in range(ts // POOL_SUB):
        lo, hi = sb * POOL_SUB, (sb + 1) * POOL_SUB
        prev = hnb if sb == 0 else xnb[lo - POOL_SUB:lo]
        ext = jnp.concatenate([prev, xnb[lo:hi]], axis=0)
        tpos = (i * ts + lo) % seq + rows
        for gi, w in enumerate(POOL_WINDOWS):
            cs = slice(gi * gd, (gi + 1) * gd)
            wsum = _dot(bands[gi], ext[:, cs])
            inv_cnt = 1.0 / jnp.minimum(tpos + 1, w).astype(f32)
            pooled_ref[lo:hi, cs] = (wsum * inv_cnt - xn[lo:hi, cs]).astype(bf16)
    for gi in range(len(POOL_WINDOWS)):
        cs = slice(gi * gd, (gi + 1) * gd)
        mixed = _dot(pooled_ref[:, cs], pw_ref[gi])
        o_ref[:, cs] = x[:, cs] + mixed * ps_ref[:, cs]


def _pool_layer(h, norm, pool_w, pool_scale, seq, after):
    t, d = h.shape
    ts = ROW_TILE
    ng = len(POOL_WINDOWS)
    gd = d // ng
    per = ts // POOL_SUB
    return pl.pallas_call(
        functools.partial(_pool_kernel, ts=ts, seq=seq),
        grid=(t // ts,),
        in_specs=[
            pl.BlockSpec((ts, d), lambda i: (i, 0)),
            pl.BlockSpec((POOL_SUB, d), lambda i: (jnp.maximum(i * per - 1, 0), 0)),
            _const_spec((1, d)),
            _const_spec((ng, gd, gd)),
            _const_spec((1, d)),
            _const_spec(after.shape),
        ],
        out_specs=pl.BlockSpec((ts, d), lambda i: (i, 0)),
        out_shape=jax.ShapeDtypeStruct((t, d), f32),
        scratch_shapes=[pltpu.VMEM((ts, d), bf16)],
        compiler_params=_params("parallel"),
        name="pool",
    )(h, h, norm.reshape(1, d), pool_w.astype(bf16), pool_scale.reshape(1, d), after)


def _swiglu_kernel(h_ref, g_ref, w1_ref, w3_ref, w2_ref, o_ref):
    x = h_ref[...]
    hn = _rms(x, g_ref[...]).astype(bf16)
    a = _dot(hn, w1_ref[...])
    b = _dot(hn, w3_ref[...])
    o_ref[...] = x + _dot((jax.nn.silu(a) * b).astype(bf16), w2_ref[...])


def _ff_chunk(ff, target):
    units = ff // LANES
    best = 1
    for k in range(1, units + 1):
        if units % k == 0 and k * LANES <= target:
            best = k
    return best * LANES


def _swiglu_layer(h, norm, w1, w3, w2):
    t, d = h.shape
    ff = w1.shape[1]
    tm = ROW_TILE
    w1r, w3r, w2r = w1.astype(bf16), w3.astype(bf16), w2.astype(bf16)
    resident = lambda shape: pl.BlockSpec(shape, lambda i: (0, 0), pipeline_mode=pl.Buffered(1))
    return pl.pallas_call(
        _swiglu_kernel,
        grid=(t // tm,),
        in_specs=[
            pl.BlockSpec((tm, d), lambda i: (i, 0)),
            _const_spec((1, d)),
            resident((d, ff)), resident((d, ff)), resident((ff, d)),
        ],
        out_specs=pl.BlockSpec((tm, d), lambda i: (i, 0)),
        out_shape=jax.ShapeDtypeStruct((t, d), f32),
        compiler_params=_params("parallel"),
        name="swiglu",
    )(h, norm.reshape(1, d), w1r, w3r, w2r)


def _pool_swiglu_kernel(x_ref, halo_ref, g_ref, pw_ref, ps_ref, after_ref, g2_ref, w1_ref, w3_ref, w2_ref,
                        o_ref, pooled_ref, h1_ref, *, ts, seq):
    _pool_kernel(x_ref, halo_ref, g_ref, pw_ref, ps_ref, after_ref, h1_ref, pooled_ref, ts=ts, seq=seq)
    _swiglu_kernel(h1_ref, g2_ref, w1_ref, w3_ref, w2_ref, o_ref)


def _pool_swiglu_layer(h, pool_norm, pool_w, pool_scale, seq, after, ffn_norm, w1, w3, w2):
    t, d = h.shape
    ff = w1.shape[1]
    ts = ROW_TILE
    ng = len(POOL_WINDOWS)
    gd = d // ng
    per = ts // POOL_SUB
    resident = lambda shape: pl.BlockSpec(shape, lambda i: (0,) * len(shape), pipeline_mode=pl.Buffered(1))
    return pl.pallas_call(
        functools.partial(_pool_swiglu_kernel, ts=ts, seq=seq),
        grid=(t // ts,),
        in_specs=[
            pl.BlockSpec((ts, d), lambda i: (i, 0)),
            pl.BlockSpec((POOL_SUB, d), lambda i: (jnp.maximum(i * per - 1, 0), 0)),
            _const_spec((1, d)),
            resident((ng, gd, gd)),
            _const_spec((1, d)),
            _const_spec(after.shape),
            _const_spec((1, d)),
            resident((d, ff)), resident((d, ff)), resident((ff, d)),
        ],
        out_specs=pl.BlockSpec((ts, d), lambda i: (i, 0)),
        out_shape=jax.ShapeDtypeStruct((t, d), f32),
        scratch_shapes=[pltpu.VMEM((ts, d), bf16), pltpu.VMEM((ts, d), f32)],
        compiler_params=_params("parallel"),
        name="pool_swiglu",
    )(h, h, pool_norm.reshape(1, d), pool_w.astype(bf16), pool_scale.reshape(1, d), after,
      ffn_norm.reshape(1, d), w1.astype(bf16), w3.astype(bf16), w2.astype(bf16))


def _rope_tables(seq):
    half = ROPE_DIM // 2
    inv_freq = jnp.float32(ROPE_THETA) ** (-(jnp.arange(0, ROPE_DIM, 2, dtype=f32) / ROPE_DIM))
    ang = jnp.arange(seq, dtype=f32)[:, None] * inv_freq[None, :]
    cos, sin = jnp.cos(ang), jnp.sin(ang)
    ones = jnp.ones((seq, HEAD_DIM - ROPE_DIM), f32)
    zeros = jnp.zeros((seq, HEAD_DIM - half), f32)
    cos_t = jnp.concatenate([cos, cos, ones], axis=1)
    up_t = jnp.concatenate([-sin, zeros], axis=1)
    dn_t = jnp.concatenate([jnp.zeros((seq, half), f32), sin, jnp.zeros((seq, HEAD_DIM - ROPE_DIM), f32)], axis=1)
    return cos_t, up_t, dn_t


def _rope(xh, cos_t, up_t, dn_t):
    half = ROPE_DIM // 2
    return (xh * cos_t + pltpu.roll(xh, HEAD_DIM - half, 1) * up_t + pltpu.roll(xh, half, 1) * dn_t)


def _ple_qkv_kernel(h_ref, p_ref, pn_ref, wg_ref, wp_ref, kvn_ref, wk_ref, wv_ref, an_ref, wq_ref,
                    cos_ref, up_ref, dn_ref, h_out, q_out, k_out, v_out, km_out):
    h = h_ref[...]
    hn = _rms(h, pn_ref[...]).astype(bf16)
    h3 = h + jax.nn.sigmoid(_dot(hn, wg_ref[...])) * _dot(p_ref[...].astype(bf16), wp_ref[...])
    h_out[...] = h3
    base = h3 * lax.rsqrt(jnp.mean(h3 * h3, axis=-1, keepdims=True) + RMS_EPS)
    kn = (base * kvn_ref[...]).astype(bf16)
    qn = (base * an_ref[...]).astype(bf16)
    v = _dot(kn, wv_ref[...])
    for bi in range(h.shape[0] // MOBA_BLOCK):
        v_out[bi] = v[bi * MOBA_BLOCK:(bi + 1) * MOBA_BLOCK].T.astype(bf16)
    k = _dot(kn, wk_ref[...])
    q = _dot(qn, wq_ref[...])
    cos_t, up_t, dn_t = cos_ref[...], up_ref[...], dn_ref[...]
    tm, d = h.shape
    sub = km_out.shape[0] // (tm // MOBA_BLOCK)
    for hh in range(d // HEAD_DIM):
        cs = slice(hh * HEAD_DIM, (hh + 1) * HEAD_DIM)
        q_out[:, cs] = _rope(q[:, cs], cos_t, up_t, dn_t)
        kr = _rope(k[:, cs], cos_t, up_t, dn_t)
        k_out[:, cs] = kr.astype(bf16)
        for bi in range(tm // MOBA_BLOCK):
            m = jnp.mean(kr[bi * MOBA_BLOCK:(bi + 1) * MOBA_BLOCK], axis=0, keepdims=True)
            km_out[bi * sub:(bi + 1) * sub, cs] = jnp.broadcast_to(m, (sub, HEAD_DIM))


def _layer_rows_spec(p_all, layer, tm):
    return pl.BlockSpec((None, tm, p_all.shape[2]), lambda i: (layer, i, 0))


def _ple_qkv_layer(h, p_all, layer, ple_norm, ple_gate, ple_proj, kv_norm, w_k, w_v, attn_norm, w_q, seq):
    t, d = h.shape
    pd = p_all.shape[2]
    tm = ROW_TILE
    sub = SUBLANES
    cos_t, up_t, dn_t = _rope_tables(seq)
    tiles_per_seq = seq // tm
    row = lambda i: (i, 0)
    tab = pl.BlockSpec((tm, HEAD_DIM), lambda i: (i % tiles_per_seq, 0))
    nkm = t // MOBA_BLOCK * sub
    outs = pl.pallas_call(
        _ple_qkv_kernel,
        grid=(t // tm,),
        in_specs=[
            pl.BlockSpec((tm, d), row), _layer_rows_spec(p_all, layer, tm),
            _const_spec((1, d)), _const_spec((d, d)), _const_spec((pd, d)),
            _const_spec((1, d)), _const_spec((d, d)), _const_spec((d, d)),
            _const_spec((1, d)), _const_spec((d, d)),
            tab, tab, tab,
        ],
        out_specs=[
            pl.BlockSpec((tm, d), row), pl.BlockSpec((tm, d), row),
            pl.BlockSpec((tm, d), row),
            pl.BlockSpec((tm // MOBA_BLOCK, d, MOBA_BLOCK), lambda i: (i, 0, 0)),
            pl.BlockSpec((tm // MOBA_BLOCK * sub, d), row),
        ],
        out_shape=[
            jax.ShapeDtypeStruct((t, d), f32), jax.ShapeDtypeStruct((t, d), f32),
            jax.ShapeDtypeStruct((t, d), bf16),
            jax.ShapeDtypeStruct((t // MOBA_BLOCK, d, MOBA_BLOCK), bf16),
            jax.ShapeDtypeStruct((nkm, d), f32),
        ],
        compiler_params=_params("parallel"),
        name="ple_qkv",
    )(h, p_all, ple_norm.reshape(1, d), ple_gate.astype(bf16), ple_proj.astype(bf16),
      kv_norm.reshape(1, d), w_k.astype(bf16), w_v.astype(bf16),
      attn_norm.reshape(1, d), w_q.astype(bf16), cos_t, up_t, dn_t)
    h3, q, k, v, km = outs
    return h3, q, k, v, km.reshape(t // MOBA_BLOCK, sub, d)[:, 0, :]


def _attn_kernel(q_ref, k_ref, vt_ref, km_ref, o_ref, qa_ref, sa_ref, sb_ref, pa_ref, pb_ref, acc_ref, *, hg):
    j = pl.program_id(2)
    nb = km_ref.shape[0]
    bs = q_ref.shape[0]
    hd = HEAD_DIM
    qscale = (hd ** -0.5) * LOG2E
    blk = lax.broadcasted_iota(jnp.int32, (nb, bs), 0).astype(f32)
    jf = j.astype(f32)
    krow = lax.broadcasted_iota(jnp.int32, (bs, bs), 0)
    qcol = lax.broadcasted_iota(jnp.int32, (bs, bs), 1)
    start_j = pl.multiple_of(j * bs, bs)

    ones_rows = jnp.ones((SUM_ROWS, bs), bf16)

    def weighted_values(n, h, pb):
        return _dot(jnp.concatenate([vt_ref[n, h * hd:(h + 1) * hd, :], ones_rows], axis=0), pb)

    heads = [slice(h * hd, (h + 1) * hd) for h in range(hg)]
    qfts = [q_ref[:, cs].T for cs in heads]
    qbts = [(qft * qscale).astype(bf16) for qft in qfts]
    diag = [_dot(k_ref[pl.ds(start_j, bs), cs], qbt) for cs, qbt in zip(heads, qbts)]
    gates = [_dot_split(km_ref[:, cs], qft) for cs, qft in zip(heads, qfts)]
    cands = [jnp.where(blk < jf, gate, NEG_INF) for gate in gates]
    sels = [blk < 0.0] * hg
    for _ in range(min(MOBA_TOPK, nb)):
        for h in range(hg):
            mx = jnp.max(cands[h], axis=0, keepdims=True)
            pick = jnp.min(jnp.where(cands[h] == mx, blk, float(nb)), axis=0, keepdims=True)
            hit = blk == pick
            sels[h] = sels[h] | hit
            cands[h] = jnp.where(hit, REMOVED, cands[h])
    init = []
    for h in range(hg):
        bias = jnp.where(sels[h] & (blk < jf), 0.0, NEG_INF)
        if nb < hd:
            bias = jnp.concatenate([bias, jnp.zeros((hd - nb, bs), f32)], axis=0)
        qa_ref[h] = jnp.concatenate([qbts[h], bias.astype(bf16)], axis=0)
        s = jnp.where(krow <= qcol, diag[h], NEG_INF)
        m0 = jnp.max(s, axis=0, keepdims=True)
        acc_ref[h] = weighted_values(j, h, jnp.exp2((s - m0).astype(bf16)))
        init += [m0, jnp.ones_like(m0)]

    lane = lax.broadcasted_iota(jnp.int32, (bs, hd), 1)

    def block_scores(n, h):
        start = pl.multiple_of(n * bs, bs)
        onehot = (lane == n).astype(bf16)
        ka = jnp.concatenate([k_ref[pl.ds(start, bs), h * hd:(h + 1) * hd], onehot], axis=1)
        return _dot(ka, qa_ref[h])

    def stage(n, s_in, s_out, p_prev, p_out, stats):
        nn = jnp.minimum(n + 1, nb - 1)
        prev = jnp.clip(n - 1, 0, nb - 1)
        new = []
        for h in range(hg):
            m, alpha_p = stats[2 * h:2 * h + 2]
            acc_ref[h] = alpha_p * acc_ref[h] + weighted_values(prev, h, p_prev[h])
            s = s_in[h]
            m_new = jnp.maximum(m, jnp.max(s, axis=0, keepdims=True))
            p_out[h] = jnp.exp2((s - m_new).astype(bf16))
            s_out[h] = block_scores(nn, h)
            new += [m_new, jnp.exp2(m - m_new)]
        return new

    per_trip = ATTN_BLOCKS_PER_TRIP

    def body(i, stats):
        for u in range(0, per_trip, 2):
            stats = stage(per_trip * i + u, sa_ref, sb_ref, pb_ref, pa_ref, stats)
            stats = stage(per_trip * i + u + 1, sb_ref, sa_ref, pa_ref, pb_ref, stats)
        return tuple(stats)

    for h in range(hg):
        sa_ref[h] = block_scores(0, h)
        pb_ref[h] = jnp.zeros((bs, bs), bf16)
    trips = (j + per_trip - 1) // per_trip
    res = lax.fori_loop(0, trips, body, tuple(init))
    last = jnp.clip(per_trip * trips - 1, 0, nb - 1)
    tails = [weighted_values(last, h, pb_ref[h]) for h in range(hg)]
    for h in range(hg):
        acc = res[2 * h + 1] * acc_ref[h] + tails[h]
        o_ref[:, h * hd:(h + 1) * hd] = (acc[:hd] * (1.0 / acc[hd:hd + 1])).T.astype(o_ref.dtype)


def _attention(q, k, vt, km, batch, seq):
    t, d = q.shape
    nh = d // HEAD_DIM
    nb = seq // MOBA_BLOCK
    hg = ATTN_HEADS_PER_STEP
    assert nb <= HEAD_DIM and nh % hg == 0
    w = hg * HEAD_DIM
    return pl.pallas_call(
        functools.partial(_attn_kernel, hg=hg),
        grid=(batch, nh // hg, nb),
        in_specs=[
            pl.BlockSpec((MOBA_BLOCK, w), lambda b, g, j: (b * nb + j, g)),
            pl.BlockSpec((seq, w), lambda b, g, j: (b, g), pipeline_mode=pl.Buffered(1)),
            pl.BlockSpec((nb, w, MOBA_BLOCK), lambda b, g, j: (b, g, 0), pipeline_mode=pl.Buffered(1)),
            pl.BlockSpec((nb, w), lambda b, g, j: (b, g)),
        ],
        out_specs=pl.BlockSpec((MOBA_BLOCK, w), lambda b, g, j: (b * nb + j, g)),
        out_shape=jax.ShapeDtypeStruct((t, d), bf16),
        scratch_shapes=[
            pltpu.VMEM((hg, 2 * HEAD_DIM, MOBA_BLOCK), bf16),
            pltpu.VMEM((hg, MOBA_BLOCK, MOBA_BLOCK), f32), pltpu.VMEM((hg, MOBA_BLOCK, MOBA_BLOCK), f32),
            pltpu.VMEM((hg, MOBA_BLOCK, MOBA_BLOCK), bf16), pltpu.VMEM((hg, MOBA_BLOCK, MOBA_BLOCK), bf16),
            pltpu.VMEM((hg, HEAD_DIM + SUM_ROWS, MOBA_BLOCK), f32),
        ],
        compiler_params=_params("parallel", "parallel", "arbitrary"),
        name="attn",
    )(q, k, vt, km)


def _oproj_router_kernel(h_ref, a_ref, wo_ref, g_ref, r_ref, h_out, hn_out, route_out, route_t_out, *, n_exp):
    h4 = h_ref[...] + _dot(a_ref[...], wo_ref[...])
    h_out[...] = h4
    hn = _rms(h4, g_ref[...])
    hn_out[...] = _pack_pairs(hn)
    logits = _dot_split(hn, r_ref[...])
    lane = lax.broadcasted_iota(jnp.int32, logits.shape, 1).astype(f32)
    cand = jnp.where(lane < n_exp, logits, NEG_INF)
    m1 = jnp.max(cand, axis=1, keepdims=True)
    i1 = jnp.min(jnp.where(cand == m1, lane, float(LANES)), axis=1, keepdims=True)
    cand = jnp.where(lane == i1, REMOVED, cand)
    m2 = jnp.max(cand, axis=1, keepdims=True)
    i2 = jnp.min(jnp.where(cand == m2, lane, float(LANES)), axis=1, keepdims=True)
    e2 = jnp.exp(m2 - m1)
    den = 1.0 + e2
    route = jnp.where(lane == 0, i1, jnp.where(lane == 1, i2, jnp.where(
        lane == 2, 1.0 / den, jnp.where(lane == 3, e2 / den, 0.0))))
    route_out[...] = route
    route_t_out[...] = route.T[:SUBLANES]


def _oproj_router_layer(h, attn, w_o, norm, router):
    t, d = h.shape
    tm = ROW_TILE
    n_exp = router.shape[1]
    r_pad = jnp.zeros((d, LANES), f32).at[:, :n_exp].set(router)
    row = lambda i: (i, 0)
    return pl.pallas_call(
        functools.partial(_oproj_router_kernel, n_exp=n_exp),
        grid=(t // tm,),
        in_specs=[pl.BlockSpec((tm, d), row), pl.BlockSpec((tm, d), row), _const_spec((d, d)),
                  _const_spec((1, d)), _const_spec((d, LANES))],
        out_specs=[pl.BlockSpec((tm, d), row), pl.BlockSpec((tm, d // 2), row), pl.BlockSpec((tm, LANES), row),
                   pl.BlockSpec((SUBLANES, tm), lambda i: (0, i))],
        out_shape=[jax.ShapeDtypeStruct((t, d), f32), jax.ShapeDtypeStruct((t, d // 2), jnp.uint32),
                   jax.ShapeDtypeStruct((t, LANES), f32), jax.ShapeDtypeStruct((SUBLANES, t), f32)],
        compiler_params=_params("parallel"),
        name="oproj_router",
    )(h, attn, w_o.astype(bf16), norm.reshape(1, d), r_pad)


def _sc_gather(x, idx):
    n = idx.shape[0]
    assert x.shape[1] == SC_ROW and n % SC_WINDOW == 0
    mesh = plsc.VectorSubcoreMesh(core_axis_name="core", subcore_axis_name="subcore")

    @pl.kernel(out_type=jax.ShapeDtypeStruct((n, SC_ROW), x.dtype), mesh=mesh, scratch_types=[])
    def gather(x_hbm, i_hbm, o_hbm):
        def body(i_vmem, o_vmem):
            pltpu.sync_copy(x_hbm.at[i_vmem.at[0]], o_vmem)

        pltpu.emit_pipeline(
            body,
            grid=(n // SC_WINDOW,),
            in_specs=[pl.BlockSpec((1, SC_WINDOW), index_map=lambda i: (0, i))],
            out_specs=[pl.BlockSpec((SC_WINDOW, SC_ROW), index_map=lambda i: (i, 0))],
            core_axis_name=("core", "subcore"),
            dimension_semantics=(pltpu.PARALLEL,),
        )(i_hbm, o_hbm)

    return gather(x, idx.reshape(n // LANES, LANES).reshape(1, n))


def _pieces(a):
    n, d = a.shape
    return a.reshape(n // SUBLANES, SUBLANES, d // SC_ROW, SC_ROW).transpose(0, 2, 1, 3).reshape(-1, SC_ROW)


def _unpieces(p, d):
    per = d // SC_ROW
    n = p.shape[0] // per
    return p.reshape(n // SUBLANES, per, SUBLANES, SC_ROW).transpose(0, 2, 1, 3).reshape(n, d)


def _gather_tokens(x, rows):
    n, d = x.shape
    m = rows[0].shape[0]
    per = d // SC_ROW
    assert n % SUBLANES == 0 and m % SUBLANES == 0
    tiled = []
    for r in rows:
        first = (r // SUBLANES * (SUBLANES * per) + r % SUBLANES).reshape(m // SUBLANES, SUBLANES)
        tiled.append(jnp.tile(first, (1, per)))
    chunk = jnp.arange(per * SUBLANES, dtype=jnp.int32) // SUBLANES * SUBLANES
    idx = jnp.concatenate(tiled, axis=1) + jnp.tile(chunk, len(rows))[None, :]
    return _unpieces(_sc_gather(_pieces(x), idx.reshape(-1)), len(rows) * d)


def _expert_kernel(te_ref, tv_ref, x_ref, w1_ref, w3_ref, w2_ref, *rest, nf, first_tile):
    o_ref, xb_ref, acc_ref = rest[-3:]
    i = pl.program_id(0) + first_tile
    f = pl.program_id(1)
    valid = tv_ref[i] > 0

    @pl.when(f == 0)
    def _():
        xb_ref[...] = _unpack_pairs(x_ref[...]).astype(bf16)
        acc_ref[...] = jnp.zeros_like(acc_ref)

    @pl.when(valid)
    def _():
        x = xb_ref[...]
        a = _dot(x, w1_ref[...])
        b = _dot(x, w3_ref[...])
        acc_ref[...] += _dot((jax.nn.silu(a) * b).astype(bf16), w2_ref[...])

    @pl.when(f == nf - 1)
    def _():
        o_ref[...] = _pack_pairs(acc_ref[...])


def _route_tables(choices, n_exp, tm):
    t = choices[0].shape[0]
    nslots = len(choices) * t
    experts = jnp.arange(n_exp, dtype=jnp.int32)[:, None]
    onehots = [(c[None, :] == experts).astype(jnp.int32) for c in choices]
    cums = [jnp.cumsum(oh, axis=1) for oh in onehots]
    totals = [cu[:, -1] for cu in cums]
    counts = sum(totals)
    tiles_e = (counts + tm - 1) // tm
    tile_end = jnp.cumsum(tiles_e)
    tile_start = tile_end - tiles_e
    group_start = jnp.cumsum(counts) - counts
    nt = nslots // tm + n_exp
    total = tile_end[-1]
    ti = jnp.arange(nt, dtype=jnp.int32)
    tv = (ti < total).astype(jnp.int32)
    tc = jnp.minimum(ti, total - 1)
    te = jnp.minimum(jnp.sum((tc[:, None] >= tile_end[None, :]).astype(jnp.int32), axis=1), n_exp - 1)
    order = jnp.argsort(jnp.concatenate(choices), stable=True).astype(jnp.int32)
    lane = jnp.arange(tm, dtype=jnp.int32)[None, :]
    rank = ((tc - tile_start[te]) * tm)[:, None] + lane
    real = (rank < counts[te][:, None]) & (tv[:, None] > 0)
    sidx = jnp.clip(group_start[te][:, None] + rank, 0, nslots - 1)
    src = jnp.where(real, order[sidx] % t, (ti[:, None] * tm + lane) % t).reshape(-1)
    pos = []
    earlier = jnp.zeros((n_exp,), jnp.int32)
    for oh, cu, tot in zip(onehots, cums, totals):
        pos.append(jnp.sum(oh * ((tile_start * tm + earlier)[:, None] + cu - oh), axis=0))
        earlier = earlier + tot
    return te, tv, src, pos, nt


def _experts_layer(hn, choices, w1, w3, w2):
    n_exp, d, ff = w1.shape
    tm = ROW_TILE
    fc = _ff_chunk(ff, EXPERT_FF_CHUNK)
    nf = ff // fc
    te, tv, src, pos, nt = _route_tables(choices, n_exp, tm)
    assert nt % SC_OVERLAP_PARTS == 0
    ntp = nt // SC_OVERLAP_PARTS

    ys = None
    for part in range(SC_OVERLAP_PARTS):
        t0 = part * ntp
        xs = _gather_tokens(hn, [src[t0 * tm:(t0 + ntp) * tm]])

        def fsel(i, f, tv_r, t0=t0):
            return jnp.where(tv_r[i + t0] > 0, f, nf - 1)

        def wmap_in(i, f, te_r, tv_r, t0=t0, fsel=fsel):
            return (te_r[i + t0], 0, fsel(i, f, tv_r))

        def wmap_out(i, f, te_r, tv_r, t0=t0, fsel=fsel):
            return (te_r[i + t0], fsel(i, f, tv_r), 0)

        in_specs = [
            pl.BlockSpec((tm, d // 2), lambda i, f, *_: (i, 0)),
            pl.BlockSpec((None, d, fc), wmap_in),
            pl.BlockSpec((None, d, fc), wmap_in),
            pl.BlockSpec((None, fc, d), wmap_out),
        ]
        args = [te, tv, xs, w1, w3, w2]
        aliases = {}
        if ys is not None:
            in_specs.append(pl.BlockSpec(memory_space=pl.ANY))
            args.append(ys)
            aliases = {len(args) - 1: 0}
        ys = pl.pallas_call(
            functools.partial(_expert_kernel, nf=nf, first_tile=t0),
            grid_spec=pltpu.PrefetchScalarGridSpec(
                num_scalar_prefetch=2,
                grid=(ntp, nf),
                in_specs=in_specs,
                out_specs=pl.BlockSpec((tm, d // 2), lambda i, f, *_, t0=t0: (i + t0, 0)),
                scratch_shapes=[pltpu.VMEM((tm, d), bf16), pltpu.VMEM((tm, d), f32)],
            ),
            out_shape=jax.ShapeDtypeStruct((nt * tm, d // 2), jnp.uint32),
            input_output_aliases=aliases,
            compiler_params=_params("parallel", "arbitrary"),
            name="experts",
        )(*args)
    return ys, pos


def _final_kernel(h_ref, y_ref, route_ref, p_ref, pn_ref, wg_ref, wp_ref, fn_ref, *rest):
    o_ref = rest[-1]
    d = h_ref.shape[1]
    route = route_ref[...]
    y = y_ref[...]
    h5 = h_ref[...] + route[:, 2:3] * _unpack_pairs(y[:, :d // 2]) + route[:, 3:4] * _unpack_pairs(y[:, d // 2:])
    hn = _rms(h5, pn_ref[...]).astype(bf16)
    h6 = h5 + jax.nn.sigmoid(_dot(hn, wg_ref[...])) * _dot(p_ref[...].astype(bf16), wp_ref[...])
    o_ref[...] = _rms(h6, fn_ref[...])


def _final_layer(h, ys, pos, route, p_all, layer, ple_norm, ple_gate, ple_proj, final_norm):
    t, d = h.shape
    pd = p_all.shape[2]
    tm = ROW_TILE
    assert t % (SC_OVERLAP_PARTS * tm) == 0
    tp = t // SC_OVERLAP_PARTS
    out = None
    for part in range(SC_OVERLAP_PARTS):
        r0 = part * tp // tm
        y2 = _gather_tokens(ys, [p[part * tp:(part + 1) * tp] for p in pos])
        row = lambda i, r0=r0: (i + r0, 0)
        in_specs = [pl.BlockSpec((tm, d), row), pl.BlockSpec((tm, y2.shape[1]), lambda i: (i, 0)),
                    pl.BlockSpec((tm, LANES), row),
                    pl.BlockSpec((None, tm, pd), lambda i, r0=r0: (layer, i + r0, 0)),
                    _const_spec((1, d)), _const_spec((d, d)), _const_spec((pd, d)), _const_spec((1, d))]
        args = [h, y2, route, p_all, ple_norm.reshape(1, d), ple_gate.astype(bf16), ple_proj.astype(bf16),
                final_norm.reshape(1, d)]
        aliases = {}
        if out is not None:
            in_specs.append(pl.BlockSpec(memory_space=pl.ANY))
            args.append(out)
            aliases = {len(args) - 1: 0}
        out = pl.pallas_call(
            _final_kernel,
            grid=(tp // tm,),
            in_specs=in_specs,
            out_specs=pl.BlockSpec((tm, d), row),
            out_shape=jax.ShapeDtypeStruct((t, d), f32),
            input_output_aliases=aliases,
            compiler_params=_params("parallel"),
            name="final",
        )(*args)
    return out


def kernel(x, p, pool_norm, pool_w, pool_scale, kv_norm, w_k, w_v, attn_norm, w_q, w_o, ffn_norm, ffn_w1, ffn_w3, ffn_w2, router, exp_w1, exp_w3, exp_w2, ple_norm, ple_gate, ple_proj, final_norm):
    batch, seq, d = x.shape
    t = batch * seq
    assert seq % ROW_TILE == 0 and ROW_TILE % MOBA_BLOCK == 0 and d % HEAD_DIM == 0
    assert p.shape[0] == 2 and router.shape[2] <= LANES
    h = x.reshape(t, d)
    pf = p.reshape(p.shape[0], t, p.shape[-1])

    ew1, ew3, ew2 = _cast_bf16([exp_w1[0], exp_w3[0], exp_w2[0]], EXPERT_CAST_STEPS)
    h = _pool_swiglu_layer(h, pool_norm[0], pool_w[0], pool_scale[0], seq, ew2[0, :2 * SUBLANES, :LANES],
                           ffn_norm[0], ffn_w1[0], ffn_w3[0], ffn_w2[0])
    h, q, k, v, km = _ple_qkv_layer(h, pf, 0, ple_norm[0], ple_gate[0], ple_proj[0],
                                    kv_norm, w_k, w_v, attn_norm[0], w_q[0], seq)
    attn = _attention(q, k, v, km, batch, seq)
    h, hn, route, route_t = _oproj_router_layer(h, attn, w_o[0], ffn_norm[1], router[0])
    choices = [route_t[k].astype(jnp.int32) for k in range(TOP_K_EXPERTS)]
    ys, pos = _experts_layer(hn, choices, ew1, ew3, ew2)
    out = _final_layer(h, ys, pos, route, pf, 1, ple_norm[1], ple_gate[1],
                       ple_proj[1], final_norm)
    return out.reshape(batch, seq, d)
```

```python
import functools

import jax
import jax.numpy as jnp
from jax import lax
from jax.experimental import pallas as pl
from jax.experimental.pallas import tpu as pltpu
from jax.experimental.pallas import tpu_sc as plsc

POOL_WINDOWS = (2, 4, 8, 16)
HEAD_DIM = 128
MOBA_BLOCK = 256
MOBA_TOPK = 3
ROPE_THETA = 500000.0
ROPE_DIM = HEAD_DIM // 4
TOP_K_EXPERTS = 2
RMS_EPS = 1e-6
NEG_INF = -1e30
REMOVED = -3e38

LANES = 128
SUBLANES = 8
ROW_TILE = 512
POOL_SUB = 128
ATTN_HEADS_PER_STEP = 4
ATTN_BLOCKS_PER_TRIP = 4
SUM_ROWS = 16
LOG2E = 1.4426950408889634
EXPERT_FF_CHUNK = 1792
EXPERT_CAST_STEPS = 32
SC_OVERLAP_PARTS = 4
SC_ROW = 128
SC_WINDOW = 128
VMEM_LIMIT = 56 * 1024 * 1024

bf16 = jnp.bfloat16
f32 = jnp.float32


def _dot(a, b):
    return jnp.dot(a, b, preferred_element_type=f32)


def _dot_split(a, b):
    a_hi = a.astype(bf16)
    a_lo = (a - a_hi.astype(f32)).astype(bf16)
    b_hi = b.astype(bf16)
    b_lo = (b - b_hi.astype(f32)).astype(bf16)
    return _dot(a_hi, b_hi) + (_dot(a_lo, b_hi) + _dot(a_hi, b_lo))


def _pack_pairs(x):
    half = x.shape[1] // 2
    lo = lax.bitcast_convert_type(x[:, :half].astype(bf16).astype(f32), jnp.uint32)
    hi = lax.bitcast_convert_type(x[:, half:].astype(bf16).astype(f32), jnp.uint32)
    return (hi & jnp.uint32(0xFFFF0000)) | (lo >> 16)


def _unpack_pairs(w):
    lo = lax.bitcast_convert_type(w << 16, f32)
    hi = lax.bitcast_convert_type(w & jnp.uint32(0xFFFF0000), f32)
    return jnp.concatenate([lo, hi], axis=1)


def _cast_kernel(*refs):
    n = len(refs) // 2
    for src, dst in zip(refs[:n], refs[n:]):
        dst[...] = src[...].astype(bf16)


def _cast_bf16(ws, steps):
    flat = [w.reshape(-1, w.shape[-1]) for w in ws]
    for a in flat:
        assert a.shape[0] % steps == 0 and (a.shape[0] // steps) % (2 * SUBLANES) == 0
    specs = [pl.BlockSpec((a.shape[0] // steps, a.shape[1]), lambda s: (s, 0)) for a in flat]
    outs = pl.pallas_call(
        _cast_kernel,
        grid=(steps,),
        in_specs=specs,
        out_specs=specs,
        out_shape=[jax.ShapeDtypeStruct(a.shape, bf16) for a in flat],
        compiler_params=_params("parallel"),
        name="cast_bf16",
    )(*flat)
    return [o.reshape(w.shape) for o, w in zip(outs, ws)]


def _rms(x, g):
    var = jnp.mean(x * x, axis=-1, keepdims=True)
    return x * lax.rsqrt(var + RMS_EPS) * g


def _params(*sem):
    return pltpu.CompilerParams(dimension_semantics=sem, vmem_limit_bytes=VMEM_LIMIT)


def _const_spec(shape):
    nd = len(shape)
    return pl.BlockSpec(shape, lambda *_: (0,) * nd)


def _pool_kernel(x_ref, halo_ref, g_ref, pw_ref, ps_ref, after_ref, o_ref, pooled_ref, *, ts, seq):
    del after_ref
    i = pl.program_id(0)
    g = g_ref[...]
    x = x_ref[...]
    xn = _rms(x, g)
    keep = jnp.where((i * ts) % seq == 0, 0.0, 1.0)
    hnb = (_rms(halo_ref[...], g) * keep).astype(bf16)
    xnb = xn.astype(bf16)
    gd = x.shape[1] // len(POOL_WINDOWS)
    r = lax.broadcasted_iota(jnp.int32, (POOL_SUB, 2 * POOL_SUB), 0)
    c = lax.broadcasted_iota(jnp.int32, (POOL_SUB, 2 * POOL_SUB), 1)
    dist = r + POOL_SUB - c
    bands = [((dist >= 0) & (dist < w)).astype(bf16) for w in POOL_WINDOWS]
    rows = lax.broadcasted_iota(jnp.int32, (POOL_SUB, 1), 0)
    for sb in range(ts // POOL_SUB):
        lo, hi = sb * POOL_SUB, (sb + 1) * POOL_SUB
        prev = hnb if sb == 0 else xnb[lo - POOL_SUB:lo]
        ext = jnp.concatenate([prev, xnb[lo:hi]], axis=0)
        tpos = (i * ts + lo) % seq + rows
        for gi, w in enumerate(POOL_WINDOWS):
            cs = slice(gi * gd, (gi + 1) * gd)
            wsum = _dot(bands[gi], ext[:, cs])
            inv_cnt = 1.0 / jnp.minimum(tpos + 1, w).astype(f32)
            pooled_ref[lo:hi, cs] = (wsum * inv_cnt - xn[lo:hi, cs]).astype(bf16)
    for gi in range(len(POOL_WINDOWS)):
        cs = slice(gi * gd, (gi + 1) * gd)
        mixed = _dot(pooled_ref[:, cs], pw_ref[gi])
        o_ref[:, cs] = x[:, cs] + mixed * ps_ref[:, cs]


def _swiglu_kernel(h_ref, g_ref, w1_ref, w3_ref, w2_ref, o_ref):
    x = h_ref[...]
    hn = _rms(x, g_ref[...]).astype(bf16)
    a = _dot(hn, w1_ref[...])
    b = _dot(hn, w3_ref[...])
    o_ref[...] = x + _dot((jax.nn.silu(a) * b).astype(bf16), w2_ref[...])


def _ff_chunk(ff, target):
    units = ff // LANES
    best = 1
    for k in range(1, units + 1):
        if units % k == 0 and k * LANES <= target:
            best = k
    return best * LANES


def _pool_swiglu_kernel(x_ref, halo_ref, g_ref, pw_ref, ps_ref, after_ref, g2_ref, w1_ref, w3_ref, w2_ref,
                        o_ref, pooled_ref, h1_ref, *, ts, seq):
    _pool_kernel(x_ref, halo_ref, g_ref, pw_ref, ps_ref, after_ref, h1_ref, pooled_ref, ts=ts, seq=seq)
    _swiglu_kernel(h1_ref, g2_ref, w1_ref, w3_ref, w2_ref, o_ref)


def _pool_swiglu_layer(h, pool_norm, pool_w, pool_scale, seq, after, ffn_norm, w1, w3, w2):
    t, d = h.shape
    ff = w1.shape[1]
    ts = ROW_TILE
    ng = len(POOL_WINDOWS)
    gd = d // ng
    per = ts // POOL_SUB
    resident = lambda shape: pl.BlockSpec(shape, lambda i: (0,) * len(shape), pipeline_mode=pl.Buffered(1))
    return pl.pallas_call(
        functools.partial(_pool_swiglu_kernel, ts=ts, seq=seq),
        grid=(t // ts,),
        in_specs=[
            pl.BlockSpec((ts, d), lambda i: (i, 0)),
            pl.BlockSpec((POOL_SUB, d), lambda i: (jnp.maximum(i * per - 1, 0), 0)),
            _const_spec((1, d)),
            resident((ng, gd, gd)),
            _const_spec((1, d)),
            _const_spec(after.shape),
            _const_spec((1, d)),
            resident((d, ff)), resident((d, ff)), resident((ff, d)),
        ],
        out_specs=pl.BlockSpec((ts, d), lambda i: (i, 0)),
        out_shape=jax.ShapeDtypeStruct((t, d), f32),
        scratch_shapes=[pltpu.VMEM((ts, d), bf16), pltpu.VMEM((ts, d), f32)],
        compiler_params=_params("parallel"),
        name="pool_swiglu",
    )(h, h, pool_norm.reshape(1, d), pool_w.astype(bf16), pool_scale.reshape(1, d), after,
      ffn_norm.reshape(1, d), w1.astype(bf16), w3.astype(bf16), w2.astype(bf16))


def _rope_tables(seq):
    half = ROPE_DIM // 2
    inv_freq = jnp.float32(ROPE_THETA) ** (-(jnp.arange(0, ROPE_DIM, 2, dtype=f32) / ROPE_DIM))
    ang = jnp.arange(seq, dtype=f32)[:, None] * inv_freq[None, :]
    cos, sin = jnp.cos(ang), jnp.sin(ang)
    ones = jnp.ones((seq, HEAD_DIM - ROPE_DIM), f32)
    zeros = jnp.zeros((seq, HEAD_DIM - half), f32)
    cos_t = jnp.concatenate([cos, cos, ones], axis=1)
    up_t = jnp.concatenate([-sin, zeros], axis=1)
    dn_t = jnp.concatenate([jnp.zeros((seq, half), f32), sin, jnp.zeros((seq, HEAD_DIM - ROPE_DIM), f32)], axis=1)
    return cos_t, up_t, dn_t


def _rope(xh, cos_t, up_t, dn_t):
    half = ROPE_DIM // 2
    return (xh * cos_t + pltpu.roll(xh, HEAD_DIM - half, 1) * up_t + pltpu.roll(xh, half, 1) * dn_t)


def _ple_qkv_kernel(h_ref, p_ref, pn_ref, wg_ref, wp_ref, kvn_ref, wk_ref, wv_ref, an_ref, wq_ref,
                    cos_ref, up_ref, dn_ref, h_out, q_out, k_out, v_out, km_out):
    h = h_ref[...]
    hn = _rms(h, pn_ref[...]).astype(bf16)
    h3 = h + jax.nn.sigmoid(_dot(hn, wg_ref[...])) * _dot(p_ref[...].astype(bf16), wp_ref[...])
    h_out[...] = h3
    base = h3 * lax.rsqrt(jnp.mean(h3 * h3, axis=-1, keepdims=True) + RMS_EPS)
    kn = (base * kvn_ref[...]).astype(bf16)
    qn = (base * an_ref[...]).astype(bf16)
    v = _dot(kn, wv_ref[...])
    for bi in range(h.shape[0] // MOBA_BLOCK):
        v_out[bi] = v[bi * MOBA_BLOCK:(bi + 1) * MOBA_BLOCK].T.astype(bf16)
    k = _dot(kn, wk_ref[...])
    q = _dot(qn, wq_ref[...])
    cos_t, up_t, dn_t = cos_ref[...], up_ref[...], dn_ref[...]
    tm, d = h.shape
    sub = km_out.shape[0] // (tm // MOBA_BLOCK)
    for hh in range(d // HEAD_DIM):
        cs = slice(hh * HEAD_DIM, (hh + 1) * HEAD_DIM)
        q_out[:, cs] = _rope(q[:, cs], cos_t, up_t, dn_t)
        kr = _rope(k[:, cs], cos_t, up_t, dn_t)
        k_out[:, cs] = kr.astype(bf16)
        for bi in range(tm // MOBA_BLOCK):
            m = jnp.mean(kr[bi * MOBA_BLOCK:(bi + 1) * MOBA_BLOCK], axis=0, keepdims=True)
            km_out[bi * sub:(bi + 1) * sub, cs] = jnp.broadcast_to(m, (sub, HEAD_DIM))


def _layer_rows_spec(p_all, layer, tm):
    return pl.BlockSpec((None, tm, p_all.shape[2]), lambda i: (layer, i, 0))


def _ple_qkv_layer(h, p_all, layer, ple_norm, ple_gate, ple_proj, kv_norm, w_k, w_v, attn_norm, w_q, seq):
    t, d = h.shape
    pd = p_all.shape[2]
    tm = ROW_TILE
    sub = SUBLANES
    cos_t, up_t, dn_t = _rope_tables(seq)
    tiles_per_seq = seq // tm
    row = lambda i: (i, 0)
    tab = pl.BlockSpec((tm, HEAD_DIM), lambda i: (i % tiles_per_seq, 0))
    nkm = t // MOBA_BLOCK * sub
    outs = pl.pallas_call(
        _ple_qkv_kernel,
        grid=(t // tm,),
        in_specs=[
            pl.BlockSpec((tm, d), row), _layer_rows_spec(p_all, layer, tm),
            _const_spec((1, d)), _const_spec((d, d)), _const_spec((pd, d)),
            _const_spec((1, d)), _const_spec((d, d)), _const_spec((d, d)),
            _const_spec((1, d)), _const_spec((d, d)),
            tab, tab, tab,
        ],
        out_specs=[
            pl.BlockSpec((tm, d), row), pl.BlockSpec((tm, d), row),
            pl.BlockSpec((tm, d), row),
            pl.BlockSpec((tm // MOBA_BLOCK, d, MOBA_BLOCK), lambda i: (i, 0, 0)),
            pl.BlockSpec((tm // MOBA_BLOCK * sub, d), row),
        ],
        out_shape=[
            jax.ShapeDtypeStruct((t, d), f32), jax.ShapeDtypeStruct((t, d), f32),
            jax.ShapeDtypeStruct((t, d), bf16),
            jax.ShapeDtypeStruct((t // MOBA_BLOCK, d, MOBA_BLOCK), bf16),
            jax.ShapeDtypeStruct((nkm, d), f32),
        ],
        compiler_params=_params("parallel"),
        name="ple_qkv",
    )(h, p_all, ple_norm.reshape(1, d), ple_gate.astype(bf16), ple_proj.astype(bf16),
      kv_norm.reshape(1, d), w_k.astype(bf16), w_v.astype(bf16),
      attn_norm.reshape(1, d), w_q.astype(bf16), cos_t, up_t, dn_t)
    h3, q, k, v, km = outs
    return h3, q, k, v, km.reshape(t // MOBA_BLOCK, sub, d)[:, 0, :]


def _attn_kernel(q_ref, k_ref, vt_ref, km_ref, o_ref, qa_ref, sa_ref, sb_ref, pa_ref, pb_ref, acc_ref, *, hg):
    j = pl.program_id(2)
    nb = km_ref.shape[0]
    bs = q_ref.shape[0]
    hd = HEAD_DIM
    qscale = (hd ** -0.5) * LOG2E
    blk = lax.broadcasted_iota(jnp.int32, (nb, bs), 0).astype(f32)
    jf = j.astype(f32)
    krow = lax.broadcasted_iota(jnp.int32, (bs, bs), 0)
    qcol = lax.broadcasted_iota(jnp.int32, (bs, bs), 1)
    start_j = pl.multiple_of(j * bs, bs)

    ones_rows = jnp.ones((SUM_ROWS, bs), bf16)

    def weighted_values(n, h, pb):
        return _dot(jnp.concatenate([vt_ref[n, h * hd:(h + 1) * hd, :], ones_rows], axis=0), pb)

    heads = [slice(h * hd, (h + 1) * hd) for h in range(hg)]
    qfts = [q_ref[:, cs].T for cs in heads]
    qbts = [(qft * qscale).astype(bf16) for qft in qfts]
    diag = [_dot(k_ref[pl.ds(start_j, bs), cs], qbt) for cs, qbt in zip(heads, qbts)]
    gates = [_dot_split(km_ref[:, cs], qft) for cs, qft in zip(heads, qfts)]
    cands = [jnp.where(blk < jf, gate, NEG_INF) for gate in gates]
    sels = [blk < 0.0] * hg
    for _ in range(min(MOBA_TOPK, nb)):
        for h in range(hg):
            mx = jnp.max(cands[h], axis=0, keepdims=True)
            pick = jnp.min(jnp.where(cands[h] == mx, blk, float(nb)), axis=0, keepdims=True)
            hit = blk == pick
            sels[h] = sels[h] | hit
            cands[h] = jnp.where(hit, REMOVED, cands[h])
    init = []
    for h in range(hg):
        bias = jnp.where(sels[h] & (blk < jf), 0.0, NEG_INF)
        if nb < hd:
            bias = jnp.concatenate([bias, jnp.zeros((hd - nb, bs), f32)], axis=0)
        qa_ref[h] = jnp.concatenate([qbts[h], bias.astype(bf16)], axis=0)
        s = jnp.where(krow <= qcol, diag[h], NEG_INF)
        m0 = jnp.max(s, axis=0, keepdims=True)
        acc_ref[h] = weighted_values(j, h, jnp.exp2((s - m0).astype(bf16)))
        init += [m0, jnp.ones_like(m0)]

    lane = lax.broadcasted_iota(jnp.int32, (bs, hd), 1)

    def block_scores(n, h):
        start = pl.multiple_of(n * bs, bs)
        onehot = (lane == n).astype(bf16)
        ka = jnp.concatenate([k_ref[pl.ds(start, bs), h * hd:(h + 1) * hd], onehot], axis=1)
        return _dot(ka, qa_ref[h])

    def stage(n, s_in, s_out, p_prev, p_out, stats):
        nn = jnp.minimum(n + 1, nb - 1)
        prev = jnp.clip(n - 1, 0, nb - 1)
        new = []
        for h in range(hg):
            m, alpha_p = stats[2 * h:2 * h + 2]
            acc_ref[h] = alpha_p * acc_ref[h] + weighted_values(prev, h, p_prev[h])
            s = s_in[h]
            m_new = jnp.maximum(m, jnp.max(s, axis=0, keepdims=True))
            p_out[h] = jnp.exp2((s - m_new).astype(bf16))
            s_out[h] = block_scores(nn, h)
            new += [m_new, jnp.exp2(m - m_new)]
        return new

    per_trip = ATTN_BLOCKS_PER_TRIP

    def body(i, stats):
        for u in range(0, per_trip, 2):
            stats = stage(per_trip * i + u, sa_ref, sb_ref, pb_ref, pa_ref, stats)
            stats = stage(per_trip * i + u + 1, sb_ref, sa_ref, pa_ref, pb_ref, stats)
        return tuple(stats)

    for h in range(hg):
        sa_ref[h] = block_scores(0, h)
        pb_ref[h] = jnp.zeros((bs, bs), bf16)
    trips = (j + per_trip - 1) // per_trip
    res = lax.fori_loop(0, trips, body, tuple(init))
    last = jnp.clip(per_trip * trips - 1, 0, nb - 1)
    tails = [weighted_values(last, h, pb_ref[h]) for h in range(hg)]
    for h in range(hg):
        acc = res[2 * h + 1] * acc_ref[h] + tails[h]
        o_ref[:, h * hd:(h + 1) * hd] = (acc[:hd] * (1.0 / acc[hd:hd + 1])).T.astype(o_ref.dtype)


def _attention(q, k, vt, km, batch, seq):
    t, d = q.shape
    nh = d // HEAD_DIM
    nb = seq // MOBA_BLOCK
    hg = ATTN_HEADS_PER_STEP
    assert nb <= HEAD_DIM and nh % hg == 0
    w = hg * HEAD_DIM
    return pl.pallas_call(
        functools.partial(_attn_kernel, hg=hg),
        grid=(batch, nh // hg, nb),
        in_specs=[
            pl.BlockSpec((MOBA_BLOCK, w), lambda b, g, j: (b * nb + j, g)),
            pl.BlockSpec((seq, w), lambda b, g, j: (b, g), pipeline_mode=pl.Buffered(1)),
            pl.BlockSpec((nb, w, MOBA_BLOCK), lambda b, g, j: (b, g, 0), pipeline_mode=pl.Buffered(1)),
            pl.BlockSpec((nb, w), lambda b, g, j: (b, g)),
        ],
        out_specs=pl.BlockSpec((MOBA_BLOCK, w), lambda b, g, j: (b * nb + j, g)),
        out_shape=jax.ShapeDtypeStruct((t, d), bf16),
        scratch_shapes=[
            pltpu.VMEM((hg, 2 * HEAD_DIM, MOBA_BLOCK), bf16),
            pltpu.VMEM((hg, MOBA_BLOCK, MOBA_BLOCK), f32), pltpu.VMEM((hg, MOBA_BLOCK, MOBA_BLOCK), f32),
            pltpu.VMEM((hg, MOBA_BLOCK, MOBA_BLOCK), bf16), pltpu.VMEM((hg, MOBA_BLOCK, MOBA_BLOCK), bf16),
            pltpu.VMEM((hg, HEAD_DIM + SUM_ROWS, MOBA_BLOCK), f32),
        ],
        compiler_params=_params("parallel", "parallel", "arbitrary"),
        name="attn",
    )(q, k, vt, km)


def _oproj_router_kernel(h_ref, a_ref, wo_ref, g_ref, r_ref, h_out, hn_out, route_out, route_t_out, *, n_exp):
    h4 = h_ref[...] + _dot(a_ref[...], wo_ref[...])
    h_out[...] = h4
    hn = _rms(h4, g_ref[...])
    hn_out[...] = _pack_pairs(hn)
    logits = _dot_split(hn, r_ref[...])
    lane = lax.broadcasted_iota(jnp.int32, logits.shape, 1).astype(f32)
    cand = jnp.where(lane < n_exp, logits, NEG_INF)
    m1 = jnp.max(cand, axis=1, keepdims=True)
    i1 = jnp.min(jnp.where(cand == m1, lane, float(LANES)), axis=1, keepdims=True)
    cand = jnp.where(lane == i1, REMOVED, cand)
    m2 = jnp.max(cand, axis=1, keepdims=True)
    i2 = jnp.min(jnp.where(cand == m2, lane, float(LANES)), axis=1, keepdims=True)
    e2 = jnp.exp(m2 - m1)
    den = 1.0 + e2
    route = jnp.where(lane == 0, i1, jnp.where(lane == 1, i2, jnp.where(
        lane == 2, 1.0 / den, jnp.where(lane == 3, e2 / den, 0.0))))
    route_out[...] = route
    route_t_out[...] = route.T[:SUBLANES]


def _oproj_router_layer(h, attn, w_o, norm, router):
    t, d = h.shape
    tm = ROW_TILE
    n_exp = router.shape[1]
    r_pad = jnp.zeros((d, LANES), f32).at[:, :n_exp].set(router)
    row = lambda i: (i, 0)
    return pl.pallas_call(
        functools.partial(_oproj_router_kernel, n_exp=n_exp),
        grid=(t // tm,),
        in_specs=[pl.BlockSpec((tm, d), row), pl.BlockSpec((tm, d), row), _const_spec((d, d)),
                  _const_spec((1, d)), _const_spec((d, LANES))],
        out_specs=[pl.BlockSpec((tm, d), row), pl.BlockSpec((tm, d // 2), row), pl.BlockSpec((tm, LANES), row),
                   pl.BlockSpec((SUBLANES, tm), lambda i: (0, i))],
        out_shape=[jax.ShapeDtypeStruct((t, d), f32), jax.ShapeDtypeStruct((t, d // 2), jnp.uint32),
                   jax.ShapeDtypeStruct((t, LANES), f32), jax.ShapeDtypeStruct((SUBLANES, t), f32)],
        compiler_params=_params("parallel"),
        name="oproj_router",
    )(h, attn, w_o.astype(bf16), norm.reshape(1, d), r_pad)


def _sc_gather(x, idx):
    n = idx.shape[0]
    assert x.shape[1] == SC_ROW and n % SC_WINDOW == 0
    mesh = plsc.VectorSubcoreMesh(core_axis_name="core", subcore_axis_name="subcore")

    @pl.kernel(out_type=jax.ShapeDtypeStruct((n, SC_ROW), x.dtype), mesh=mesh, scratch_types=[])
    def gather(x_hbm, i_hbm, o_hbm):
        def body(i_vmem, o_vmem):
            pltpu.sync_copy(x_hbm.at[i_vmem.at[0]], o_vmem)

        pltpu.emit_pipeline(
            body,
            grid=(n // SC_WINDOW,),
            in_specs=[pl.BlockSpec((1, SC_WINDOW), index_map=lambda i: (0, i))],
            out_specs=[pl.BlockSpec((SC_WINDOW, SC_ROW), index_map=lambda i: (i, 0))],
            core_axis_name=("core", "subcore"),
            dimension_semantics=(pltpu.PARALLEL,),
        )(i_hbm, o_hbm)

    return gather(x, idx.reshape(n // LANES, LANES).reshape(1, n))


def _pieces(a):
    n, d = a.shape
    return a.reshape(n // SUBLANES, SUBLANES, d // SC_ROW, SC_ROW).transpose(0, 2, 1, 3).reshape(-1, SC_ROW)


def _unpieces(p, d):
    per = d // SC_ROW
    n = p.shape[0] // per
    return p.reshape(n // SUBLANES, per, SUBLANES, SC_ROW).transpose(0, 2, 1, 3).reshape(n, d)


def _gather_tokens(x, rows):
    n, d = x.shape
    m = rows[0].shape[0]
    per = d // SC_ROW
    assert n % SUBLANES == 0 and m % SUBLANES == 0
    tiled = []
    for r in rows:
        first = (r // SUBLANES * (SUBLANES * per) + r % SUBLANES).reshape(m // SUBLANES, SUBLANES)
        tiled.append(jnp.tile(first, (1, per)))
    chunk = jnp.arange(per * SUBLANES, dtype=jnp.int32) // SUBLANES * SUBLANES
    idx = jnp.concatenate(tiled, axis=1) + jnp.tile(chunk, len(rows))[None, :]
    return _unpieces(_sc_gather(_pieces(x), idx.reshape(-1)), len(rows) * d)


def _expert_kernel(te_ref, tv_ref, x_ref, w1_ref, w3_ref, w2_ref, *rest, nf, first_tile):
    o_ref, xb_ref, acc_ref = rest[-3:]
    i = pl.program_id(0) + first_tile
    f = pl.program_id(1)
    valid = tv_ref[i] > 0

    @pl.when(f == 0)
    def _():
        xb_ref[...] = _unpack_pairs(x_ref[...]).astype(bf16)
        acc_ref[...] = jnp.zeros_like(acc_ref)

    @pl.when(valid)
    def _():
        x = xb_ref[...]
        a = _dot(x, w1_ref[...])
        b = _dot(x, w3_ref[...])
        acc_ref[...] += _dot((jax.nn.silu(a) * b).astype(bf16), w2_ref[...])

    @pl.when(f == nf - 1)
    def _():
        o_ref[...] = _pack_pairs(acc_ref[...])


def _route_tables(choices, n_exp, tm):
    t = choices[0].shape[0]
    nslots = len(choices) * t
    experts = jnp.arange(n_exp, dtype=jnp.int32)[:, None]
    onehots = [(c[None, :] == experts).astype(jnp.int32) for c in choices]
    cums = [jnp.cumsum(oh, axis=1) for oh in onehots]
    totals = [cu[:, -1] for cu in cums]
    counts = sum(totals)
    tiles_e = (counts + tm - 1) // tm
    tile_end = jnp.cumsum(tiles_e)
    tile_start = tile_end - tiles_e
    group_start = jnp.cumsum(counts) - counts
    nt = nslots // tm + n_exp
    total = tile_end[-1]
    ti = jnp.arange(nt, dtype=jnp.int32)
    tv = (ti < total).astype(jnp.int32)
    tc = jnp.minimum(ti, total - 1)
    te = jnp.minimum(jnp.sum((tc[:, None] >= tile_end[None, :]).astype(jnp.int32), axis=1), n_exp - 1)
    order = jnp.argsort(jnp.concatenate(choices), stable=True).astype(jnp.int32)
    lane = jnp.arange(tm, dtype=jnp.int32)[None, :]
    rank = ((tc - tile_start[te]) * tm)[:, None] + lane
    real = (rank < counts[te][:, None]) & (tv[:, None] > 0)
    sidx = jnp.clip(group_start[te][:, None] + rank, 0, nslots - 1)
    src = jnp.where(real, order[sidx] % t, (ti[:, None] * tm + lane) % t).reshape(-1)
    pos = []
    earlier = jnp.zeros((n_exp,), jnp.int32)
    for oh, cu, tot in zip(onehots, cums, totals):
        pos.append(jnp.sum(oh * ((tile_start * tm + earlier)[:, None] + cu - oh), axis=0))
        earlier = earlier + tot
    return te, tv, src, pos, nt


def _experts_layer(hn, choices, w1, w3, w2):
    n_exp, d, ff = w1.shape
    tm = ROW_TILE
    fc = _ff_chunk(ff, EXPERT_FF_CHUNK)
    nf = ff // fc
    te, tv, src, pos, nt = _route_tables(choices, n_exp, tm)
    assert nt % SC_OVERLAP_PARTS == 0
    ntp = nt // SC_OVERLAP_PARTS

    ys = None
    for part in range(SC_OVERLAP_PARTS):
        t0 = part * ntp
        xs = _gather_tokens(hn, [src[t0 * tm:(t0 + ntp) * tm]])

        def fsel(i, f, tv_r, t0=t0):
            return jnp.where(tv_r[i + t0] > 0, f, nf - 1)

        def wmap_in(i, f, te_r, tv_r, t0=t0, fsel=fsel):
            return (te_r[i + t0], 0, fsel(i, f, tv_r))

        def wmap_out(i, f, te_r, tv_r, t0=t0, fsel=fsel):
            return (te_r[i + t0], fsel(i, f, tv_r), 0)

        in_specs = [
            pl.BlockSpec((tm, d // 2), lambda i, f, *_: (i, 0)),
            pl.BlockSpec((None, d, fc), wmap_in),
            pl.BlockSpec((None, d, fc), wmap_in),
            pl.BlockSpec((None, fc, d), wmap_out),
        ]
        args = [te, tv, xs, w1, w3, w2]
        aliases = {}
        if ys is not None:
            in_specs.append(pl.BlockSpec(memory_space=pl.ANY))
            args.append(ys)
            aliases = {len(args) - 1: 0}
        ys = pl.pallas_call(
            functools.partial(_expert_kernel, nf=nf, first_tile=t0),
            grid_spec=pltpu.PrefetchScalarGridSpec(
                num_scalar_prefetch=2,
                grid=(ntp, nf),
                in_specs=in_specs,
                out_specs=pl.BlockSpec((tm, d // 2), lambda i, f, *_, t0=t0: (i + t0, 0)),
                scratch_shapes=[pltpu.VMEM((tm, d), bf16), pltpu.VMEM((tm, d), f32)],
            ),
            out_shape=jax.ShapeDtypeStruct((nt * tm, d // 2), jnp.uint32),
            input_output_aliases=aliases,
            compiler_params=_params("parallel", "arbitrary"),
            name="experts",
        )(*args)
    return ys, pos


def _final_kernel(h_ref, y_ref, route_ref, p_ref, pn_ref, wg_ref, wp_ref, fn_ref, *rest):
    o_ref = rest[-1]
    d = h_ref.shape[1]
    route = route_ref[...]
    y = y_ref[...]
    h5 = h_ref[...] + route[:, 2:3] * _unpack_pairs(y[:, :d // 2]) + route[:, 3:4] * _unpack_pairs(y[:, d // 2:])
    hn = _rms(h5, pn_ref[...]).astype(bf16)
    h6 = h5 + jax.nn.sigmoid(_dot(hn, wg_ref[...])) * _dot(p_ref[...].astype(bf16), wp_ref[...])
    o_ref[...] = _rms(h6, fn_ref[...])


def _final_layer(h, ys, pos, route, p_all, layer, ple_norm, ple_gate, ple_proj, final_norm):
    t, d = h.shape
    pd = p_all.shape[2]
    tm = ROW_TILE
    assert t % (SC_OVERLAP_PARTS * tm) == 0
    tp = t // SC_OVERLAP_PARTS
    out = None
    for part in range(SC_OVERLAP_PARTS):
        r0 = part * tp // tm
        y2 = _gather_tokens(ys, [p[part * tp:(part + 1) * tp] for p in pos])
        row = lambda i, r0=r0: (i + r0, 0)
        in_specs = [pl.BlockSpec((tm, d), row), pl.BlockSpec((tm, y2.shape[1]), lambda i: (i, 0)),
                    pl.BlockSpec((tm, LANES), row),
                    pl.BlockSpec((None, tm, pd), lambda i, r0=r0: (layer, i + r0, 0)),
                    _const_spec((1, d)), _const_spec((d, d)), _const_spec((pd, d)), _const_spec((1, d))]
        args = [h, y2, route, p_all, ple_norm.reshape(1, d), ple_gate.astype(bf16), ple_proj.astype(bf16),
                final_norm.reshape(1, d)]
        aliases = {}
        if out is not None:
            in_specs.append(pl.BlockSpec(memory_space=pl.ANY))
            args.append(out)
            aliases = {len(args) - 1: 0}
        out = pl.pallas_call(
            _final_kernel,
            grid=(tp // tm,),
            in_specs=in_specs,
            out_specs=pl.BlockSpec((tm, d), row),
            out_shape=jax.ShapeDtypeStruct((t, d), f32),
            input_output_aliases=aliases,
            compiler_params=_params("parallel"),
            name="final",
        )(*args)
    return out


def kernel(x, p, pool_norm, pool_w, pool_scale, kv_norm, w_k, w_v, attn_norm, w_q, w_o, ffn_norm, ffn_w1, ffn_w3, ffn_w2, router, exp_w1, exp_w3, exp_w2, ple_norm, ple_gate, ple_proj, final_norm):
    batch, seq, d = x.shape
    t = batch * seq
    assert seq % ROW_TILE == 0 and ROW_TILE % MOBA_BLOCK == 0 and d % HEAD_DIM == 0
    assert p.shape[0] == 2 and router.shape[2] <= LANES
    h = x.reshape(t, d)
    pf = p.reshape(p.shape[0], t, p.shape[-1])

    ew1, ew3, ew2 = _cast_bf16([exp_w1[0], exp_w3[0], exp_w2[0]], EXPERT_CAST_STEPS)
    h = _pool_swiglu_layer(h, pool_norm[0], pool_w[0], pool_scale[0], seq, ew2[0, :2 * SUBLANES, :LANES],
                           ffn_norm[0], ffn_w1[0], ffn_w3[0], ffn_w2[0])
    h, q, k, v, km = _ple_qkv_layer(h, pf, 0, ple_norm[0], ple_gate[0], ple_proj[0],
                                    kv_norm, w_k, w_v, attn_norm[0], w_q[0], seq)
    attn = _attention(q, k, v, km, batch, seq)
    h, hn, route, route_t = _oproj_router_layer(h, attn, w_o[0], ffn_norm[1], router[0])
    choices = [route_t[k].astype(jnp.int32) for k in range(TOP_K_EXPERTS)]
    ys, pos = _experts_layer(hn, choices, ew1, ew3, ew2)
    out = _final_layer(h, ys, pos, route, pf, 1, ple_norm[1], ple_gate[1],
                       ple_proj[1], final_norm)
    return out.reshape(batch, seq, d)
```
